```python
import jax, jax.numpy as jnp
from jax import lax
import numpy as np

D_MODEL = 2048
BATCH = 8
SEQ = 4096
DEPTH = 2

N_META = 16
HEAD_DIM = 128
N_HEADS_SB = D_MODEL // (2 * HEAD_DIM)
N_HEADS_FOX = D_MODEL // (2 * HEAD_DIM)
W_SB = N_HEADS_SB * HEAD_DIM
W_FOX = N_HEADS_FOX * HEAD_DIM
W_MIX = W_SB + W_FOX
N_IN = 3 * W_SB + 3 * W_FOX + N_HEADS_FOX
D_FF = 11 * D_MODEL // 4
CONV_WIDTH = 3
Q_BLOCK = 128
EPS = 1e-6

kernel_name = "hymba_stickbreak_fox_convffn"


def rms_norm(x, g):
    xf = x.astype(jnp.float32)
    y = xf * lax.rsqrt(jnp.mean(xf * xf, axis=-1, keepdims=True) + EPS)
    return (y * g.astype(jnp.float32)).astype(x.dtype)


def block_bounds():
    bounds = [(0, N_META)]
    for i in range(SEQ // Q_BLOCK):
        bounds.append((N_META + i * Q_BLOCK, N_META + (i + 1) * Q_BLOCK))
    return bounds


def stick_breaking_attention(q, k, v):
    scale = HEAD_DIM ** -0.5
    outs = []
    for qs, qe in block_bounds():
        z = jnp.einsum('bqhd,bkhd->bhqk', q[:, qs:qe], k[:, :qe]).astype(jnp.float32) * scale
        t_pos = jnp.arange(qs, qe)[:, None]
        s_pos = jnp.arange(qe)[None, :]
        before = s_pos < t_pos
        log_keep = jnp.where(before, -jax.nn.softplus(z), 0.0)
        log_keep_between = lax.cumsum(log_keep, axis=3, reverse=True) - log_keep
        a = jnp.where(before, jnp.exp(jax.nn.log_sigmoid(z) + log_keep_between), 0.0)
        outs.append(jnp.einsum('bhqk,bkhd->bqhd', a.astype(v.dtype), v[:, :qe]))
    return jnp.concatenate(outs, axis=1)


def forgetting_attention(q, k, v, log_f):
    scale = HEAD_DIM ** -0.5
    c = jnp.transpose(jnp.cumsum(log_f, axis=1), (0, 2, 1))
    outs = []
    for qs, qe in block_bounds():
        logits = jnp.einsum('bqhd,bkhd->bhqk', q[:, qs:qe], k[:, :qe]).astype(jnp.float32) * scale
        logits = logits + (c[:, :, qs:qe, None] - c[:, :, None, :qe])
        t_pos = jnp.arange(qs, qe)[:, None]
        s_pos = jnp.arange(qe)[None, :]
        logits = jnp.where(s_pos <= t_pos, logits, -jnp.inf)
        p = jax.nn.softmax(logits, axis=-1)
        outs.append(jnp.einsum('bhqk,bkhd->bqhd', p.astype(v.dtype), v[:, :qe]))
    return jnp.concatenate(outs, axis=1)


def causal_depthwise_conv(a, w, bias):
    c = a.shape[-1]
    out = lax.conv_general_dilated(
        a, w.astype(a.dtype)[:, None, :], window_strides=(1,),
        padding=[(CONV_WIDTH - 1, 0)], dimension_numbers=('NWC', 'WIO', 'NWC'),
        feature_group_count=c)
    return out + bias.astype(a.dtype)


def hybrid_layer(h, g_mix_pre, w_in, b_f, g_sb, g_fox, w_out, g_mix_post,
                 g_ffn_pre, w_up, conv_w, conv_b, w_down, g_ffn_post):
    b, l, _ = h.shape
    u = rms_norm(h, g_mix_pre)
    proj = u @ w_in
    splits = [W_SB, 2 * W_SB, 3 * W_SB, 3 * W_SB + W_FOX, 3 * W_SB + 2 * W_FOX, 3 * W_SB + 3 * W_FOX]
    q_sb, k_sb, v_sb, q_fx, k_fx, v_fx, f_logit = jnp.split(proj, splits, axis=-1)
    heads_sb = lambda t: t.reshape(b, l, N_HEADS_SB, HEAD_DIM)
    heads_fx = lambda t: t.reshape(b, l, N_HEADS_FOX, HEAD_DIM)
    o_sb = stick_breaking_attention(heads_sb(q_sb), heads_sb(k_sb), heads_sb(v_sb))
    log_f = jax.nn.log_sigmoid((f_logit + b_f).astype(jnp.float32))
    o_fx = forgetting_attention(heads_fx(q_fx), heads_fx(k_fx), heads_fx(v_fx), log_f)
    o_sb = rms_norm(o_sb, g_sb).reshape(b, l, W_SB)
    o_fx = rms_norm(o_fx, g_fox).reshape(b, l, W_FOX)
    mix = jnp.concatenate([o_sb, o_fx], axis=-1) @ w_out
    h = h + rms_norm(mix, g_mix_post)
    u = rms_norm(h, g_ffn_pre)
    a = causal_depthwise_conv(u @ w_up, conv_w, conv_b)
    gate, up = jnp.split(a, [D_FF], axis=-1)
    ff = (jax.nn.silu(gate) * up) @ w_down
    return h + rms_norm(ff, g_ffn_post)


def _fwd_setup_inputs(seed: int = 0) -> dict:
    key = jax.random.key(seed)
    ks = jax.random.split(key, 16)
    f32 = jnp.float32
    nrm = lambda k, shape, s: jax.random.normal(k, shape, f32) * s
    gain = lambda k, shape: 1.0 + 0.02 * jax.random.normal(k, shape, f32)
    return {
        "x": nrm(ks[0], (BATCH, SEQ, D_MODEL), 1.0),
        "meta": nrm(ks[1], (N_META, D_MODEL), 1.0),
        "g_mix_pre": gain(ks[2], (DEPTH, D_MODEL)),
        "w_in": nrm(ks[3], (DEPTH, D_MODEL, N_IN), D_MODEL ** -0.5),
        "b_f": 3.0 + 0.5 * jax.random.normal(ks[4], (DEPTH, N_HEADS_FOX), f32),
        "g_sb": gain(ks[5], (DEPTH, N_HEADS_SB, HEAD_DIM)),
        "g_fox": gain(ks[6], (DEPTH, N_HEADS_FOX, HEAD_DIM)),
        "w_out": nrm(ks[7], (DEPTH, W_MIX, D_MODEL), W_MIX ** -0.5),
        "g_mix_post": gain(ks[8], (DEPTH, D_MODEL)),
        "g_ffn_pre": gain(ks[9], (DEPTH, D_MODEL)),
        "w_up": nrm(ks[10], (DEPTH, D_MODEL, 2 * D_FF), D_MODEL ** -0.5),
        "conv_w": nrm(ks[11], (DEPTH, CONV_WIDTH, 2 * D_FF), CONV_WIDTH ** -0.5),
        "conv_b": nrm(ks[12], (DEPTH, 2 * D_FF), 0.01),
        "w_down": nrm(ks[13], (DEPTH, D_FF, D_MODEL), D_FF ** -0.5),
        "g_ffn_post": gain(ks[14], (DEPTH, D_MODEL)),
    }


def _fwd_reference(x, meta, g_mix_pre, w_in, b_f, g_sb, g_fox, w_out, g_mix_post,
              g_ffn_pre, w_up, conv_w, conv_b, w_down, g_ffn_post):
    b = x.shape[0]
    meta_b = jnp.broadcast_to(meta[None].astype(x.dtype), (b, N_META, D_MODEL))
    h = jnp.concatenate([meta_b, x], axis=1)
    for i in range(DEPTH):
        h = hybrid_layer(h, g_mix_pre[i], w_in[i], b_f[i], g_sb[i], g_fox[i], w_out[i],
                         g_mix_post[i], g_ffn_pre[i], w_up[i], conv_w[i], conv_b[i],
                         w_down[i], g_ffn_post[i])
    return h[:, N_META:]


import jax as _jax
import jax.numpy as _jnp

TWIN_FORMAT = 'train_step'
FWD_PARAMS = ['x', 'meta', 'g_mix_pre', 'w_in', 'b_f', 'g_sb', 'g_fox', 'w_out', 'g_mix_post', 'g_ffn_pre', 'w_up', 'conv_w', 'conv_b', 'w_down', 'g_ffn_post']
TWIN_WEIGHTS = ['meta', 'g_mix_pre', 'w_in', 'b_f', 'g_sb', 'g_fox', 'w_out', 'g_mix_post', 'g_ffn_pre', 'w_up', 'conv_w', 'conv_b', 'w_down', 'g_ffn_post']
TWIN_DIFF_INPUT = 'x'
TWIN_INPUTS = ['x', 'meta', 'g_mix_pre', 'w_in', 'b_f', 'g_sb', 'g_fox', 'w_out', 'g_mix_post', 'g_ffn_pre', 'w_up', 'conv_w', 'conv_b', 'w_down', 'g_ffn_post', 'loss_target', 'm_meta', 'm_g_mix_pre', 'm_w_in', 'm_b_f', 'm_g_sb', 'm_g_fox', 'm_w_out', 'm_g_mix_post', 'm_g_ffn_pre', 'm_w_up', 'm_conv_w', 'm_conv_b', 'm_w_down', 'm_g_ffn_post', 'v_meta', 'v_g_mix_pre', 'v_w_in', 'v_b_f', 'v_g_sb', 'v_g_fox', 'v_w_out', 'v_g_mix_post', 'v_g_ffn_pre', 'v_w_up', 'v_conv_w', 'v_conv_b', 'v_w_down', 'v_g_ffn_post']
TWIN_OUTPUTS = ['loss', 'grad_x', 'grad_meta', 'grad_g_mix_pre', 'grad_w_in', 'grad_b_f', 'grad_g_sb', 'grad_g_fox', 'grad_w_out', 'grad_g_mix_post', 'grad_g_ffn_pre', 'grad_w_up', 'grad_conv_w', 'grad_conv_b', 'grad_w_down', 'grad_g_ffn_post', 'delta_meta', 'delta_g_mix_pre', 'delta_w_in', 'delta_b_f', 'delta_g_sb', 'delta_g_fox', 'delta_w_out', 'delta_g_mix_post', 'delta_g_ffn_pre', 'delta_w_up', 'delta_conv_w', 'delta_conv_b', 'delta_w_down', 'delta_g_ffn_post', 'new_m_meta', 'new_m_g_mix_pre', 'new_m_w_in', 'new_m_b_f', 'new_m_g_sb', 'new_m_g_fox', 'new_m_w_out', 'new_m_g_mix_post', 'new_m_g_ffn_pre', 'new_m_w_up', 'new_m_conv_w', 'new_m_conv_b', 'new_m_w_down', 'new_m_g_ffn_post', 'new_v_meta', 'new_v_g_mix_pre', 'new_v_w_in', 'new_v_b_f', 'new_v_g_sb', 'new_v_g_fox', 'new_v_w_out', 'new_v_g_mix_post', 'new_v_g_ffn_pre', 'new_v_w_up', 'new_v_conv_w', 'new_v_conv_b', 'new_v_w_down', 'new_v_g_ffn_post']
TWIN_LEAF_KINDS = {'loss': 'loss', 'grad_x': 'grad_x', 'grad_meta': 'grad_w', 'grad_g_mix_pre': 'grad_w', 'grad_w_in': 'grad_w', 'grad_b_f': 'grad_w', 'grad_g_sb': 'grad_w', 'grad_g_fox': 'grad_w', 'grad_w_out': 'grad_w', 'grad_g_mix_post': 'grad_w', 'grad_g_ffn_pre': 'grad_w', 'grad_w_up': 'grad_w', 'grad_conv_w': 'grad_w', 'grad_conv_b': 'grad_w', 'grad_w_down': 'grad_w', 'grad_g_ffn_post': 'grad_w', 'delta_meta': 'delta_w', 'delta_g_mix_pre': 'delta_w', 'delta_w_in': 'delta_w', 'delta_b_f': 'delta_w', 'delta_g_sb': 'delta_w', 'delta_g_fox': 'delta_w', 'delta_w_out': 'delta_w', 'delta_g_mix_post': 'delta_w', 'delta_g_ffn_pre': 'delta_w', 'delta_w_up': 'delta_w', 'delta_conv_w': 'delta_w', 'delta_conv_b': 'delta_w', 'delta_w_down': 'delta_w', 'delta_g_ffn_post': 'delta_w', 'new_m_meta': 'new_m', 'new_m_g_mix_pre': 'new_m', 'new_m_w_in': 'new_m', 'new_m_b_f': 'new_m', 'new_m_g_sb': 'new_m', 'new_m_g_fox': 'new_m', 'new_m_w_out': 'new_m', 'new_m_g_mix_post': 'new_m', 'new_m_g_ffn_pre': 'new_m', 'new_m_w_up': 'new_m', 'new_m_conv_w': 'new_m', 'new_m_conv_b': 'new_m', 'new_m_w_down': 'new_m', 'new_m_g_ffn_post': 'new_m', 'new_v_meta': 'new_v', 'new_v_g_mix_pre': 'new_v', 'new_v_w_in': 'new_v', 'new_v_b_f': 'new_v', 'new_v_g_sb': 'new_v', 'new_v_g_fox': 'new_v', 'new_v_w_out': 'new_v', 'new_v_g_mix_post': 'new_v', 'new_v_g_ffn_pre': 'new_v', 'new_v_w_up': 'new_v', 'new_v_conv_w': 'new_v', 'new_v_conv_b': 'new_v', 'new_v_w_down': 'new_v', 'new_v_g_ffn_post': 'new_v'}


def _forward(args):
    return _fwd_reference(*[args[k] for k in FWD_PARAMS])


def _output_shape():
    def fwd():
        inp = _fwd_setup_inputs(0)
        return _fwd_reference(*[inp[k] for k in FWD_PARAMS])
    out = _jax.eval_shape(fwd)
    return out.shape, out.dtype

N_MICROBATCH = 1
ADAM_LR = 0.001
ADAM_B1 = 0.9
ADAM_B2 = 0.999
ADAM_EPS = 1e-08
ADAM_WD = 0.01
ADAM_STEP = 10
PER_EXAMPLE_BATCH_AXIS = {'x': 0, 'loss_target': 0}
SHARED_INPUTS = []
_WEIGHT_DTYPES = {'meta': _jnp.float32, 'g_mix_pre': _jnp.float32, 'w_in': _jnp.float32, 'b_f': _jnp.float32, 'g_sb': _jnp.float32, 'g_fox': _jnp.float32, 'w_out': _jnp.float32, 'g_mix_post': _jnp.float32, 'g_ffn_pre': _jnp.float32, 'w_up': _jnp.float32, 'conv_w': _jnp.float32, 'conv_b': _jnp.float32, 'w_down': _jnp.float32, 'g_ffn_post': _jnp.float32}
MOMENT_SCALE = {'meta': 3.825084e-02, 'g_mix_pre': 5.651231e-01, 'w_in': 3.192816e-01, 'b_f': 2.548508e+00, 'g_sb': 4.341885e-01, 'g_fox': 5.209180e-01, 'w_out': 4.664568e-01, 'g_mix_post': 1.597657e+01, 'g_ffn_pre': 4.079744e-01, 'w_up': 1.714331e-01, 'conv_w': 1.804739e-01, 'conv_b': 2.820038e-01, 'w_down': 3.051295e-01, 'g_ffn_post': 1.598690e+01}


def _to_microbatches(a, axis):
    t = _jnp.moveaxis(a, axis, 0)
    t = t.reshape((N_MICROBATCH, t.shape[0] // N_MICROBATCH) + t.shape[1:])
    return _jnp.moveaxis(t, 1, axis + 1)


def setup_inputs(seed: int = 0) -> dict:
    inp = _fwd_setup_inputs(seed)
    key = _jax.random.fold_in(_jax.random.key(seed), 7919)
    shape, _ = _output_shape()
    out = dict(inp)
    out["loss_target"] = _jax.random.normal(_jax.random.fold_in(key, 0), shape, _jnp.float32)
    for i, name in enumerate(TWIN_WEIGHTS):
        w = inp[name].astype(_jnp.float32)
        if MOMENT_SCALE is None:
            s = _jnp.sqrt(_jnp.mean(_jnp.square(w)) + 1e-30)
        else:
            s = MOMENT_SCALE[name]
        km, kv = _jax.random.split(_jax.random.fold_in(key, i + 1))
        out[name] = w
        out["m_" + name] = s * _jax.random.normal(km, w.shape, _jnp.float32)
        out["v_" + name] = (s * s) * _jax.random.uniform(kv, w.shape, _jnp.float32, 0.5, 1.5)
    if N_MICROBATCH > 1:
        for name, axis in PER_EXAMPLE_BATCH_AXIS.items():
            out[name] = _to_microbatches(out[name], axis)
    return {'x': out['x'], 'meta': out['meta'], 'g_mix_pre': out['g_mix_pre'], 'w_in': out['w_in'], 'b_f': out['b_f'], 'g_sb': out['g_sb'], 'g_fox': out['g_fox'], 'w_out': out['w_out'], 'g_mix_post': out['g_mix_post'], 'g_ffn_pre': out['g_ffn_pre'], 'w_up': out['w_up'], 'conv_w': out['conv_w'], 'conv_b': out['conv_b'], 'w_down': out['w_down'], 'g_ffn_post': out['g_ffn_post'], 'loss_target': out['loss_target'], 'm_meta': out['m_meta'], 'm_g_mix_pre': out['m_g_mix_pre'], 'm_w_in': out['m_w_in'], 'm_b_f': out['m_b_f'], 'm_g_sb': out['m_g_sb'], 'm_g_fox': out['m_g_fox'], 'm_w_out': out['m_w_out'], 'm_g_mix_post': out['m_g_mix_post'], 'm_g_ffn_pre': out['m_g_ffn_pre'], 'm_w_up': out['m_w_up'], 'm_conv_w': out['m_conv_w'], 'm_conv_b': out['m_conv_b'], 'm_w_down': out['m_w_down'], 'm_g_ffn_post': out['m_g_ffn_post'], 'v_meta': out['v_meta'], 'v_g_mix_pre': out['v_g_mix_pre'], 'v_w_in': out['v_w_in'], 'v_b_f': out['v_b_f'], 'v_g_sb': out['v_g_sb'], 'v_g_fox': out['v_g_fox'], 'v_w_out': out['v_w_out'], 'v_g_mix_post': out['v_g_mix_post'], 'v_g_ffn_pre': out['v_g_ffn_pre'], 'v_w_up': out['v_w_up'], 'v_conv_w': out['v_conv_w'], 'v_conv_b': out['v_conv_b'], 'v_w_down': out['v_w_down'], 'v_g_ffn_post': out['v_g_ffn_post']}


def _loss(weights, diff, rest, loss_target):
    with _jax.named_scope("forward"):
        args = {**rest, TWIN_DIFF_INPUT: diff, **{k: w.astype(_WEIGHT_DTYPES[k]) for k, w in weights.items()}}
        y = _forward(args)
    with _jax.named_scope("loss_head"):
        err = _jnp.square(y.astype(_jnp.float32) - loss_target)
        return 0.5 * _jnp.sum(_jnp.mean(err, axis=-1)) if err.ndim else 0.5 * err


def _adamw(w, g, m, v):
    m = ADAM_B1 * m + (1.0 - ADAM_B1) * g
    v = ADAM_B2 * v + (1.0 - ADAM_B2) * _jnp.square(g)
    m_hat = m / (1.0 - ADAM_B1 ** ADAM_STEP)
    v_hat = v / (1.0 - ADAM_B2 ** ADAM_STEP)
    delta = -ADAM_LR * (m_hat / (_jnp.sqrt(v_hat) + ADAM_EPS) + ADAM_WD * w)
    return delta, m, v


def reference(x, meta, g_mix_pre, w_in, b_f, g_sb, g_fox, w_out, g_mix_post, g_ffn_pre, w_up, conv_w, conv_b, w_down, g_ffn_post, loss_target, m_meta, m_g_mix_pre, m_w_in, m_b_f, m_g_sb, m_g_fox, m_w_out, m_g_mix_post, m_g_ffn_pre, m_w_up, m_conv_w, m_conv_b, m_w_down, m_g_ffn_post, v_meta, v_g_mix_pre, v_w_in, v_b_f, v_g_sb, v_g_fox, v_w_out, v_g_mix_post, v_g_ffn_pre, v_w_up, v_conv_w, v_conv_b, v_w_down, v_g_ffn_post):
    given = dict(x=x, meta=meta, g_mix_pre=g_mix_pre, w_in=w_in, b_f=b_f, g_sb=g_sb, g_fox=g_fox, w_out=w_out, g_mix_post=g_mix_post, g_ffn_pre=g_ffn_pre, w_up=w_up, conv_w=conv_w, conv_b=conv_b, w_down=w_down, g_ffn_post=g_ffn_post, loss_target=loss_target, m_meta=m_meta, m_g_mix_pre=m_g_mix_pre, m_w_in=m_w_in, m_b_f=m_b_f, m_g_sb=m_g_sb, m_g_fox=m_g_fox, m_w_out=m_w_out, m_g_mix_post=m_g_mix_post, m_g_ffn_pre=m_g_ffn_pre, m_w_up=m_w_up, m_conv_w=m_conv_w, m_conv_b=m_conv_b, m_w_down=m_w_down, m_g_ffn_post=m_g_ffn_post, v_meta=v_meta, v_g_mix_pre=v_g_mix_pre, v_w_in=v_w_in, v_b_f=v_b_f, v_g_sb=v_g_sb, v_g_fox=v_g_fox, v_w_out=v_w_out, v_g_mix_post=v_g_mix_post, v_g_ffn_pre=v_g_ffn_pre, v_w_up=v_w_up, v_conv_w=v_conv_w, v_conv_b=v_conv_b, v_w_down=v_w_down, v_g_ffn_post=v_g_ffn_post)
    weights = {n: given[n] for n in TWIN_WEIGHTS}
    shared = {n: given[n] for n in SHARED_INPUTS}
    per_example = {n: given[n] for n in ['x']}
    grad_fn = _jax.value_and_grad(_loss, argnums=(0, 1))

    def one_microbatch(ex, loss_target):
        ex = dict(ex)
        diff = ex.pop(TWIN_DIFF_INPUT)
        return grad_fn(weights, diff, {**shared, **ex}, loss_target)

    if N_MICROBATCH == 1:
        loss, (grad_w, grad_x) = one_microbatch(per_example, given["loss_target"])
    else:
        def body(carry, xs):
            loss_sum, grad_sum = carry
            l_k, (gw_k, gx_k) = one_microbatch(xs[0], xs[1])
            with _jax.named_scope("update"):
                return (loss_sum + l_k, _jax.tree.map(_jnp.add, grad_sum, gw_k)), gx_k

        init = (_jnp.zeros((), _jnp.float32), _jax.tree.map(_jnp.zeros_like, weights))
        (loss, grad_w), grad_x = _jax.lax.scan(body, init, (per_example, given["loss_target"]))
    with _jax.named_scope("update"):
        delta_w, new_m, new_v = {}, {}, {}
        for n in TWIN_WEIGHTS:
            delta_w[n], new_m[n], new_v[n] = _adamw(weights[n], grad_w[n], given["m_" + n], given["v_" + n])
    return (loss, grad_x, *[grad_w[n] for n in TWIN_WEIGHTS], *[delta_w[n] for n in TWIN_WEIGHTS],
            *[new_m[n] for n in TWIN_WEIGHTS], *[new_v[n] for n in TWIN_WEIGHTS])
```

```python
import functools
import math

import jax
import jax.numpy as jnp
from jax import lax
from jax.experimental import pallas as pl
from jax.experimental.pallas import tpu as pltpu

F32, BF16 = jnp.float32, jnp.bfloat16
HEAD_DIM = 128
LANES = 128
EPS = 1e-6
NEG_INF = -1e30
ATT_BLOCK = 256
N_DEV = 8
V7X_VMEM_BUDGET = 56 * 1024 * 1024

ADAM_LR, ADAM_B1, ADAM_B2, ADAM_EPS, ADAM_WD, ADAM_STEP = 0.001, 0.9, 0.999, 1e-08, 0.01, 10


def _divisor(n, cands):
    for c in cands:
        if c <= n and n % c == 0:
            return c
    raise ValueError(f"no tile for {n} among {cands}")


def _params(sem, est_bytes):
    limit = int(min(V7X_VMEM_BUDGET, max(16 * 1024 * 1024, est_bytes * 5 // 4 + (2 << 20))))
    return pltpu.CompilerParams(dimension_semantics=sem, vmem_limit_bytes=limit)


def _nbytes(shape, dtype):
    return math.prod(shape) * jnp.dtype(dtype).itemsize


_DN = {"nn": (((1,), (0,)), ((), ())), "nt": (((1,), (1,)), ((), ())), "tn": (((0,), (0,)), ((), ()))}
_ROW_TILES = (1088, 544, 272, 512, 256, 128, 64, 32, 16, 8)
_COL_TILES = (512, 256, 128)


def _ktile(k, cap):
    if k <= cap:
        return k
    for t in range(cap - cap % LANES, 0, -LANES):
        if k % t == 0:
            return t
    raise ValueError(k)


def _mm(a, b, mode, out_dtype, name, a2=None, b2=None):
    if mode == "nn":
        (M, K), (_, N) = a.shape, b.shape
    elif mode == "nt":
        (M, K), (N, _) = a.shape, b.shape
    else:
        (K, M), (_, N) = a.shape, b.shape
    if mode == "tn":
        tm, tn, tk = _divisor(M, _COL_TILES), _divisor(N, _COL_TILES), _ktile(K, 4608)
    else:
        tm, tn, tk = _divisor(M, _ROW_TILES), _divisor(N, _COL_TILES), _ktile(K, 3072)
    nk = K // tk
    a_spec = {"nn": pl.BlockSpec((tm, tk), lambda i, j, k: (i, k)),
              "nt": pl.BlockSpec((tm, tk), lambda i, j, k: (i, k)),
              "tn": pl.BlockSpec((tk, tm), lambda i, j, k: (k, i))}[mode]
    b_spec = {"nn": pl.BlockSpec((tk, tn), lambda i, j, k: (k, j)),
              "nt": pl.BlockSpec((tn, tk), lambda i, j, k: (j, k)),
              "tn": pl.BlockSpec((tk, tn), lambda i, j, k: (k, j))}[mode]
    dn = _DN[mode]
    extra = a2 is not None
    in_specs, args = [a_spec, b_spec], [a, b]
    if extra:
        k2 = a2.shape[1]
        in_specs += [pl.BlockSpec((tm, k2), lambda i, j, k: (i, 0)), pl.BlockSpec((tn, k2), lambda i, j, k: (j, 0))]
        args += [a2, b2]

    def body(*refs):
        if extra:
            a_ref, b_ref, a2_ref, b2_ref, o_ref, acc = refs
        else:
            a_ref, b_ref, o_ref, acc = refs
        part = lax.dot_general(a_ref[...], b_ref[...], dn, preferred_element_type=F32)
        if nk == 1:
            if extra:
                part = part + lax.dot_general(a2_ref[...], b2_ref[...], _DN["nt"], preferred_element_type=F32)
            o_ref[...] = part.astype(o_ref.dtype)
            return
        kk = pl.program_id(2)

        @pl.when(kk == 0)
        def _():
            if extra:
                acc[...] = part + lax.dot_general(a2_ref[...], b2_ref[...], _DN["nt"], preferred_element_type=F32)
            else:
                acc[...] = part

        @pl.when(kk > 0)
        def _():
            acc[...] += part

        @pl.when(kk == nk - 1)
        def _():
            o_ref[...] = acc[...].astype(o_ref.dtype)

    est = 2 * (tm * tk + tk * tn) * 2 + 2 * _nbytes((tm, tn), out_dtype) + tm * tn * 4
    return pl.pallas_call(
        body, name=name, grid=(M // tm, N // tn, nk), in_specs=in_specs,
        out_specs=pl.BlockSpec((tm, tn), lambda i, j, k: (i, j)),
        out_shape=jax.ShapeDtypeStruct((M, N), out_dtype),
        scratch_shapes=[pltpu.VMEM((tm, tn), F32)],
        compiler_params=_params(("parallel", "parallel", "arbitrary"), est),
    )(*args)


def _rms(x, g):
    r = lax.rsqrt(jnp.mean(x * x, axis=-1, keepdims=True) + EPS)
    return x * r * g


def _rms_bwd(x, g, dout):
    r = lax.rsqrt(jnp.mean(x * x, axis=-1, keepdims=True) + EPS)
    xhat = x * r
    dxh = dout * g
    dx = r * (dxh - xhat * jnp.mean(dxh * xhat, axis=-1, keepdims=True))
    return dx, dout * xhat


def _row_tile(L):
    return _divisor(L, (272, 256, 128, 64, 32, 16))


def _resnorm_fwd(h, y, g_post, g_pre, name):
    L, D = h.shape
    bt = _row_tile(L)
    has_y, has_pre = y is not None, g_pre is not None
    row = pl.BlockSpec((bt, D), lambda i: (i, 0))
    vec = pl.BlockSpec((1, D), lambda i: (0, 0))
    args, in_specs = [h], [row]
    if has_y:
        args += [y, g_post.reshape(1, D)]
        in_specs += [row, vec]
    if has_pre:
        args += [g_pre.reshape(1, D)]
        in_specs += [vec]
    out_shape, out_specs = [], []
    if has_y:
        out_shape.append(jax.ShapeDtypeStruct((L, D), F32))
        out_specs.append(row)
    if has_pre:
        out_shape.append(jax.ShapeDtypeStruct((L, D), BF16))
        out_specs.append(row)

    def body(*refs):
        refs = list(refs)
        h_ref = refs.pop(0)
        hn = h_ref[...]
        if has_y:
            y_ref, gp_ref = refs.pop(0), refs.pop(0)
            hn = hn + _rms(y_ref[...], gp_ref[...])
        if has_pre:
            g_ref = refs.pop(0)
        if has_y:
            refs.pop(0)[...] = hn
        if has_pre:
            refs.pop(0)[...] = _rms(hn, g_ref[...]).astype(BF16)

    outs = pl.pallas_call(
        body, name=name, grid=(L // bt,), in_specs=in_specs, out_specs=out_specs, out_shape=out_shape,
        compiler_params=_params(("parallel",), 10 * bt * D * 4),
    )(*args)
    outs = list(outs)
    h_new = outs.pop(0) if has_y else h
    u = outs.pop(0) if has_pre else None
    return h_new, u


def _resnorm_bwd(dh_direct, du, h_new, y, g_post, g_pre, name):
    L, D = dh_direct.shape
    bt = _row_tile(L)
    has_y, has_pre = y is not None, du is not None
    row = pl.BlockSpec((bt, D), lambda i: (i, 0))
    vec = pl.BlockSpec((1, D), lambda i: (0, 0))
    acc = pl.BlockSpec((8, D), lambda i: (0, 0))
    args, in_specs = [dh_direct], [row]
    if has_pre:
        args += [du, h_new, g_pre.reshape(1, D)]
        in_specs += [row, row, vec]
    if has_y:
        args += [y, g_post.reshape(1, D)]
        in_specs += [row, vec]
    out_shape, out_specs = [], []
    if has_pre:
        out_shape += [jax.ShapeDtypeStruct((L, D), F32), jax.ShapeDtypeStruct((8, D), F32)]
        out_specs += [row, acc]
    if has_y:
        out_shape += [jax.ShapeDtypeStruct((L, D), BF16), jax.ShapeDtypeStruct((8, D), F32)]
        out_specs += [row, acc]

    def colsum8(v):
        return jnp.sum(v.reshape(bt // 8, 8, D), axis=0)

    def body(*refs):
        refs = list(refs)
        first = pl.program_id(0) == 0
        dh = refs.pop(0)[...]
        if has_pre:
            du_ref, hn_ref, g_ref = refs.pop(0), refs.pop(0), refs.pop(0)
        if has_y:
            y_ref, gp_ref = refs.pop(0), refs.pop(0)
        if has_pre:
            dh_ref, dgpre_ref = refs.pop(0), refs.pop(0)
            dx, dgp = _rms_bwd(hn_ref[...], g_ref[...], du_ref[...].astype(F32))
            dh = dh + dx
            dh_ref[...] = dh

            @pl.when(first)
            def _():
                dgpre_ref[...] = jnp.zeros_like(dgpre_ref)
            dgpre_ref[...] += colsum8(dgp)
        if has_y:
            dy_ref, dgpost_ref = refs.pop(0), refs.pop(0)
            dy, dgq = _rms_bwd(y_ref[...], gp_ref[...], dh)
            dy_ref[...] = dy.astype(BF16)

            @pl.when(first)
            def _():
                dgpost_ref[...] = jnp.zeros_like(dgpost_ref)
            dgpost_ref[...] += colsum8(dgq)

    outs = list(pl.pallas_call(
        body, name=name, grid=(L // bt,), in_specs=in_specs, out_specs=out_specs, out_shape=out_shape,
        compiler_params=_params(("arbitrary",), 14 * bt * D * 4),
    )(*args))
    dh, dg_pre, dy, dg_post = dh_direct, None, None, None
    if has_pre:
        dh, dg_pre = outs.pop(0), outs.pop(0).sum(0)
    if has_y:
        dy, dg_post = outs.pop(0), outs.pop(0).sum(0)
    return dh, dy, dg_post, dg_pre


def _loss(h, tgt, n_meta, seq, name):
    L, D = h.shape
    bt = _row_tile(L)
    row = pl.BlockSpec((bt, D), lambda i: (i, 0))

    def body(h_ref, t_ref, dy_ref, loss_ref):
        i = pl.program_id(0)
        r = i * bt + lax.broadcasted_iota(jnp.int32, (bt, 1), 0)
        valid = (r >= n_meta) & (r < n_meta + seq)
        e = jnp.where(valid, h_ref[...] - t_ref[...], 0.0)
        dy_ref[...] = e * (1.0 / D)

        @pl.when(i == 0)
        def _():
            loss_ref[...] = jnp.zeros_like(loss_ref)
        loss_ref[...] += 0.5 * jnp.sum(jnp.sum(e * e, axis=-1, keepdims=True) * (1.0 / D))

    dy, loss = pl.pallas_call(
        body, name=name, grid=(L // bt,), in_specs=[row, row],
        out_specs=[row, pl.BlockSpec((8, LANES), lambda i: (0, 0))],
        out_shape=[jax.ShapeDtypeStruct((L, D), F32), jax.ShapeDtypeStruct((8, LANES), F32)],
        compiler_params=_params(("arbitrary",), 8 * bt * D * 4),
    )(h, tgt)
    return loss[0, 0], dy


def _split3(x):
    x1 = x.astype(BF16)
    r1 = x - x1.astype(F32)
    x2 = r1.astype(BF16)
    x3 = (r1 - x2.astype(F32)).astype(BF16)
    return x1, x2, x3


def _tri_dot3(tri, x):
    x1, x2, x3 = _split3(x)
    d = functools.partial(jnp.dot, preferred_element_type=F32)
    return d(tri, x1) + d(tri, x2) + d(tri, x3)


def _gate_fwd(fl, b_pad, name):
    L = fl.shape[0]
    bt = 128
    blk = pl.BlockSpec((bt, LANES), lambda i: (i, 0))

    def body(fl_ref, b_ref, c_ref, carry):
        @pl.when(pl.program_id(0) == 0)
        def _():
            carry[...] = jnp.zeros_like(carry)
        x = fl_ref[...] + b_ref[...]
        lf = jnp.minimum(x, 0.0) - jnp.log(1.0 + jnp.exp(-jnp.abs(x)))
        r = lax.broadcasted_iota(jnp.int32, (bt, bt), 0)
        s = lax.broadcasted_iota(jnp.int32, (bt, bt), 1)
        tri = (s <= r).astype(BF16)
        c = _tri_dot3(tri, lf) + carry[...]
        c_ref[...] = c
        carry[...] = c[bt - 1:bt, :]

    return pl.pallas_call(
        body, name=name, grid=(L // bt,), in_specs=[blk, pl.BlockSpec((1, LANES), lambda i: (0, 0))],
        out_specs=blk, out_shape=jax.ShapeDtypeStruct((L, LANES), F32),
        scratch_shapes=[pltpu.VMEM((1, LANES), F32)],
        compiler_params=_params(("arbitrary",), 1 << 20),
    )(fl, b_pad)


def _gate_bwd(dc, fl, b_pad, name):
    L = fl.shape[0]
    bt = 128
    n = L // bt
    blk = pl.BlockSpec((bt, LANES), lambda i: (n - 1 - i, 0))

    def body(dc_ref, fl_ref, b_ref, dfl_ref, db_ref, carry):
        @pl.when(pl.program_id(0) == 0)
        def _():
            carry[...] = jnp.zeros_like(carry)
            db_ref[...] = jnp.zeros_like(db_ref)
        r = lax.broadcasted_iota(jnp.int32, (bt, bt), 0)
        s = lax.broadcasted_iota(jnp.int32, (bt, bt), 1)
        tri = (s >= r).astype(BF16)
        dlf = _tri_dot3(tri, dc_ref[...]) + carry[...]
        carry[...] = dlf[0:1, :]
        x = fl_ref[...] + b_ref[...]
        dfl = dlf / (1.0 + jnp.exp(x))
        dfl_ref[...] = dfl.astype(BF16)
        db_ref[...] += jnp.sum(dfl.reshape(bt // 8, 8, LANES), axis=0)

    dfl, db = pl.pallas_call(
        body, name=name, grid=(n,), in_specs=[blk, blk, pl.BlockSpec((1, LANES), lambda i: (0, 0))],
        out_specs=[blk, pl.BlockSpec((8, LANES), lambda i: (0, 0))],
        out_shape=[jax.ShapeDtypeStruct((L, LANES), BF16), jax.ShapeDtypeStruct((8, LANES), F32)],
        scratch_shapes=[pltpu.VMEM((1, LANES), F32)],
        compiler_params=_params(("arbitrary",), 1 << 20),
    )(dc, fl, b_pad)
    return dfl, db.sum(0)


def _split2(x):
    hi = x.astype(BF16)
    return hi, (x - hi.astype(F32)).astype(BF16)


def _suffix_sums(x, tri):
    hi, lo = _split2(x)
    return jnp.dot(hi, tri, preferred_element_type=F32) + jnp.dot(lo, tri, preferred_element_type=F32)


def _dot_nt(a, b):
    return lax.dot_general(a, b, _DN["nt"], preferred_element_type=F32)


def _dot_tn(a, b):
    return lax.dot_general(a, b, _DN["tn"], preferred_element_type=F32)


def _suffix_matrix(bk):
    j = lax.broadcasted_iota(jnp.int32, (bk, bk), 0)
    s = lax.broadcasted_iota(jnp.int32, (bk, bk), 1)
    return (j >= s).astype(BF16)


def _sb_tile(q, k, t_pos, ks, bk, scale, carry_c, tri):
    z = _dot_nt(q, k) * scale
    s_pos = ks + lax.broadcasted_iota(jnp.int32, (1, bk), 1)
    mask = s_pos < t_pos
    sp = jnp.maximum(z, 0.0) + jnp.log(1.0 + jnp.exp(-jnp.abs(z)))
    lkm = jnp.where(mask, -sp, 0.0)
    cs = _suffix_sums(lkm, tri)
    a = jnp.where(mask, jnp.exp(z + carry_c + cs), 0.0)
    return z, sp, mask, a, cs


def _attn_specs(H, L, bq):
    W3 = 3 * H

    def col(role):
        return lambda h, i: (h // H) * W3 + role * H + h % H

    q_spec = pl.BlockSpec((bq, HEAD_DIM), lambda h, i: (i, col(0)(h, i)))
    k_spec = pl.BlockSpec((L, HEAD_DIM), lambda h, i: (0, col(1)(h, i)))
    v_spec = pl.BlockSpec((L, HEAD_DIM), lambda h, i: (0, col(2)(h, i)))
    crow_spec = pl.BlockSpec((None, 1, L), lambda h, i: (jnp.maximum(h - H, 0), 0, 0))
    ccol_spec = pl.BlockSpec((None, bq, 1), lambda h, i: (jnp.maximum(h - H, 0), i, 0))
    g_spec = pl.BlockSpec((None, 1, HEAD_DIM), lambda h, i: (h, 0, 0))
    tile = pl.BlockSpec((bq, HEAD_DIM), lambda h, i: (i, h))
    stat = pl.BlockSpec((None, bq, 1), lambda h, i: (h, i, 0))
    return q_spec, k_spec, v_spec, crow_spec, ccol_spec, g_spec, tile, stat


def _attn_fwd(qkv, crow, ccol, g_heads, H, name):
    L = qkv.shape[0]
    bq = bk = min(ATT_BLOCK, L)
    nq = L // bq
    scale = HEAD_DIM ** -0.5
    q_spec, k_spec, v_spec, crow_spec, ccol_spec, g_spec, tile, stat = _attn_specs(H, L, bq)

    def body(q_ref, k_ref, v_ref, crow_ref, ccol_ref, g_ref, o_ref, on_ref, lse_ref):
        h, i = pl.program_id(0), pl.program_id(1)
        q = q_ref[...]
        t_pos = i * bq + lax.broadcasted_iota(jnp.int32, (bq, 1), 0)

        def finish(o):
            o_ref[...] = o
            on_ref[...] = _rms(o, g_ref[...]).astype(BF16)

        @pl.when(h < H)
        def _stick_breaking():
            tri = _suffix_matrix(bk)

            def step(jj, carry):
                c, acc = carry
                ks = pl.multiple_of((i - jj) * bk, bk)
                k = k_ref[pl.ds(ks, bk), :]
                v = v_ref[pl.ds(ks, bk), :]
                _, _, _, a, cs = _sb_tile(q, k, t_pos, ks, bk, scale, c, tri)
                a_hi, a_lo = _split2(a)
                acc = acc + jnp.dot(a_hi, v, preferred_element_type=F32) + jnp.dot(a_lo, v, preferred_element_type=F32)
                return c + cs[:, 0:1], acc

            _, acc = lax.fori_loop(0, i + 1, step, (jnp.zeros((bq, 1), F32), jnp.zeros((bq, HEAD_DIM), F32)))
            finish(acc)
            lse_ref[...] = jnp.zeros_like(lse_ref)

        @pl.when(h >= H)
        def _forgetting():
            cq = ccol_ref[...]

            def step(j, carry):
                m, l, acc = carry
                ks = pl.multiple_of(j * bk, bk)
                k = k_ref[pl.ds(ks, bk), :]
                v = v_ref[pl.ds(ks, bk), :]
                s = _dot_nt(q, k) * scale + (cq - crow_ref[:, pl.ds(ks, bk)])
                s_pos = ks + lax.broadcasted_iota(jnp.int32, (1, bk), 1)
                s = jnp.where(s_pos <= t_pos, s, NEG_INF)
                m_new = jnp.maximum(m, jnp.max(s, axis=-1, keepdims=True))
                alpha = jnp.exp(m - m_new)
                p = jnp.exp(s - m_new)
                l = alpha * l + jnp.sum(p, axis=-1, keepdims=True)
                p_hi, p_lo = _split2(p)
                acc = alpha * acc + jnp.dot(p_hi, v, preferred_element_type=F32) + jnp.dot(p_lo, v, preferred_element_type=F32)
                return m_new, l, acc

            init = (jnp.full((bq, 1), NEG_INF, F32), jnp.zeros((bq, 1), F32), jnp.zeros((bq, HEAD_DIM), F32))
            m, l, acc = lax.fori_loop(0, i + 1, step, init)
            finish(acc / l)
            lse_ref[...] = m + jnp.log(l)

    W2 = 2 * H * HEAD_DIM
    est = 4 * L * HEAD_DIM * 2 + 16 * bq * bk * 4 + (4 << 20)
    return pl.pallas_call(
        body, name=name, grid=(2 * H, nq),
        in_specs=[q_spec, k_spec, v_spec, crow_spec, ccol_spec, g_spec],
        out_specs=[tile, tile, stat],
        out_shape=[jax.ShapeDtypeStruct((L, W2), F32), jax.ShapeDtypeStruct((L, W2), BF16),
                   jax.ShapeDtypeStruct((2 * H, L, 1), F32)],
        compiler_params=_params(("parallel", "arbitrary"), est),
    )(qkv, qkv, qkv, crow, ccol, g_heads)


def _attn_bwd(qkv, crow, ccol, g_heads, o, lse, d_on, H, name):
    L = qkv.shape[0]
    bq = bk = min(ATT_BLOCK, L)
    nq = L // bq
    scale = HEAD_DIM ** -0.5
    q_spec, k_spec, v_spec, crow_spec, ccol_spec, g_spec, tile, stat = _attn_specs(H, L, bq)
    full = pl.BlockSpec((L, HEAD_DIM), lambda h, i: (0, h))
    dg_spec = pl.BlockSpec((None, 1, HEAD_DIM), lambda h, i: (h, 0, 0))
    dc_spec = pl.BlockSpec((None, 1, L), lambda h, i: (h, 0, 0))

    def body(q_ref, k_ref, v_ref, crow_ref, ccol_ref, g_ref, o_ref, lse_ref, don_ref,
             dq_ref, dk_ref, dv_ref, dg_ref, dc_ref, dk_acc, dv_acc):
        h, i = pl.program_id(0), pl.program_id(1)

        @pl.when(i == 0)
        def _():
            dk_acc[...] = jnp.zeros_like(dk_acc)
            dv_acc[...] = jnp.zeros_like(dv_acc)
            dg_ref[...] = jnp.zeros_like(dg_ref)
            dc_ref[...] = jnp.zeros_like(dc_ref)

        q = q_ref[...]
        t_pos = i * bq + lax.broadcasted_iota(jnp.int32, (bq, 1), 0)
        o_t = o_ref[...]
        d_o, dg = _rms_bwd(o_t, g_ref[...], don_ref[...])
        dg_ref[...] += jnp.sum(dg, axis=0, keepdims=True)
        d_ob = d_o.astype(BF16)
        dsum = jnp.sum(d_ob.astype(F32) * o_t, axis=-1, keepdims=True)

        @pl.when(h < H)
        def _stick_breaking():
            tri = _suffix_matrix(bk)

            def step(jj, carry):
                c, gs, dq = carry
                ks = pl.multiple_of((i - jj) * bk, bk)
                k = k_ref[pl.ds(ks, bk), :]
                v = v_ref[pl.ds(ks, bk), :]
                z, sp, mask, a, cs = _sb_tile(q, k, t_pos, ks, bk, scale, c, tri)
                g_w = a * _dot_nt(d_ob, v)
                incl = _suffix_sums(g_w, tri)
                prefix = dsum - gs - (incl - g_w)
                dz = jnp.where(mask, g_w - jnp.exp(z - sp) * prefix, 0.0) * scale
                dzb = dz.astype(BF16)
                dq = dq + jnp.dot(dzb, k, preferred_element_type=F32)
                dk_acc[pl.ds(ks, bk), :] += _dot_tn(dzb, q)
                dv_acc[pl.ds(ks, bk), :] += _dot_tn(a.astype(BF16), d_ob)
                return c + cs[:, 0:1], gs + incl[:, 0:1], dq

            z1 = jnp.zeros((bq, 1), F32)
            _, _, dq = lax.fori_loop(0, i + 1, step, (z1, z1, jnp.zeros((bq, HEAD_DIM), F32)))
            dq_ref[...] = dq.astype(BF16)

        @pl.when(h >= H)
        def _forgetting():
            cq = ccol_ref[...]
            lse_t = lse_ref[...]

            def step(j, dq):
                ks = pl.multiple_of(j * bk, bk)
                k = k_ref[pl.ds(ks, bk), :]
                v = v_ref[pl.ds(ks, bk), :]
                s = _dot_nt(q, k) * scale + (cq - crow_ref[:, pl.ds(ks, bk)])
                s_pos = ks + lax.broadcasted_iota(jnp.int32, (1, bk), 1)
                p = jnp.where(s_pos <= t_pos, jnp.exp(s - lse_t), 0.0)
                ds = p * (_dot_nt(d_ob, v) - dsum)
                dsb = (ds * scale).astype(BF16)
                dq = dq + jnp.dot(dsb, k, preferred_element_type=F32)
                dk_acc[pl.ds(ks, bk), :] += _dot_tn(dsb, q)
                dv_acc[pl.ds(ks, bk), :] += _dot_tn(p.astype(BF16), d_ob)
                dc_ref[:, pl.ds(ks, bk)] += -jnp.sum(ds, axis=0, keepdims=True)
                return dq

            dq = lax.fori_loop(0, i + 1, step, jnp.zeros((bq, HEAD_DIM), F32))
            dq_ref[...] = dq.astype(BF16)

        @pl.when(i == nq - 1)
        def _():
            dk_ref[...] = dk_acc[...].astype(BF16)
            dv_ref[...] = dv_acc[...].astype(BF16)

    W2 = 2 * H * HEAD_DIM
    est = 4 * L * HEAD_DIM * 2 + 4 * L * HEAD_DIM * 2 + 2 * L * HEAD_DIM * 4 + 24 * bq * bk * 4 + (4 << 20)
    dq, dk, dv, dg, dc = pl.pallas_call(
        body, name=name, grid=(2 * H, nq),
        in_specs=[q_spec, k_spec, v_spec, crow_spec, ccol_spec, g_spec, tile, stat, tile],
        out_specs=[tile, full, full, dg_spec, dc_spec],
        out_shape=[jax.ShapeDtypeStruct((L, W2), BF16)] * 3
        + [jax.ShapeDtypeStruct((2 * H, 1, HEAD_DIM), F32), jax.ShapeDtypeStruct((2 * H, 1, L), F32)],
        scratch_shapes=[pltpu.VMEM((L, HEAD_DIM), F32), pltpu.VMEM((L, HEAD_DIM), F32)],
        compiler_params=_params(("parallel", "arbitrary"), est),
    )(qkv, qkv, qkv, crow, ccol, g_heads, o, lse, d_on)
    return dq, dk, dv, dg[:, 0, :], dc


HALO = 16


def _conv_tiles(L, F):
    return _row_tile(L), _divisor(F, (512, 256, 128))


def _shift_down(x, halo_last2, first, bt):
    row = lax.broadcasted_iota(jnp.int32, (bt, 1), 0)
    h1 = jnp.where(first, 0.0, halo_last2[1:2, :])
    h2 = jnp.where(first, 0.0, halo_last2[0:1, :])
    x1 = jnp.where(row == 0, h1, pltpu.roll(x, 1, 0))
    x2 = jnp.where(row == 0, h2, jnp.where(row == 1, h1, pltpu.roll(x, 2, 0)))
    return x1, x2


def _conv_rows(p_ref, halo_ref, w_ref, b_ref, first, bt):
    p = p_ref[...].astype(F32)
    hl = halo_ref[...].astype(F32)
    p1, p2 = _shift_down(p, hl[HALO - 2:HALO, :], first, bt)
    w = w_ref[...]
    a = w[0:1, :] * p2 + w[1:2, :] * p1 + w[2:3, :] * p + b_ref[...]
    return a, p, p1, p2


def _sigmoid(x):
    return 1.0 / (1.0 + jnp.exp(-x))


def _conv_in_specs(L, F, bt, bc, order):
    nf = F // bc
    r = bt // HALO
    ix = (lambda a, b: (a, b)) if order == "ij" else (lambda a, b: (b, a))

    def mk(shape, fn):
        return pl.BlockSpec(shape, lambda a, b: fn(*ix(a, b)))

    return [
        mk((bt, bc), lambda i, j: (i, j)), mk((HALO, bc), lambda i, j: (jnp.maximum(i * r - 1, 0), j)),
        mk((bt, bc), lambda i, j: (i, nf + j)), mk((HALO, bc), lambda i, j: (jnp.maximum(i * r - 1, 0), nf + j)),
        mk((3, bc), lambda i, j: (0, j)), mk((3, bc), lambda i, j: (0, nf + j)),
        mk((1, bc), lambda i, j: (0, j)), mk((1, bc), lambda i, j: (0, nf + j)),
    ]


def _convgate_fwd(p, conv_w, conv_b, name):
    L, F2 = p.shape
    F = F2 // 2
    bt, bc = _conv_tiles(L, F)

    def body(pg, hg, pu, hu, wg, wu, bg, bu, act_ref):
        first = pl.program_id(0) == 0
        ag = _conv_rows(pg, hg, wg, bg, first, bt)[0]
        au = _conv_rows(pu, hu, wu, bu, first, bt)[0]
        act_ref[...] = (ag * _sigmoid(ag) * au).astype(BF16)

    return pl.pallas_call(
        body, name=name, grid=(L // bt, F // bc), in_specs=_conv_in_specs(L, F, bt, bc, "ij"),
        out_specs=pl.BlockSpec((bt, bc), lambda i, j: (i, j)), out_shape=jax.ShapeDtypeStruct((L, F), BF16),
        compiler_params=_params(("parallel", "parallel"), 24 * bt * bc * 4),
    )(p, p, p, p, conv_w, conv_w, conv_b.reshape(1, F2), conv_b.reshape(1, F2))


def _convgate_bwd(p, conv_w, conv_b, d_act, name):
    L, F2 = p.shape
    F = F2 // 2
    bt, bc = _conv_tiles(L, F)

    def body(pg, hg, pu, hu, wg, wu, bg, bu, dact_ref, da_ref, dwb_ref):
        first = pl.program_id(1) == 0
        ag, xg, xg1, xg2 = _conv_rows(pg, hg, wg, bg, first, bt)
        au, xu, xu1, xu2 = _conv_rows(pu, hu, wu, bu, first, bt)
        d_act = dact_ref[...].astype(F32)
        sg = _sigmoid(ag)
        dag = d_act * au * sg * (1.0 + ag * (1.0 - sg))
        dau = d_act * ag * sg
        da_ref[0] = dag.astype(BF16)
        da_ref[1] = dau.astype(BF16)

        @pl.when(first)
        def _():
            dwb_ref[...] = jnp.zeros_like(dwb_ref)
        cs = lambda v: jnp.sum(v, axis=0, keepdims=True)
        dwb_ref[0] += jnp.concatenate([cs(dag * xg2), cs(dag * xg1), cs(dag * xg), cs(dag)], axis=0)
        dwb_ref[1] += jnp.concatenate([cs(dau * xu2), cs(dau * xu1), cs(dau * xu), cs(dau)], axis=0)

    da, dwb = pl.pallas_call(
        body, name=name, grid=(F // bc, L // bt),
        in_specs=_conv_in_specs(L, F, bt, bc, "ji") + [pl.BlockSpec((bt, bc), lambda j, i: (i, j))],
        out_specs=[pl.BlockSpec((2, bt, bc), lambda j, i: (0, i, j)), pl.BlockSpec((2, 4, bc), lambda j, i: (0, 0, j))],
        out_shape=[jax.ShapeDtypeStruct((2, L, F), BF16), jax.ShapeDtypeStruct((2, 4, F), F32)],
        compiler_params=_params(("parallel", "arbitrary"), 40 * bt * bc * 4),
    )(p, p, p, p, conv_w, conv_w, conv_b.reshape(1, F2), conv_b.reshape(1, F2), d_act)
    d_w = jnp.concatenate([dwb[0, 0:3], dwb[1, 0:3]], axis=1)
    d_b = jnp.concatenate([dwb[0, 3], dwb[1, 3]], axis=0)
    return da, d_w, d_b


def _conv_bwd_data(da, conv_w, name):
    _, L, F = da.shape
    bt, bc = _conv_tiles(L, F)
    nf, r, nt = F // bc, bt // HALO, L // bt

    def body(da_ref, nxt_ref, w_ref, dp_ref):
        last = pl.program_id(0) == nt - 1
        x = da_ref[...].astype(F32)
        nx = nxt_ref[...].astype(F32)
        n0 = jnp.where(last, 0.0, nx[0:1, :])
        n1 = jnp.where(last, 0.0, nx[1:2, :])
        row = lax.broadcasted_iota(jnp.int32, (bt, 1), 0)
        x1 = jnp.where(row == bt - 1, n0, pltpu.roll(x, bt - 1, 0))
        x2 = jnp.where(row == bt - 1, n1, jnp.where(row == bt - 2, n0, pltpu.roll(x, bt - 2, 0)))
        w = w_ref[...]
        dp_ref[...] = (w[2:3, :] * x + w[1:2, :] * x1 + w[0:1, :] * x2).astype(BF16)

    return pl.pallas_call(
        body, name=name, grid=(nt, 2 * nf),
        in_specs=[pl.BlockSpec((None, bt, bc), lambda i, j: (j // nf, i, j % nf)),
                  pl.BlockSpec((None, HALO, bc), lambda i, j: (j // nf, jnp.minimum((i + 1) * r, nt * r - 1), j % nf)),
                  pl.BlockSpec((3, bc), lambda i, j: (0, j))],
        out_specs=pl.BlockSpec((bt, bc), lambda i, j: (i, j)), out_shape=jax.ShapeDtypeStruct((L, 2 * F), BF16),
        compiler_params=_params(("parallel", "parallel"), 16 * bt * bc * 4),
    )(da, da, conv_w)


def _pad_cols(a, n):
    return jnp.pad(a, ((0, 0), (0, n - a.shape[1])))


def _local_step(x, tgt, meta, W):
    S, D = x.shape
    n_meta = meta.shape[0]
    depth = len(W["w_in"])
    H = D // (2 * HEAD_DIM)
    WQ = 6 * H * HEAD_DIM
    L = -(-(S + n_meta) // ATT_BLOCK) * ATT_BLOCK
    tail = L - S - n_meta
    zeros_tail = jnp.zeros((tail, D), F32)
    h = jnp.concatenate([meta, x, zeros_tail], axis=0)
    tgt_p = jnp.concatenate([jnp.zeros((n_meta, D), F32), tgt, zeros_tail], axis=0)

    saved = []
    _, u1 = _resnorm_fwd(h, None, None, W["g_mix_pre"][0], "prenorm0")
    for l in range(depth):
        w_in = W["w_in"][l]
        w_qkv, w_f = w_in[:, :WQ], _pad_cols(w_in[:, WQ:], LANES)
        b_pad = jnp.pad(W["b_f"][l], (0, LANES - H)).reshape(1, LANES)
        g_heads = jnp.concatenate([W["g_sb"][l], W["g_fox"][l]], axis=0).reshape(2 * H, 1, HEAD_DIM)
        qkv = _mm(u1, w_qkv, "nn", BF16, f"qkv{l}")
        fl = _mm(u1, w_f, "nn", F32, f"flogit{l}")
        c = _gate_fwd(fl, b_pad, f"gate_fwd{l}")
        c_heads = c[:, :H].T
        crow, ccol = c_heads[:, None, :], c_heads[:, :, None]
        o, on, lse = _attn_fwd(qkv, crow, ccol, g_heads, H, f"attn_fwd{l}")
        mix = _mm(on, W["w_out"][l], "nn", F32, f"mix{l}")
        h_mid, u2 = _resnorm_fwd(h, mix, W["g_mix_post"][l], W["g_ffn_pre"][l], f"resnorm_a{l}")
        p = _mm(u2, W["w_up"][l], "nn", BF16, f"up{l}")
        act = _convgate_fwd(p, W["conv_w"][l], W["conv_b"][l], f"convgate{l}")
        ff = _mm(act, W["w_down"][l], "nn", F32, f"down{l}")
        g_next = W["g_mix_pre"][l + 1] if l + 1 < depth else None
        h_out, u1_next = _resnorm_fwd(h_mid, ff, W["g_ffn_post"][l], g_next, f"resnorm_b{l}")
        saved.append(dict(h_in=h, u1=u1, w_qkv=w_qkv, w_f=w_f, b_pad=b_pad, g_heads=g_heads, qkv=qkv, fl=fl, crow=crow,
                          ccol=ccol, o=o, on=on, lse=lse, mix=mix, h_mid=h_mid, u2=u2, p=p, act=act, ff=ff, h_out=h_out))
        h, u1 = h_out, u1_next

    loss, dh = _loss(h, tgt_p, n_meta, S, "loss")

    grads = {k: [None] * depth for k in ("g_mix_pre", "w_in", "b_f", "g_sb", "g_fox", "w_out", "g_mix_post", "g_ffn_pre",
                                         "w_up", "conv_w", "conv_b", "w_down", "g_ffn_post")}
    du1_next = None
    for l in reversed(range(depth)):
        s = saved[l]
        g_next = W["g_mix_pre"][l + 1] if l + 1 < depth else None
        dh, d_ff, grads["g_ffn_post"][l], dg_pre_next = _resnorm_bwd(
            dh, du1_next, s["h_out"], s["ff"], W["g_ffn_post"][l], g_next, f"resnorm_b_bwd{l}")
        if l + 1 < depth:
            grads["g_mix_pre"][l + 1] = dg_pre_next
        d_act = _mm(d_ff, W["w_down"][l], "nt", BF16, f"d_act{l}")
        grads["w_down"][l] = _mm(s["act"], d_ff, "tn", F32, f"dw_down{l}")
        da, grads["conv_w"][l], grads["conv_b"][l] = _convgate_bwd(s["p"], W["conv_w"][l], W["conv_b"][l], d_act, f"convgate_bwd{l}")
        dp = _conv_bwd_data(da, W["conv_w"][l], f"conv_bwd{l}")
        du2 = _mm(dp, W["w_up"][l], "nt", F32, f"d_u2{l}")
        grads["w_up"][l] = _mm(s["u2"], dp, "tn", F32, f"dw_up{l}")
        dh, d_mix, grads["g_mix_post"][l], grads["g_ffn_pre"][l] = _resnorm_bwd(
            dh, du2, s["h_mid"], s["mix"], W["g_mix_post"][l], W["g_ffn_pre"][l], f"resnorm_a_bwd{l}")
        d_on = _mm(d_mix, W["w_out"][l], "nt", F32, f"d_on{l}")
        grads["w_out"][l] = _mm(s["on"], d_mix, "tn", F32, f"dw_out{l}")
        dq, dk, dv, dg_heads, dcrow = _attn_bwd(s["qkv"], s["crow"], s["ccol"], s["g_heads"], s["o"], s["lse"], d_on, H, f"attn_bwd{l}")
        grads["g_sb"][l], grads["g_fox"][l] = dg_heads[:H], dg_heads[H:]
        dc = _pad_cols(dcrow[H:, 0, :].T, LANES)
        dfl, db = _gate_bwd(dc, s["fl"], s["b_pad"], f"gate_bwd{l}")
        grads["b_f"][l] = db[:H]
        Wh = H * HEAD_DIM
        d_qkv = jnp.concatenate([dq[:, :Wh], dk[:, :Wh], dv[:, :Wh], dq[:, Wh:], dk[:, Wh:], dv[:, Wh:]], axis=1)
        du1_next = _mm(d_qkv, s["w_qkv"], "nt", F32, f"d_u1{l}", a2=dfl, b2=s["w_f"])
        dw_qkv = _mm(s["u1"], d_qkv, "tn", F32, f"dw_qkv{l}")
        dw_f = _mm(s["u1"], dfl, "tn", F32, f"dw_f{l}")
        grads["w_in"][l] = jnp.concatenate([dw_qkv, dw_f[:, :H]], axis=1)
    dh0, _, _, grads["g_mix_pre"][0] = _resnorm_bwd(dh, du1_next, saved[0]["h_in"], None, None, W["g_mix_pre"][0], "prenorm0_bwd")
    return loss, dh0[n_meta:n_meta + S], dh0[:n_meta], grads


_MESH = pl.DeviceIdType.MESH
_ANY = pl.BlockSpec(memory_space=pl.ANY)


def _flip(v, bit):
    return 1 - v if bit else v


def _all_gather(x, name):
    def body(x_ref, out_ref, send_sems, recv_sems, local_sem):
        mx, my, mc = lax.axis_index("x"), lax.axis_index("y"), lax.axis_index("c")
        me, sibling = (mx, my, mc), (mx, my, 1 - mc)
        chips = [(1 - mx, my), (mx, 1 - my), (1 - mx, 1 - my)]

        def slot(px, py, pc):
            return out_ref.at[4 * px + 2 * py + pc]

        def copy(k, block, to, src=None):
            return pltpu.make_async_remote_copy(
                src_ref=slot(*block) if src is None else src, dst_ref=slot(*block),
                send_sem=send_sems.at[k], recv_sem=recv_sems.at[k], device_id=to, device_id_type=_MESH)

        mine = pltpu.make_async_copy(x_ref, slot(*me), local_sem)
        mine.start()
        first = [copy(0, me, sibling, src=x_ref)]
        first += [copy(1 + j, me, (*chip, mc), src=x_ref) for j, chip in enumerate(chips)]
        for cp in first:
            cp.start()
        passed = [copy(4 + j, (*chip, mc), sibling) for j, chip in enumerate(chips)]
        for j, chip in enumerate(chips):
            copy(1 + j, (*chip, mc), me).wait_recv()
            passed[j].start()
        copy(0, sibling, me).wait_recv()
        for j, chip in enumerate(chips):
            copy(4 + j, (*chip, 1 - mc), me).wait_recv()
        for cp in first + passed:
            cp.wait_send()
        mine.wait()

    return pl.pallas_call(
        body, name=name, out_shape=jax.ShapeDtypeStruct((N_DEV, *x.shape), x.dtype),
        in_specs=[_ANY], out_specs=_ANY,
        scratch_shapes=[pltpu.SemaphoreType.DMA((7,)), pltpu.SemaphoreType.DMA((7,)), pltpu.SemaphoreType.DMA],
    )(x)


def _exchange(g, name):
    def body(g_ref, land_ref, send_sems, recv_sems, local_sem):
        mx, my, mc = lax.axis_index("x"), lax.axis_index("y"), lax.axis_index("c")
        me = 4 * mx + 2 * my + mc
        local = pltpu.make_async_copy(g_ref.at[me], land_ref.at[me], local_sem)
        local.start()
        sends, recvs = [], []
        for k in range(1, N_DEV):
            peer = (_flip(mx, k & 4), _flip(my, k & 2), _flip(mc, k & 1))
            p = 4 * peer[0] + 2 * peer[1] + peer[2]
            sends.append(pltpu.make_async_remote_copy(
                src_ref=g_ref.at[p], dst_ref=land_ref.at[me], send_sem=send_sems.at[k - 1], recv_sem=recv_sems.at[k - 1],
                device_id=peer, device_id_type=_MESH))
            recvs.append(pltpu.make_async_remote_copy(
                src_ref=g_ref.at[p], dst_ref=land_ref.at[p], send_sem=send_sems.at[k - 1], recv_sem=recv_sems.at[k - 1],
                device_id=peer, device_id_type=_MESH))
        for cp in sends:
            cp.start()
        for cp in recvs:
            cp.wait_recv()
        for cp in sends:
            cp.wait_send()
        local.wait()

    return pl.pallas_call(
        body, name=name, out_shape=jax.ShapeDtypeStruct(g.shape, g.dtype), in_specs=[_ANY], out_specs=_ANY,
        scratch_shapes=[pltpu.SemaphoreType.DMA((7,)), pltpu.SemaphoreType.DMA((7,)), pltpu.SemaphoreType.DMA],
    )(g)


def _adamw_landed(land, w, m, v, name):
    _, n, R, C = land.shape
    tr = _divisor(R, (64, 32, 16, 8))
    blk = pl.BlockSpec((None, tr, C), lambda l, r: (l, r, 0))

    def body(land_ref, w_ref, m_ref, v_ref, g_ref, d_ref, mo_ref, vo_ref):
        g = land_ref[0].astype(F32)
        for s in range(1, N_DEV):
            g = g + land_ref[s].astype(F32)
        m_new = ADAM_B1 * m_ref[...] + (1.0 - ADAM_B1) * g
        v_new = ADAM_B2 * v_ref[...] + (1.0 - ADAM_B2) * (g * g)
        m_hat = m_new / (1.0 - ADAM_B1 ** ADAM_STEP)
        v_hat = v_new / (1.0 - ADAM_B2 ** ADAM_STEP)
        g_ref[...] = g
        d_ref[...] = -ADAM_LR * (m_hat / (jnp.sqrt(v_hat) + ADAM_EPS) + ADAM_WD * w_ref[...])
        mo_ref[...] = m_new
        vo_ref[...] = v_new

    est = 2 * tr * C * (8 * jnp.dtype(land.dtype).itemsize + 7 * 4) * 9 // 8
    return pl.pallas_call(
        body, name=name, grid=(n, R // tr),
        in_specs=[pl.BlockSpec((N_DEV, None, tr, C), lambda l, r: (0, l, r, 0)), blk, blk, blk],
        out_specs=[blk] * 4, out_shape=[jax.ShapeDtypeStruct((n, R, C), F32)] * 4,
        compiler_params=_params(("parallel", "parallel"), est),
    )(land, w, m, v)


def _pack(arrs):
    flat = jnp.concatenate([a.reshape(-1).astype(F32) for a in arrs])
    rows = -(-flat.shape[0] // (8 * LANES)) * 8
    return jnp.pad(flat, (0, rows * LANES - flat.shape[0])).reshape(rows, LANES)


def _unpack(packed, shapes):
    flat, out, at = packed.reshape(-1), [], 0
    for s in shapes:
        n = math.prod(s)
        out.append(flat[at:at + n].reshape(s))
        at += n
    return out


_BIG = ("w_in", "w_out", "w_up", "w_down")
_REPLICATED = ("g_mix_pre", "b_f", "g_sb", "g_fox", "g_mix_post", "g_ffn_pre", "conv_b", "g_ffn_post")
_ORDER = ("meta", "g_mix_pre", "w_in", "b_f", "g_sb", "g_fox", "w_out", "g_mix_post", "g_ffn_pre", "w_up", "conv_w",
          "conv_b", "w_down", "g_ffn_post")
_COLUMN_SHARDED = ("w_in", "w_up")


def kernel(x, meta, g_mix_pre, w_in, b_f, g_sb, g_fox, w_out, g_mix_post, g_ffn_pre, w_up, conv_w, conv_b, w_down, g_ffn_post, loss_target, m_meta, m_g_mix_pre, m_w_in, m_b_f, m_g_sb, m_g_fox, m_w_out, m_g_mix_post, m_g_ffn_pre, m_w_up, m_conv_w, m_conv_b, m_w_down, m_g_ffn_post, v_meta, v_g_mix_pre, v_w_in, v_b_f, v_g_sb, v_g_fox, v_w_out, v_g_mix_post, v_g_ffn_pre, v_w_up, v_conv_w, v_conv_b, v_w_down, v_g_ffn_post):
    w = dict(meta=meta, g_mix_pre=g_mix_pre, w_in=w_in, b_f=b_f, g_sb=g_sb, g_fox=g_fox, w_out=w_out, g_mix_post=g_mix_post,
             g_ffn_pre=g_ffn_pre, w_up=w_up, conv_w=conv_w, conv_b=conv_b, w_down=w_down, g_ffn_post=g_ffn_post)
    m = dict(meta=m_meta, g_mix_pre=m_g_mix_pre, w_in=m_w_in, b_f=m_b_f, g_sb=m_g_sb, g_fox=m_g_fox, w_out=m_w_out,
             g_mix_post=m_g_mix_post, g_ffn_pre=m_g_ffn_pre, w_up=m_w_up, conv_w=m_conv_w, conv_b=m_conv_b, w_down=m_w_down,
             g_ffn_post=m_g_ffn_post)
    v = dict(meta=v_meta, g_mix_pre=v_g_mix_pre, w_in=v_w_in, b_f=v_b_f, g_sb=v_g_sb, g_fox=v_g_fox, w_out=v_w_out,
             g_mix_post=v_g_mix_post, g_ffn_pre=v_g_ffn_pre, w_up=v_w_up, conv_w=v_conv_w, conv_b=v_conv_b, w_down=v_w_down,
             g_ffn_post=v_g_ffn_post)
    depth = w_in.shape[0]
    me = 4 * lax.axis_index("x") + 2 * lax.axis_index("y") + lax.axis_index("c")

    full = {}
    for name in _BIG:
        g = _all_gather(w[name].astype(BF16), f"gather_{name}")
        if name in _COLUMN_SHARDED:
            full[name] = [jnp.transpose(g[:, l], (1, 0, 2)).reshape(g.shape[2], -1) for l in range(depth)]
        else:
            full[name] = [g[:, l].reshape(-1, g.shape[3]) for l in range(depth)]
    small_shapes = [conv_w.shape, meta.shape]
    gs = _all_gather(_pack([conv_w, meta]), "gather_small")
    parts = [_unpack(gs[d], small_shapes) for d in range(N_DEV)]
    conv_full = jnp.concatenate([p[0] for p in parts], axis=2)
    meta_full = jnp.concatenate([p[1] for p in parts], axis=1)

    W = dict(full)
    W["conv_w"] = [conv_full[l] for l in range(depth)]
    for name in _REPLICATED:
        W[name] = [w[name][l] for l in range(depth)]
    loss, grad_x, d_meta, grads = _local_step(x[0], loss_target[0], meta_full, W)

    out = {}
    for name in _BIG:
        R, C = w[name].shape[1:]
        if name in _COLUMN_SHARDED:
            blocks = jnp.stack([jnp.transpose(grads[name][l].reshape(R, N_DEV, C), (1, 0, 2)) for l in range(depth)], axis=1)
        else:
            blocks = jnp.stack([grads[name][l].reshape(N_DEV, R, C) for l in range(depth)], axis=1)
        land = _exchange(blocks.astype(BF16), f"exchange_{name}")
        out[name] = _adamw_landed(land, w[name], m[name], v[name], f"adamw_{name}")

    Fs, Ms = conv_w.shape[2], meta.shape[1]
    d_conv = jnp.stack(grads["conv_w"], axis=0)
    blocks = jnp.stack([_pack([d_conv[:, :, d * Fs:(d + 1) * Fs], d_meta[:, d * Ms:(d + 1) * Ms]]) for d in range(N_DEV)])
    land = _exchange(blocks, "exchange_small")
    res = _adamw_landed(land[:, None], _pack([conv_w, meta])[None], _pack([m_conv_w, m_meta])[None],
                        _pack([v_conv_w, v_meta])[None], "adamw_small")
    for i, r in enumerate(res):
        cw, mt = _unpack(r[0], small_shapes)
        out.setdefault("conv_w", [None] * 4)[i] = cw
        out.setdefault("meta", [None] * 4)[i] = mt

    rep_shapes = [()] + [w[name].shape for name in _REPLICATED]
    mine = _pack([loss] + [jnp.stack(grads[name], axis=0) for name in _REPLICATED])
    land = _all_gather(mine, "gather_replicated")
    zero = jnp.zeros((), F32)
    res = _adamw_landed(land[:, None], _pack([zero] + [w[n] for n in _REPLICATED])[None],
                        _pack([zero] + [m[n] for n in _REPLICATED])[None],
                        _pack([zero + 1.0] + [v[n] for n in _REPLICATED])[None], "adamw_replicated")
    for i, r in enumerate(res):
        vals = _unpack(r[0], rep_shapes)
        if i == 0:
            loss_total = vals[0]
        for name, val in zip(_REPLICATED, vals[1:]):
            out.setdefault(name, [None] * 4)[i] = val

    return (loss_total, grad_x[None], *[out[n][0] for n in _ORDER], *[out[n][1] for n in _ORDER],
            *[out[n][2] for n in _ORDER], *[out[n][3] for n in _ORDER])
```

```python
import functools
import math

import jax
import jax.numpy as jnp
from jax import lax
from jax.experimental import pallas as pl
from jax.experimental.pallas import tpu as pltpu

F32, BF16 = jnp.float32, jnp.bfloat16
HEAD_DIM = 128
LANES = 128
EPS = 1e-6
NEG_INF = -1e30
ATT_BLOCK = 256
N_DEV = 8
V7X_VMEM_BUDGET = 56 * 1024 * 1024

ADAM_LR, ADAM_B1, ADAM_B2, ADAM_EPS, ADAM_WD, ADAM_STEP = 0.001, 0.9, 0.999, 1e-08, 0.01, 10


def _divisor(n, cands):
    for c in cands:
        if c <= n and n % c == 0:
            return c
    raise ValueError(f"no tile for {n} among {cands}")


def _params(sem, est_bytes):
    limit = int(min(V7X_VMEM_BUDGET, max(16 * 1024 * 1024, est_bytes * 5 // 4 + (2 << 20))))
    return pltpu.CompilerParams(dimension_semantics=sem, vmem_limit_bytes=limit)


def _nbytes(shape, dtype):
    return math.prod(shape) * jnp.dtype(dtype).itemsize


_DN = {"nn": (((1,), (0,)), ((), ())), "nt": (((1,), (1,)), ((), ())), "tn": (((0,), (0,)), ((), ()))}
_ROW_TILES = (1088, 544, 272, 512, 256, 128, 64, 32, 16, 8)
_COL_TILES = (512, 256, 128)


def _ktile(k, cap):
    if k <= cap:
        return k
    for t in range(cap - cap % LANES, 0, -LANES):
        if k % t == 0:
            return t
    raise ValueError(k)


def _mm(a, b, mode, out_dtype, name, a2=None, b2=None):
    if mode == "nn":
        (M, K), (_, N) = a.shape, b.shape
    elif mode == "nt":
        (M, K), (N, _) = a.shape, b.shape
    else:
        (K, M), (_, N) = a.shape, b.shape
    if mode == "tn":
        tm, tn, tk = _divisor(M, _COL_TILES), _divisor(N, _COL_TILES), _ktile(K, 4608)
    else:
        tm, tn, tk = _divisor(M, _ROW_TILES), _divisor(N, _COL_TILES), _ktile(K, 3072)
    nk = K // tk
    a_spec = {"nn": pl.BlockSpec((tm, tk), lambda i, j, k: (i, k)),
              "nt": pl.BlockSpec((tm, tk), lambda i, j, k: (i, k)),
              "tn": pl.BlockSpec((tk, tm), lambda i, j, k: (k, i))}[mode]
    b_spec = {"nn": pl.BlockSpec((tk, tn), lambda i, j, k: (k, j)),
              "nt": pl.BlockSpec((tn, tk), lambda i, j, k: (j, k)),
              "tn": pl.BlockSpec((tk, tn), lambda i, j, k: (k, j))}[mode]
    dn = _DN[mode]
    extra = a2 is not None
    in_specs, args = [a_spec, b_spec], [a, b]
    if extra:
        k2 = a2.shape[1]
        in_specs += [pl.BlockSpec((tm, k2), lambda i, j, k: (i, 0)), pl.BlockSpec((tn, k2), lambda i, j, k: (j, 0))]
        args += [a2, b2]

    def body(*refs):
        if extra:
            a_ref, b_ref, a2_ref, b2_ref, o_ref, acc = refs
        else:
            a_ref, b_ref, o_ref, acc = refs
        part = lax.dot_general(a_ref[...], b_ref[...], dn, preferred_element_type=F32)
        if nk == 1:
            if extra:
                part = part + lax.dot_general(a2_ref[...], b2_ref[...], _DN["nt"], preferred_element_type=F32)
            o_ref[...] = part.astype(o_ref.dtype)
            return
        kk = pl.program_id(2)

        @pl.when(kk == 0)
        def _():
            if extra:
                acc[...] = part + lax.dot_general(a2_ref[...], b2_ref[...], _DN["nt"], preferred_element_type=F32)
            else:
                acc[...] = part

        @pl.when(kk > 0)
        def _():
            acc[...] += part

        @pl.when(kk == nk - 1)
        def _():
            o_ref[...] = acc[...].astype(o_ref.dtype)

    est = 2 * (tm * tk + tk * tn) * 2 + 2 * _nbytes((tm, tn), out_dtype) + tm * tn * 4
    return pl.pallas_call(
        body, name=name, grid=(M // tm, N // tn, nk), in_specs=in_specs,
        out_specs=pl.BlockSpec((tm, tn), lambda i, j, k: (i, j)),
        out_shape=jax.ShapeDtypeStruct((M, N), out_dtype),
        scratch_shapes=[pltpu.VMEM((tm, tn), F32)],
        compiler_params=_params(("parallel", "parallel", "arbitrary"), est),
    )(*args)


def _rms(x, g):
    r = lax.rsqrt(jnp.mean(x * x, axis=-1, keepdims=True) + EPS)
    return x * r * g


def _rms_bwd(x, g, dout):
    r = lax.rsqrt(jnp.mean(x * x, axis=-1, keepdims=True) + EPS)
    xhat = x * r
    dxh = dout * g
    dx = r * (dxh - xhat * jnp.mean(dxh * xhat, axis=-1, keepdims=True))
    return dx, dout * xhat


def _row_tile(L):
    return _divisor(L, (272, 256, 128, 64, 32, 16))


def _resnorm_fwd(h, y, g_post, g_pre, name):
    L, D = h.shape
    bt = _row_tile(L)
    has_y, has_pre = y is not None, g_pre is not None
    row = pl.BlockSpec((bt, D), lambda i: (i, 0))
    vec = pl.BlockSpec((1, D), lambda i: (0, 0))
    args, in_specs = [h], [row]
    if has_y:
        args += [y, g_post.reshape(1, D)]
        in_specs += [row, vec]
    if has_pre:
        args += [g_pre.reshape(1, D)]
        in_specs += [vec]
    out_shape, out_specs = [], []
    if has_y:
        out_shape.append(jax.ShapeDtypeStruct((L, D), F32))
        out_specs.append(row)
    if has_pre:
        out_shape.append(jax.ShapeDtypeStruct((L, D), BF16))
        out_specs.append(row)

    def body(*refs):
        refs = list(refs)
        h_ref = refs.pop(0)
        hn = h_ref[...]
        if has_y:
            y_ref, gp_ref = refs.pop(0), refs.pop(0)
            hn = hn + _rms(y_ref[...], gp_ref[...])
        if has_pre:
            g_ref = refs.pop(0)
        if has_y:
            refs.pop(0)[...] = hn
        if has_pre:
            refs.pop(0)[...] = _rms(hn, g_ref[...]).astype(BF16)

    outs = pl.pallas_call(
        body, name=name, grid=(L // bt,), in_specs=in_specs, out_specs=out_specs, out_shape=out_shape,
        compiler_params=_params(("parallel",), 10 * bt * D * 4),
    )(*args)
    outs = list(outs)
    h_new = outs.pop(0) if has_y else h
    u = outs.pop(0) if has_pre else None
    return h_new, u


def _resnorm_bwd(dh_direct, du, h_new, y, g_post, g_pre, name):
    L, D = dh_direct.shape
    bt = _row_tile(L)
    has_y, has_pre = y is not None, du is not None
    row = pl.BlockSpec((bt, D), lambda i: (i, 0))
    vec = pl.BlockSpec((1, D), lambda i: (0, 0))
    acc = pl.BlockSpec((8, D), lambda i: (0, 0))
    args, in_specs = [dh_direct], [row]
    if has_pre:
        args += [du, h_new, g_pre.reshape(1, D)]
        in_specs += [row, row, vec]
    if has_y:
        args += [y, g_post.reshape(1, D)]
        in_specs += [row, vec]
    out_shape, out_specs = [], []
    if has_pre:
        out_shape += [jax.ShapeDtypeStruct((L, D), F32), jax.ShapeDtypeStruct((8, D), F32)]
        out_specs += [row, acc]
    if has_y:
        out_shape += [jax.ShapeDtypeStruct((L, D), BF16), jax.ShapeDtypeStruct((8, D), F32)]
        out_specs += [row, acc]

    def colsum8(v):
        return jnp.sum(v.reshape(bt // 8, 8, D), axis=0)

    def body(*refs):
        refs = list(refs)
        first = pl.program_id(0) == 0
        dh = refs.pop(0)[...]
        if has_pre:
            du_ref, hn_ref, g_ref = refs.pop(0), refs.pop(0), refs.pop(0)
        if has_y:
            y_ref, gp_ref = refs.pop(0), refs.pop(0)
        if has_pre:
            dh_ref, dgpre_ref = refs.pop(0), refs.pop(0)
            dx, dgp = _rms_bwd(hn_ref[...], g_ref[...], du_ref[...].astype(F32))
            dh = dh + dx
            dh_ref[...] = dh

            @pl.when(first)
            def _():
                dgpre_ref[...] = jnp.zeros_like(dgpre_ref)
            dgpre_ref[...] += colsum8(dgp)
        if has_y:
            dy_ref, dgpost_ref = refs.pop(0), refs.pop(0)
            dy, dgq = _rms_bwd(y_ref[...], gp_ref[...], dh)
            dy_ref[...] = dy.astype(BF16)

            @pl.when(first)
            def _():
                dgpost_ref[...] = jnp.zeros_like(dgpost_ref)
            dgpost_ref[...] += colsum8(dgq)

    outs = list(pl.pallas_call(
        body, name=name, grid=(L // bt,), in_specs=in_specs, out_specs=out_specs, out_shape=out_shape,
        compiler_params=_params(("arbitrary",), 14 * bt * D * 4),
    )(*args))
    dh, dg_pre, dy, dg_post = dh_direct, None, None, None
    if has_pre:
        dh, dg_pre = outs.pop(0), outs.pop(0).sum(0)
    if has_y:
        dy, dg_post = outs.pop(0), outs.pop(0).sum(0)
    return dh, dy, dg_post, dg_pre


def _loss(h, tgt, n_meta, seq, name):
    L, D = h.shape
    bt = _row_tile(L)
    row = pl.BlockSpec((bt, D), lambda i: (i, 0))

    def body(h_ref, t_ref, dy_ref, loss_ref):
        i = pl.program_id(0)
        r = i * bt + lax.broadcasted_iota(jnp.int32, (bt, 1), 0)
        valid = (r >= n_meta) & (r < n_meta + seq)
        e = jnp.where(valid, h_ref[...] - t_ref[...], 0.0)
        dy_ref[...] = e * (1.0 / D)

        @pl.when(i == 0)
        def _():
            loss_ref[...] = jnp.zeros_like(loss_ref)
        loss_ref[...] += 0.5 * jnp.sum(jnp.sum(e * e, axis=-1, keepdims=True) * (1.0 / D))

    dy, loss = pl.pallas_call(
        body, name=name, grid=(L // bt,), in_specs=[row, row],
        out_specs=[row, pl.BlockSpec((8, LANES), lambda i: (0, 0))],
        out_shape=[jax.ShapeDtypeStruct((L, D), F32), jax.ShapeDtypeStruct((8, LANES), F32)],
        compiler_params=_params(("arbitrary",), 8 * bt * D * 4),
    )(h, tgt)
    return loss[0, 0], dy


def _split3(x):
    x1 = x.astype(BF16)
    r1 = x - x1.astype(F32)
    x2 = r1.astype(BF16)
    x3 = (r1 - x2.astype(F32)).astype(BF16)
    return x1, x2, x3


def _tri_dot3(tri, x):
    x1, x2, x3 = _split3(x)
    d = functools.partial(jnp.dot, preferred_element_type=F32)
    return d(tri, x1) + d(tri, x2) + d(tri, x3)


def _gate_fwd(fl, b_pad, name):
    L = fl.shape[0]
    bt = 128
    blk = pl.BlockSpec((bt, LANES), lambda i: (i, 0))

    def body(fl_ref, b_ref, c_ref, carry):
        @pl.when(pl.program_id(0) == 0)
        def _():
            carry[...] = jnp.zeros_like(carry)
        x = fl_ref[...] + b_ref[...]
        lf = jnp.minimum(x, 0.0) - jnp.log(1.0 + jnp.exp(-jnp.abs(x)))
        r = lax.broadcasted_iota(jnp.int32, (bt, bt), 0)
        s = lax.broadcasted_iota(jnp.int32, (bt, bt), 1)
        tri = (s <= r).astype(BF16)
        c = _tri_dot3(tri, lf) + carry[...]
        c_ref[...] = c
        carry[...] = c[bt - 1:bt, :]

    return pl.pallas_call(
        body, name=name, grid=(L // bt,), in_specs=[blk, pl.BlockSpec((1, LANES), lambda i: (0, 0))],
        out_specs=blk, out_shape=jax.ShapeDtypeStruct((L, LANES), F32),
        scratch_shapes=[pltpu.VMEM((1, LANES), F32)],
        compiler_params=_params(("arbitrary",), 1 << 20),
    )(fl, b_pad)


def _gate_bwd(dc, fl, b_pad, name):
    L = fl.shape[0]
    bt = 128
    n = L // bt
    blk = pl.BlockSpec((bt, LANES), lambda i: (n - 1 - i, 0))

    def body(dc_ref, fl_ref, b_ref, dfl_ref, db_ref, carry):
        @pl.when(pl.program_id(0) == 0)
        def _():
            carry[...] = jnp.zeros_like(carry)
            db_ref[...] = jnp.zeros_like(db_ref)
        r = lax.broadcasted_iota(jnp.int32, (bt, bt), 0)
        s = lax.broadcasted_iota(jnp.int32, (bt, bt), 1)
        tri = (s >= r).astype(BF16)
        dlf = _tri_dot3(tri, dc_ref[...]) + carry[...]
        carry[...] = dlf[0:1, :]
        x = fl_ref[...] + b_ref[...]
        dfl = dlf / (1.0 + jnp.exp(x))
        dfl_ref[...] = dfl.astype(BF16)
        db_ref[...] += jnp.sum(dfl.reshape(bt // 8, 8, LANES), axis=0)

    dfl, db = pl.pallas_call(
        body, name=name, grid=(n,), in_specs=[blk, blk, pl.BlockSpec((1, LANES), lambda i: (0, 0))],
        out_specs=[blk, pl.BlockSpec((8, LANES), lambda i: (0, 0))],
        out_shape=[jax.ShapeDtypeStruct((L, LANES), BF16), jax.ShapeDtypeStruct((8, LANES), F32)],
        scratch_shapes=[pltpu.VMEM((1, LANES), F32)],
        compiler_params=_params(("arbitrary",), 1 << 20),
    )(dc, fl, b_pad)
    return dfl, db.sum(0)


def _split2(x):
    hi = x.astype(BF16)
    return hi, (x - hi.astype(F32)).astype(BF16)


def _dot_hi_lo(x, w2):
    hi, lo = _split2(x)
    return jnp.dot(jnp.concatenate([hi, lo], axis=1), w2, preferred_element_type=F32)


def _suffix_sums(x, tri2):
    return _dot_hi_lo(x, tri2)


def _dot_nt(a, b):
    return lax.dot_general(a, b, _DN["nt"], preferred_element_type=F32)


def _suffix_matrix(bk):
    j = lax.broadcasted_iota(jnp.int32, (2 * bk, bk), 0) % bk
    s = lax.broadcasted_iota(jnp.int32, (2 * bk, bk), 1)
    return (j >= s).astype(BF16)


LOG2E = 1.4426950408889634


def _sb_window(q, k, t_pos, ks, bk, scale2, carry_c, tri, masked):
    width = k.shape[0]
    z = _dot_nt(q, k) * scale2
    sp = jnp.maximum(z, 0.0) + jnp.log2(1.0 + jnp.exp2(-jnp.abs(z)))
    if masked:
        mask = ks + lax.broadcasted_iota(jnp.int32, (1, width), 1) < t_pos
        lkm = jnp.where(mask, -sp, 0.0)
    else:
        mask, lkm = None, -sp
    sums, run = _window_suffix_sums(lkm, bk, tri, carry_c)
    a = jnp.exp2(z + sums)
    if masked:
        a = jnp.where(mask, a, 0.0)
    return z, sp, mask, a, run


def _window_suffix_sums(x, bk, tri, carry):
    nb = x.shape[1] // bk
    parts, run = [None] * nb, carry
    for b in reversed(range(nb)):
        cs = _suffix_sums(x[:, b * bk:(b + 1) * bk], tri)
        parts[b] = run + cs
        run = run + cs[:, 0:1]
    return (parts[0] if nb == 1 else jnp.concatenate(parts, axis=1)), run


KEY_WINDOWS = (4, 2, 1)


def _key_tiles(i, bk, step, carry):
    carry = step(pl.multiple_of(i * bk, bk), bk, carry, True)
    done = 0
    for w in sorted(KEY_WINDOWS):
        wider = [x for x in KEY_WINDOWS if x > w]
        trips = (i % min(wider)) // w if wider else (i - done) // w

        def body(p, c, w=w, done=done):
            return step(pl.multiple_of((i - done - w * (p + 1)) * bk, bk), w * bk, c, False)

        carry = lax.fori_loop(0, trips, body, carry)
        done = done + trips * w
    return carry


def _attn_specs(H, L, bq):
    W3 = 3 * H

    def col(role):
        return lambda h, i: (h // H) * W3 + role * H + h % H

    q_spec = pl.BlockSpec((bq, HEAD_DIM), lambda h, i: (i, col(0)(h, i)))
    k_spec = pl.BlockSpec((L, HEAD_DIM), lambda h, i: (0, col(1)(h, i)))
    v_spec = pl.BlockSpec((L, HEAD_DIM), lambda h, i: (0, col(2)(h, i)))
    crow_spec = pl.BlockSpec((None, 1, L), lambda h, i: (jnp.maximum(h - H, 0), 0, 0))
    ccol_spec = pl.BlockSpec((None, bq, 1), lambda h, i: (jnp.maximum(h - H, 0), i, 0))
    g_spec = pl.BlockSpec((None, 1, HEAD_DIM), lambda h, i: (h, 0, 0))
    tile = pl.BlockSpec((bq, HEAD_DIM), lambda h, i: (i, h))
    stat = pl.BlockSpec((None, bq, 1), lambda h, i: (h, i, 0))
    return q_spec, k_spec, v_spec, crow_spec, ccol_spec, g_spec, tile, stat


def _attn_fwd(qkv, crow, ccol, g_heads, H, name):
    L = qkv.shape[0]
    bq = bk = min(ATT_BLOCK, L)
    nq = L // bq
    scale = HEAD_DIM ** -0.5
    scale2 = scale * LOG2E
    q_spec, k_spec, v_spec, crow_spec, ccol_spec, g_spec, tile, stat = _attn_specs(H, L, bq)

    def body(q_ref, k_ref, v_ref, crow_ref, ccol_ref, g_ref, o_ref, on_ref, lse_ref):
        h, i = pl.program_id(0), pl.program_id(1)
        q = q_ref[...]
        t_pos = i * bq + lax.broadcasted_iota(jnp.int32, (bq, 1), 0)

        def finish(o):
            o_ref[...] = o
            on_ref[...] = _rms(o, g_ref[...]).astype(BF16)

        @pl.when(h < H)
        def _stick_breaking():
            tri = _suffix_matrix(bk)

            def step(ks, rows, carry, masked):
                c, acc = carry
                k = k_ref[pl.ds(ks, rows), :]
                v = v_ref[pl.ds(ks, rows), :]
                _, _, _, a, c = _sb_window(q, k, t_pos, ks, bk, scale2, c, tri, masked)
                acc = acc + _dot_hi_lo(a, jnp.concatenate([v, v], axis=0))
                return c, acc

            _, acc = _key_tiles(i, bk, step, (jnp.zeros((bq, 1), F32), jnp.zeros((bq, HEAD_DIM), F32)))
            finish(acc)
            lse_ref[...] = jnp.zeros_like(lse_ref)

        @pl.when(h >= H)
        def _forgetting():
            cq = ccol_ref[...] * LOG2E

            def step(ks, rows, carry, masked):
                m, l, acc = carry
                k = k_ref[pl.ds(ks, rows), :]
                v = v_ref[pl.ds(ks, rows), :]
                s = _dot_nt(q, k) * scale2 + (cq - crow_ref[:, pl.ds(ks, rows)] * LOG2E)
                if masked:
                    s = jnp.where(ks + lax.broadcasted_iota(jnp.int32, (1, rows), 1) <= t_pos, s, NEG_INF)
                m_new = jnp.maximum(m, jnp.max(s, axis=-1, keepdims=True))
                alpha = jnp.exp2(m - m_new)
                p = jnp.exp2(s - m_new)
                l = alpha * l + jnp.sum(p, axis=-1, keepdims=True)
                acc = alpha * acc + _dot_hi_lo(p, jnp.concatenate([v, v], axis=0))
                return m_new, l, acc

            init = (jnp.full((bq, 1), NEG_INF, F32), jnp.zeros((bq, 1), F32), jnp.zeros((bq, HEAD_DIM), F32))
            m, l, acc = _key_tiles(i, bk, step, init)
            finish(acc / l)
            lse_ref[...] = m + jnp.log2(l)

    W2 = 2 * H * HEAD_DIM
    est = 4 * L * HEAD_DIM * 2 + 32 * bq * bk * 4 + (4 << 20)
    return pl.pallas_call(
        body, name=name, grid=(2 * H, nq),
        in_specs=[q_spec, k_spec, v_spec, crow_spec, ccol_spec, g_spec],
        out_specs=[tile, tile, stat],
        out_shape=[jax.ShapeDtypeStruct((L, W2), F32), jax.ShapeDtypeStruct((L, W2), BF16),
                   jax.ShapeDtypeStruct((2 * H, L, 1), F32)],
        compiler_params=_params(("parallel", "arbitrary"), est),
    )(qkv, qkv, qkv, crow, ccol, g_heads)


def _attn_bwd(qkv, crow, ccol, g_heads, o, lse, d_on, H, name):
    L = qkv.shape[0]
    bq = bk = min(ATT_BLOCK, L)
    nq = L // bq
    scale = HEAD_DIM ** -0.5
    scale2 = scale * LOG2E
    q_spec, k_spec, v_spec, crow_spec, ccol_spec, g_spec, tile, stat = _attn_specs(H, L, bq)
    full = pl.BlockSpec((L, HEAD_DIM), lambda h, i: (0, h))
    dg_spec = pl.BlockSpec((None, 1, HEAD_DIM), lambda h, i: (h, 0, 0))
    dc_spec = pl.BlockSpec((None, 1, L), lambda h, i: (h, 0, 0))

    def body(q_ref, k_ref, v_ref, crow_ref, ccol_ref, g_ref, o_ref, lse_ref, don_ref,
             dq_ref, dk_ref, dv_ref, dg_ref, dc_ref, dkt_acc, dvt_acc):
        h, i = pl.program_id(0), pl.program_id(1)

        @pl.when(i == 0)
        def _():
            dkt_acc[...] = jnp.zeros_like(dkt_acc)
            dvt_acc[...] = jnp.zeros_like(dvt_acc)
            dg_ref[...] = jnp.zeros_like(dg_ref)
            dc_ref[...] = jnp.zeros_like(dc_ref)

        q = q_ref[...]
        t_pos = i * bq + lax.broadcasted_iota(jnp.int32, (bq, 1), 0)
        o_t = o_ref[...]
        d_o, dg = _rms_bwd(o_t, g_ref[...], don_ref[...])
        dg_ref[...] += jnp.sum(dg, axis=0, keepdims=True)
        d_ob = d_o.astype(BF16)
        dsum = jnp.sum(d_ob.astype(F32) * o_t, axis=-1, keepdims=True)
        q_t = q.astype(F32).T.astype(BF16)
        d_obt = d_o.T.astype(BF16)

        @pl.when(h < H)
        def _stick_breaking():
            tri = _suffix_matrix(bk)

            def step(ks, rows, carry, masked):
                c, gs, dq = carry
                k = k_ref[pl.ds(ks, rows), :]
                v = v_ref[pl.ds(ks, rows), :]
                z, sp, mask, a, c = _sb_window(q, k, t_pos, ks, bk, scale2, c, tri, masked)
                g_w = a * _dot_nt(d_ob, v)
                later, gs = _window_suffix_sums(g_w, bk, tri, gs)
                prefix = dsum - (later - g_w)
                dz = (g_w - jnp.exp2(z - sp) * prefix) * scale
                if masked:
                    dz = jnp.where(mask, dz, 0.0)
                dzb = dz.astype(BF16)
                dq = dq + jnp.dot(dzb, k, preferred_element_type=F32)
                dkt_acc[:, pl.ds(ks, rows)] += jnp.dot(q_t, dzb, preferred_element_type=F32)
                dvt_acc[:, pl.ds(ks, rows)] += jnp.dot(d_obt, a.astype(BF16), preferred_element_type=F32)
                return c, gs, dq

            z1 = jnp.zeros((bq, 1), F32)
            _, _, dq = _key_tiles(i, bk, step, (z1, z1, jnp.zeros((bq, HEAD_DIM), F32)))
            dq_ref[...] = dq.astype(BF16)

        @pl.when(h >= H)
        def _forgetting():
            cq = ccol_ref[...] * LOG2E - lse_ref[...]

            def step(ks, rows, dq, masked):
                k = k_ref[pl.ds(ks, rows), :]
                v = v_ref[pl.ds(ks, rows), :]
                p = jnp.exp2(_dot_nt(q, k) * scale2 + (cq - crow_ref[:, pl.ds(ks, rows)] * LOG2E))
                if masked:
                    p = jnp.where(ks + lax.broadcasted_iota(jnp.int32, (1, rows), 1) <= t_pos, p, 0.0)
                ds = p * (_dot_nt(d_ob, v) - dsum)
                dsb = (ds * scale).astype(BF16)
                dq = dq + jnp.dot(dsb, k, preferred_element_type=F32)
                dkt_acc[:, pl.ds(ks, rows)] += jnp.dot(q_t, dsb, preferred_element_type=F32)
                dvt_acc[:, pl.ds(ks, rows)] += jnp.dot(d_obt, p.astype(BF16), preferred_element_type=F32)
                dc_ref[:, pl.ds(ks, rows)] += -jnp.sum(ds, axis=0, keepdims=True)
                return dq

            dq = _key_tiles(i, bk, step, jnp.zeros((bq, HEAD_DIM), F32))
            dq_ref[...] = dq.astype(BF16)

        @pl.when(i == nq - 1)
        def _():
            dk_ref[...] = dkt_acc[...].T.astype(BF16)
            dv_ref[...] = dvt_acc[...].T.astype(BF16)

    W2 = 2 * H * HEAD_DIM
    est = 4 * L * HEAD_DIM * 2 + 4 * L * HEAD_DIM * 2 + 2 * L * HEAD_DIM * 4 + 48 * bq * bk * 4 + (4 << 20)
    dq, dk, dv, dg, dc = pl.pallas_call(
        body, name=name, grid=(2 * H, nq),
        in_specs=[q_spec, k_spec, v_spec, crow_spec, ccol_spec, g_spec, tile, stat, tile],
        out_specs=[tile, full, full, dg_spec, dc_spec],
        out_shape=[jax.ShapeDtypeStruct((L, W2), BF16)] * 3
        + [jax.ShapeDtypeStruct((2 * H, 1, HEAD_DIM), F32), jax.ShapeDtypeStruct((2 * H, 1, L), F32)],
        scratch_shapes=[pltpu.VMEM((HEAD_DIM, L), F32), pltpu.VMEM((HEAD_DIM, L), F32)],
        compiler_params=_params(("parallel", "arbitrary"), est),
    )(qkv, qkv, qkv, crow, ccol, g_heads, o, lse, d_on)
    return dq, dk, dv, dg[:, 0, :], dc


HALO = 16


def _conv_tiles(L, F):
    return _row_tile(L), _divisor(F, (512, 256, 128))


def _shift_down(x, halo_last2, first, bt):
    row = lax.broadcasted_iota(jnp.int32, (bt, 1), 0)
    h1 = jnp.where(first, 0.0, halo_last2[1:2, :])
    h2 = jnp.where(first, 0.0, halo_last2[0:1, :])
    x1 = jnp.where(row == 0, h1, pltpu.roll(x, 1, 0))
    x2 = jnp.where(row == 0, h2, jnp.where(row == 1, h1, pltpu.roll(x, 2, 0)))
    return x1, x2


def _conv_rows(p_ref, halo_ref, w_ref, b_ref, first, bt):
    p = p_ref[...].astype(F32)
    hl = halo_ref[...].astype(F32)
    p1, p2 = _shift_down(p, hl[HALO - 2:HALO, :], first, bt)
    w = w_ref[...]
    a = w[0:1, :] * p2 + w[1:2, :] * p1 + w[2:3, :] * p + b_ref[...]
    return a, p, p1, p2


def _sigmoid(x):
    return 1.0 / (1.0 + jnp.exp(-x))


def _conv_in_specs(L, F, bt, bc, order):
    nf = F // bc
    r = bt // HALO
    ix = (lambda a, b: (a, b)) if order == "ij" else (lambda a, b: (b, a))

    def mk(shape, fn):
        return pl.BlockSpec(shape, lambda a, b: fn(*ix(a, b)))

    return [
        mk((bt, bc), lambda i, j: (i, j)), mk((HALO, bc), lambda i, j: (jnp.maximum(i * r - 1, 0), j)),
        mk((bt, bc), lambda i, j: (i, nf + j)), mk((HALO, bc), lambda i, j: (jnp.maximum(i * r - 1, 0), nf + j)),
        mk((3, bc), lambda i, j: (0, j)), mk((3, bc), lambda i, j: (0, nf + j)),
        mk((1, bc), lambda i, j: (0, j)), mk((1, bc), lambda i, j: (0, nf + j)),
    ]


def _convgate_fwd(p, conv_w, conv_b, name):
    L, F2 = p.shape
    F = F2 // 2
    bt, bc = _conv_tiles(L, F)

    def body(pg, hg, pu, hu, wg, wu, bg, bu, act_ref):
        first = pl.program_id(0) == 0
        ag = _conv_rows(pg, hg, wg, bg, first, bt)[0]
        au = _conv_rows(pu, hu, wu, bu, first, bt)[0]
        act_ref[...] = (ag * _sigmoid(ag) * au).astype(BF16)

    return pl.pallas_call(
        body, name=name, grid=(L // bt, F // bc), in_specs=_conv_in_specs(L, F, bt, bc, "ij"),
        out_specs=pl.BlockSpec((bt, bc), lambda i, j: (i, j)), out_shape=jax.ShapeDtypeStruct((L, F), BF16),
        compiler_params=_params(("parallel", "parallel"), 24 * bt * bc * 4),
    )(p, p, p, p, conv_w, conv_w, conv_b.reshape(1, F2), conv_b.reshape(1, F2))


def _convgate_bwd(p, conv_w, conv_b, d_act, name):
    L, F2 = p.shape
    F = F2 // 2
    bt, bc = _conv_tiles(L, F)

    def body(pg, hg, pu, hu, wg, wu, bg, bu, dact_ref, da_ref, dwb_ref):
        first = pl.program_id(1) == 0
        ag, xg, xg1, xg2 = _conv_rows(pg, hg, wg, bg, first, bt)
        au, xu, xu1, xu2 = _conv_rows(pu, hu, wu, bu, first, bt)
        d_act = dact_ref[...].astype(F32)
        sg = _sigmoid(ag)
        dag = d_act * au * sg * (1.0 + ag * (1.0 - sg))
        dau = d_act * ag * sg
        da_ref[0] = dag.astype(BF16)
        da_ref[1] = dau.astype(BF16)

        @pl.when(first)
        def _():
            dwb_ref[...] = jnp.zeros_like(dwb_ref)
        cs = lambda v: jnp.sum(v, axis=0, keepdims=True)
        dwb_ref[0] += jnp.concatenate([cs(dag * xg2), cs(dag * xg1), cs(dag * xg), cs(dag)], axis=0)
        dwb_ref[1] += jnp.concatenate([cs(dau * xu2), cs(dau * xu1), cs(dau * xu), cs(dau)], axis=0)

    da, dwb = pl.pallas_call(
        body, name=name, grid=(F // bc, L // bt),
        in_specs=_conv_in_specs(L, F, bt, bc, "ji") + [pl.BlockSpec((bt, bc), lambda j, i: (i, j))],
        out_specs=[pl.BlockSpec((2, bt, bc), lambda j, i: (0, i, j)), pl.BlockSpec((2, 4, bc), lambda j, i: (0, 0, j))],
        out_shape=[jax.ShapeDtypeStruct((2, L, F), BF16), jax.ShapeDtypeStruct((2, 4, F), F32)],
        compiler_params=_params(("parallel", "arbitrary"), 40 * bt * bc * 4),
    )(p, p, p, p, conv_w, conv_w, conv_b.reshape(1, F2), conv_b.reshape(1, F2), d_act)
    d_w = jnp.concatenate([dwb[0, 0:3], dwb[1, 0:3]], axis=1)
    d_b = jnp.concatenate([dwb[0, 3], dwb[1, 3]], axis=0)
    return da, d_w, d_b


def _conv_bwd_data(da, conv_w, name):
    _, L, F = da.shape
    bt, bc = _conv_tiles(L, F)
    nf, r, nt = F // bc, bt // HALO, L // bt

    def body(da_ref, nxt_ref, w_ref, dp_ref):
        last = pl.program_id(0) == nt - 1
        x = da_ref[...].astype(F32)
        nx = nxt_ref[...].astype(F32)
        n0 = jnp.where(last, 0.0, nx[0:1, :])
        n1 = jnp.where(last, 0.0, nx[1:2, :])
        row = lax.broadcasted_iota(jnp.int32, (bt, 1), 0)
        x1 = jnp.where(row == bt - 1, n0, pltpu.roll(x, bt - 1, 0))
        x2 = jnp.where(row == bt - 1, n1, jnp.where(row == bt - 2, n0, pltpu.roll(x, bt - 2, 0)))
        w = w_ref[...]
        dp_ref[...] = (w[2:3, :] * x + w[1:2, :] * x1 + w[0:1, :] * x2).astype(BF16)

    return pl.pallas_call(
        body, name=name, grid=(nt, 2 * nf),
        in_specs=[pl.BlockSpec((None, bt, bc), lambda i, j: (j // nf, i, j % nf)),
                  pl.BlockSpec((None, HALO, bc), lambda i, j: (j // nf, jnp.minimum((i + 1) * r, nt * r - 1), j % nf)),
                  pl.BlockSpec((3, bc), lambda i, j: (0, j))],
        out_specs=pl.BlockSpec((bt, bc), lambda i, j: (i, j)), out_shape=jax.ShapeDtypeStruct((L, 2 * F), BF16),
        compiler_params=_params(("parallel", "parallel"), 16 * bt * bc * 4),
    )(da, da, conv_w)


def _pad_cols(a, n):
    return jnp.pad(a, ((0, 0), (0, n - a.shape[1])))


def _local_step(x, tgt, meta, W):
    S, D = x.shape
    n_meta = meta.shape[0]
    depth = len(W["w_in"])
    H = D // (2 * HEAD_DIM)
    WQ = 6 * H * HEAD_DIM
    L = -(-(S + n_meta) // ATT_BLOCK) * ATT_BLOCK
    tail = L - S - n_meta
    zeros_tail = jnp.zeros((tail, D), F32)
    h = jnp.concatenate([meta, x, zeros_tail], axis=0)
    tgt_p = jnp.concatenate([jnp.zeros((n_meta, D), F32), tgt, zeros_tail], axis=0)

    saved = []
    _, u1 = _resnorm_fwd(h, None, None, W["g_mix_pre"][0], "prenorm0")
    for l in range(depth):
        w_in = W["w_in"][l]
        w_qkv, w_f = w_in[:, :WQ], _pad_cols(w_in[:, WQ:], LANES)
        b_pad = jnp.pad(W["b_f"][l], (0, LANES - H)).reshape(1, LANES)
        g_heads = jnp.concatenate([W["g_sb"][l], W["g_fox"][l]], axis=0).reshape(2 * H, 1, HEAD_DIM)
        qkv = _mm(u1, w_qkv, "nn", BF16, f"qkv{l}")
        fl = _mm(u1, w_f, "nn", F32, f"flogit{l}")
        c = _gate_fwd(fl, b_pad, f"gate_fwd{l}")
        c_heads = c[:, :H].T
        crow, ccol = c_heads[:, None, :], c_heads[:, :, None]
        o, on, lse = _attn_fwd(qkv, crow, ccol, g_heads, H, f"attn_fwd{l}")
        mix = _mm(on, W["w_out"][l], "nn", F32, f"mix{l}")
        h_mid, u2 = _resnorm_fwd(h, mix, W["g_mix_post"][l], W["g_ffn_pre"][l], f"resnorm_a{l}")
        p = _mm(u2, W["w_up"][l], "nn", BF16, f"up{l}")
        act = _convgate_fwd(p, W["conv_w"][l], W["conv_b"][l], f"convgate{l}")
        ff = _mm(act, W["w_down"][l], "nn", F32, f"down{l}")
        g_next = W["g_mix_pre"][l + 1] if l + 1 < depth else None
        h_out, u1_next = _resnorm_fwd(h_mid, ff, W["g_ffn_post"][l], g_next, f"resnorm_b{l}")
        saved.append(dict(h_in=h, u1=u1, w_qkv=w_qkv, w_f=w_f, b_pad=b_pad, g_heads=g_heads, qkv=qkv, fl=fl, crow=crow,
                          ccol=ccol, o=o, on=on, lse=lse, mix=mix, h_mid=h_mid, u2=u2, p=p, act=act, ff=ff, h_out=h_out))
        h, u1 = h_out, u1_next

    loss, dh = _loss(h, tgt_p, n_meta, S, "loss")

    grads = {k: [None] * depth for k in ("g_mix_pre", "w_in", "b_f", "g_sb", "g_fox", "w_out", "g_mix_post", "g_ffn_pre",
                                         "w_up", "conv_w", "conv_b", "w_down", "g_ffn_post")}
    du1_next = None
    for l in reversed(range(depth)):
        s = saved[l]
        g_next = W["g_mix_pre"][l + 1] if l + 1 < depth else None
        dh, d_ff, grads["g_ffn_post"][l], dg_pre_next = _resnorm_bwd(
            dh, du1_next, s["h_out"], s["ff"], W["g_ffn_post"][l], g_next, f"resnorm_b_bwd{l}")
        if l + 1 < depth:
            grads["g_mix_pre"][l + 1] = dg_pre_next
        d_act = _mm(d_ff, W["w_down"][l], "nt", BF16, f"d_act{l}")
        grads["w_down"][l] = _mm(s["act"], d_ff, "tn", F32, f"dw_down{l}")
        da, grads["conv_w"][l], grads["conv_b"][l] = _convgate_bwd(s["p"], W["conv_w"][l], W["conv_b"][l], d_act, f"convgate_bwd{l}")
        dp = _conv_bwd_data(da, W["conv_w"][l], f"conv_bwd{l}")
        du2 = _mm(dp, W["w_up"][l], "nt", F32, f"d_u2{l}")
        grads["w_up"][l] = _mm(s["u2"], dp, "tn", F32, f"dw_up{l}")
        dh, d_mix, grads["g_mix_post"][l], grads["g_ffn_pre"][l] = _resnorm_bwd(
            dh, du2, s["h_mid"], s["mix"], W["g_mix_post"][l], W["g_ffn_pre"][l], f"resnorm_a_bwd{l}")
        d_on = _mm(d_mix, W["w_out"][l], "nt", F32, f"d_on{l}")
        grads["w_out"][l] = _mm(s["on"], d_mix, "tn", F32, f"dw_out{l}")
        dq, dk, dv, dg_heads, dcrow = _attn_bwd(s["qkv"], s["crow"], s["ccol"], s["g_heads"], s["o"], s["lse"], d_on, H, f"attn_bwd{l}")
        grads["g_sb"][l], grads["g_fox"][l] = dg_heads[:H], dg_heads[H:]
        dc = _pad_cols(dcrow[H:, 0, :].T, LANES)
        dfl, db = _gate_bwd(dc, s["fl"], s["b_pad"], f"gate_bwd{l}")
        grads["b_f"][l] = db[:H]
        Wh = H * HEAD_DIM
        d_qkv = jnp.concatenate([dq[:, :Wh], dk[:, :Wh], dv[:, :Wh], dq[:, Wh:], dk[:, Wh:], dv[:, Wh:]], axis=1)
        du1_next = _mm(d_qkv, s["w_qkv"], "nt", F32, f"d_u1{l}", a2=dfl, b2=s["w_f"])
        dw_qkv = _mm(s["u1"], d_qkv, "tn", F32, f"dw_qkv{l}")
        dw_f = _mm(s["u1"], dfl, "tn", F32, f"dw_f{l}")
        grads["w_in"][l] = jnp.concatenate([dw_qkv, dw_f[:, :H]], axis=1)
    dh0, _, _, grads["g_mix_pre"][0] = _resnorm_bwd(dh, du1_next, saved[0]["h_in"], None, None, W["g_mix_pre"][0], "prenorm0_bwd")
    return loss, dh0[n_meta:n_meta + S], dh0[:n_meta], grads


_MESH = pl.DeviceIdType.MESH
_ANY = pl.BlockSpec(memory_space=pl.ANY)


def _flip(v, bit):
    return 1 - v if bit else v


def _all_gather(x, name):
    def body(x_ref, out_ref, send_sems, recv_sems, local_sem):
        mx, my, mc = lax.axis_index("x"), lax.axis_index("y"), lax.axis_index("c")
        me, sibling = (mx, my, mc), (mx, my, 1 - mc)
        chips = [(1 - mx, my), (mx, 1 - my), (1 - mx, 1 - my)]

        def slot(px, py, pc):
            return out_ref.at[4 * px + 2 * py + pc]

        def copy(k, block, to, src=None):
            return pltpu.make_async_remote_copy(
                src_ref=slot(*block) if src is None else src, dst_ref=slot(*block),
                send_sem=send_sems.at[k], recv_sem=recv_sems.at[k], device_id=to, device_id_type=_MESH)

        mine = pltpu.make_async_copy(x_ref, slot(*me), local_sem)
        mine.start()
        first = [copy(0, me, sibling, src=x_ref)]
        first += [copy(1 + j, me, (*chip, mc), src=x_ref) for j, chip in enumerate(chips)]
        for cp in first:
            cp.start()
        passed = [copy(4 + j, (*chip, mc), sibling) for j, chip in enumerate(chips)]
        for j, chip in enumerate(chips):
            copy(1 + j, (*chip, mc), me).wait_recv()
            passed[j].start()
        copy(0, sibling, me).wait_recv()
        for j, chip in enumerate(chips):
            copy(4 + j, (*chip, 1 - mc), me).wait_recv()
        for cp in first + passed:
            cp.wait_send()
        mine.wait()

    return pl.pallas_call(
        body, name=name, out_shape=jax.ShapeDtypeStruct((N_DEV, *x.shape), x.dtype),
        in_specs=[_ANY], out_specs=_ANY,
        scratch_shapes=[pltpu.SemaphoreType.DMA((7,)), pltpu.SemaphoreType.DMA((7,)), pltpu.SemaphoreType.DMA],
    )(x)


def _exchange(g, name):
    def body(g_ref, land_ref, send_sems, recv_sems, local_sem):
        mx, my, mc = lax.axis_index("x"), lax.axis_index("y"), lax.axis_index("c")
        me = 4 * mx + 2 * my + mc
        local = pltpu.make_async_copy(g_ref.at[me], land_ref.at[me], local_sem)
        local.start()
        sends, recvs = [], []
        for k in range(1, N_DEV):
            peer = (_flip(mx, k & 4), _flip(my, k & 2), _flip(mc, k & 1))
            p = 4 * peer[0] + 2 * peer[1] + peer[2]
            sends.append(pltpu.make_async_remote_copy(
                src_ref=g_ref.at[p], dst_ref=land_ref.at[me], send_sem=send_sems.at[k - 1], recv_sem=recv_sems.at[k - 1],
                device_id=peer, device_id_type=_MESH))
            recvs.append(pltpu.make_async_remote_copy(
                src_ref=g_ref.at[p], dst_ref=land_ref.at[p], send_sem=send_sems.at[k - 1], recv_sem=recv_sems.at[k - 1],
                device_id=peer, device_id_type=_MESH))
        for cp in sends:
            cp.start()
        for cp in recvs:
            cp.wait_recv()
        for cp in sends:
            cp.wait_send()
        local.wait()

    return pl.pallas_call(
        body, name=name, out_shape=jax.ShapeDtypeStruct(g.shape, g.dtype), in_specs=[_ANY], out_specs=_ANY,
        scratch_shapes=[pltpu.SemaphoreType.DMA((7,)), pltpu.SemaphoreType.DMA((7,)), pltpu.SemaphoreType.DMA],
    )(g)


def _exchange_cores(g, name):
    def body(g_ref, land_ref, send_sems, recv_sems):
        mx, my, mc = lax.axis_index("x"), lax.axis_index("y"), lax.axis_index("c")
        copies = [pltpu.make_async_remote_copy(
            src_ref=g_ref.at[2 * q + (1 - mc)], dst_ref=land_ref.at[q], send_sem=send_sems.at[q], recv_sem=recv_sems.at[q],
            device_id=(mx, my, 1 - mc), device_id_type=_MESH) for q in range(4)]
        for cp in copies:
            cp.start()
        for cp in copies:
            cp.wait_recv()
        for cp in copies:
            cp.wait_send()

    return pl.pallas_call(
        body, name=name, out_shape=jax.ShapeDtypeStruct((4, *g.shape[1:]), g.dtype), in_specs=[_ANY], out_specs=_ANY,
        scratch_shapes=[pltpu.SemaphoreType.DMA((4,)), pltpu.SemaphoreType.DMA((4,))],
    )(g)


def _pair_sum(g, land, name):
    _, n, R, C = g.shape
    tr = _divisor(R, (128, 64, 32, 16, 8))
    core = lax.axis_index("c").astype(jnp.int32).reshape(1)

    def body(c_ref, g_ref, land_ref, o_ref):
        o_ref[...] = (g_ref[...].astype(F32) + land_ref[...].astype(F32)).astype(o_ref.dtype)

    blk = pl.BlockSpec((None, None, tr, C), lambda q, l, r, c_ref: (q, l, r, 0))
    return pl.pallas_call(
        body, name=name, out_shape=jax.ShapeDtypeStruct(land.shape, g.dtype),
        grid_spec=pltpu.PrefetchScalarGridSpec(
            num_scalar_prefetch=1, grid=(4, n, R // tr),
            in_specs=[pl.BlockSpec((None, None, tr, C), lambda q, l, r, c_ref: (2 * q + c_ref[0], l, r, 0)), blk],
            out_specs=blk),
        compiler_params=_params(("parallel", "parallel", "parallel"), 8 * tr * C * 4),
    )(core, g, land)


def _exchange_chips(part, name):
    def body(p_ref, land_ref, send_sems, recv_sems, local_sem):
        mx, my, mc = lax.axis_index("x"), lax.axis_index("y"), lax.axis_index("c")
        me = 2 * mx + my
        local = pltpu.make_async_copy(p_ref.at[me], land_ref.at[me], local_sem)
        local.start()
        sends, recvs = [], []
        for k in range(1, 4):
            px, py = _flip(mx, k & 2), _flip(my, k & 1)
            p = 2 * px + py
            sends.append(pltpu.make_async_remote_copy(
                src_ref=p_ref.at[p], dst_ref=land_ref.at[me], send_sem=send_sems.at[k - 1], recv_sem=recv_sems.at[k - 1],
                device_id=(px, py, mc), device_id_type=_MESH))
            recvs.append(pltpu.make_async_remote_copy(
                src_ref=p_ref.at[p], dst_ref=land_ref.at[p], send_sem=send_sems.at[k - 1], recv_sem=recv_sems.at[k - 1],
                device_id=(px, py, mc), device_id_type=_MESH))
        for cp in sends:
            cp.start()
        for cp in recvs:
            cp.wait_recv()
        for cp in sends:
            cp.wait_send()
        local.wait()

    return pl.pallas_call(
        body, name=name, out_shape=jax.ShapeDtypeStruct(part.shape, part.dtype), in_specs=[_ANY], out_specs=_ANY,
        scratch_shapes=[pltpu.SemaphoreType.DMA((3,)), pltpu.SemaphoreType.DMA((3,)), pltpu.SemaphoreType.DMA],
    )(part)


def _adamw_landed(land, w, m, v, name):
    S, n, R, C = land.shape
    tr = _divisor(R, (64, 32, 16, 8))
    blk = pl.BlockSpec((None, tr, C), lambda l, r: (l, r, 0))

    def body(land_ref, w_ref, m_ref, v_ref, g_ref, d_ref, mo_ref, vo_ref):
        g = land_ref[0].astype(F32)
        for s in range(1, S):
            g = g + land_ref[s].astype(F32)
        m_new = ADAM_B1 * m_ref[...] + (1.0 - ADAM_B1) * g
        v_new = ADAM_B2 * v_ref[...] + (1.0 - ADAM_B2) * (g * g)
        m_hat = m_new / (1.0 - ADAM_B1 ** ADAM_STEP)
        v_hat = v_new / (1.0 - ADAM_B2 ** ADAM_STEP)
        g_ref[...] = g
        d_ref[...] = -ADAM_LR * (m_hat / (jnp.sqrt(v_hat) + ADAM_EPS) + ADAM_WD * w_ref[...])
        mo_ref[...] = m_new
        vo_ref[...] = v_new

    est = 2 * tr * C * (S * jnp.dtype(land.dtype).itemsize + 7 * 4) * 9 // 8
    return pl.pallas_call(
        body, name=name, grid=(n, R // tr),
        in_specs=[pl.BlockSpec((S, None, tr, C), lambda l, r: (0, l, r, 0)), blk, blk, blk],
        out_specs=[blk] * 4, out_shape=[jax.ShapeDtypeStruct((n, R, C), F32)] * 4,
        compiler_params=_params(("parallel", "parallel"), est),
    )(land, w, m, v)


def _pack(arrs):
    flat = jnp.concatenate([a.reshape(-1).astype(F32) for a in arrs])
    rows = -(-flat.shape[0] // (8 * LANES)) * 8
    return jnp.pad(flat, (0, rows * LANES - flat.shape[0])).reshape(rows, LANES)


def _unpack(packed, shapes):
    flat, out, at = packed.reshape(-1), [], 0
    for s in shapes:
        n = math.prod(s)
        out.append(flat[at:at + n].reshape(s))
        at += n
    return out


_BIG = ("w_in", "w_out", "w_up", "w_down")
_REPLICATED = ("g_mix_pre", "b_f", "g_sb", "g_fox", "g_mix_post", "g_ffn_pre", "conv_b", "g_ffn_post")
_ORDER = ("meta", "g_mix_pre", "w_in", "b_f", "g_sb", "g_fox", "w_out", "g_mix_post", "g_ffn_pre", "w_up", "conv_w",
          "conv_b", "w_down", "g_ffn_post")
_COLUMN_SHARDED = ("w_in", "w_up")


def kernel(x, meta, g_mix_pre, w_in, b_f, g_sb, g_fox, w_out, g_mix_post, g_ffn_pre, w_up, conv_w, conv_b, w_down, g_ffn_post, loss_target, m_meta, m_g_mix_pre, m_w_in, m_b_f, m_g_sb, m_g_fox, m_w_out, m_g_mix_post, m_g_ffn_pre, m_w_up, m_conv_w, m_conv_b, m_w_down, m_g_ffn_post, v_meta, v_g_mix_pre, v_w_in, v_b_f, v_g_sb, v_g_fox, v_w_out, v_g_mix_post, v_g_ffn_pre, v_w_up, v_conv_w, v_conv_b, v_w_down, v_g_ffn_post):
    w = dict(meta=meta, g_mix_pre=g_mix_pre, w_in=w_in, b_f=b_f, g_sb=g_sb, g_fox=g_fox, w_out=w_out, g_mix_post=g_mix_post,
             g_ffn_pre=g_ffn_pre, w_up=w_up, conv_w=conv_w, conv_b=conv_b, w_down=w_down, g_ffn_post=g_ffn_post)
    m = dict(meta=m_meta, g_mix_pre=m_g_mix_pre, w_in=m_w_in, b_f=m_b_f, g_sb=m_g_sb, g_fox=m_g_fox, w_out=m_w_out,
             g_mix_post=m_g_mix_post, g_ffn_pre=m_g_ffn_pre, w_up=m_w_up, conv_w=m_conv_w, conv_b=m_conv_b, w_down=m_w_down,
             g_ffn_post=m_g_ffn_post)
    v = dict(meta=v_meta, g_mix_pre=v_g_mix_pre, w_in=v_w_in, b_f=v_b_f, g_sb=v_g_sb, g_fox=v_g_fox, w_out=v_w_out,
             g_mix_post=v_g_mix_post, g_ffn_pre=v_g_ffn_pre, w_up=v_w_up, conv_w=v_conv_w, conv_b=v_conv_b, w_down=v_w_down,
             g_ffn_post=v_g_ffn_post)
    depth = w_in.shape[0]
    me = 4 * lax.axis_index("x") + 2 * lax.axis_index("y") + lax.axis_index("c")

    full = {}
    for name in _BIG:
        g = _all_gather(w[name].astype(BF16), f"gather_{name}")
        if name in _COLUMN_SHARDED:
            full[name] = [jnp.transpose(g[:, l], (1, 0, 2)).reshape(g.shape[2], -1) for l in range(depth)]
        else:
            full[name] = [g[:, l].reshape(-1, g.shape[3]) for l in range(depth)]
    small_shapes = [conv_w.shape, meta.shape]
    gs = _all_gather(_pack([conv_w, meta]), "gather_small")
    parts = [_unpack(gs[d], small_shapes) for d in range(N_DEV)]
    conv_full = jnp.concatenate([p[0] for p in parts], axis=2)
    meta_full = jnp.concatenate([p[1] for p in parts], axis=1)

    W = dict(full)
    W["conv_w"] = [conv_full[l] for l in range(depth)]
    for name in _REPLICATED:
        W[name] = [w[name][l] for l in range(depth)]
    loss, grad_x, d_meta, grads = _local_step(x[0], loss_target[0], meta_full, W)

    out = {}
    for name in _BIG:
        R, C = w[name].shape[1:]
        if name in _COLUMN_SHARDED:
            blocks = jnp.stack([jnp.transpose(grads[name][l].reshape(R, N_DEV, C), (1, 0, 2)) for l in range(depth)], axis=1)
        else:
            blocks = jnp.stack([grads[name][l].reshape(N_DEV, R, C) for l in range(depth)], axis=1)
        blocks = blocks.astype(BF16)
        pairs = _pair_sum(blocks, _exchange_cores(blocks, f"exchange_cores_{name}"), f"pair_sum_{name}")
        land = _exchange_chips(pairs, f"exchange_chips_{name}")
        out[name] = _adamw_landed(land, w[name], m[name], v[name], f"adamw_{name}")

    Fs, Ms = conv_w.shape[2], meta.shape[1]
    d_conv = jnp.stack(grads["conv_w"], axis=0)
    blocks = jnp.stack([_pack([d_conv[:, :, d * Fs:(d + 1) * Fs], d_meta[:, d * Ms:(d + 1) * Ms]]) for d in range(N_DEV)])
    land = _exchange(blocks, "exchange_small")
    res = _adamw_landed(land[:, None], _pack([conv_w, meta])[None], _pack([m_conv_w, m_meta])[None],
                        _pack([v_conv_w, v_meta])[None], "adamw_small")
    for i, r in enumerate(res):
        cw, mt = _unpack(r[0], small_shapes)
        out.setdefault("conv_w", [None] * 4)[i] = cw
        out.setdefault("meta", [None] * 4)[i] = mt

    rep_shapes = [()] + [w[name].shape for name in _REPLICATED]
    mine = _pack([loss] + [jnp.stack(grads[name], axis=0) for name in _REPLICATED])
    land = _all_gather(mine, "gather_replicated")
    zero = jnp.zeros((), F32)
    res = _adamw_landed(land[:, None], _pack([zero] + [w[n] for n in _REPLICATED])[None],
                        _pack([zero] + [m[n] for n in _REPLICATED])[None],
                        _pack([zero + 1.0] + [v[n] for n in _REPLICATED])[None], "adamw_replicated")
    for i, r in enumerate(res):
        vals = _unpack(r[0], rep_shapes)
        if i == 0:
            loss_total = vals[0]
        for name, val in zip(_REPLICATED, vals[1:]):
            out.setdefault(name, [None] * 4)[i] = val

    return (loss_total, grad_x[None], *[out[n][0] for n in _ORDER], *[out[n][1] for n in _ORDER],
            *[out[n][2] for n in _ORDER], *[out[n][3] for n in _ORDER])
```

```python
import functools
import math

import jax
import jax.numpy as jnp
from jax import lax
from jax.experimental import pallas as pl
from jax.experimental.pallas import tpu as pltpu

F32, BF16 = jnp.float32, jnp.bfloat16
HEAD_DIM = 128
LANES = 128
EPS = 1e-6
NEG_INF = -1e30
ATT_BLOCK = 256
N_DEV = 8
V7X_VMEM_BUDGET = 56 * 1024 * 1024

ADAM_LR, ADAM_B1, ADAM_B2, ADAM_EPS, ADAM_WD, ADAM_STEP = 0.001, 0.9, 0.999, 1e-08, 0.01, 10


def _divisor(n, cands):
    for c in cands:
        if c <= n and n % c == 0:
            return c
    raise ValueError(f"no tile for {n} among {cands}")


def _params(sem, est_bytes):
    limit = int(min(V7X_VMEM_BUDGET, max(16 * 1024 * 1024, est_bytes * 5 // 4 + (2 << 20))))
    return pltpu.CompilerParams(dimension_semantics=sem, vmem_limit_bytes=limit)


def _nbytes(shape, dtype):
    return math.prod(shape) * jnp.dtype(dtype).itemsize


_DN = {"nn": (((1,), (0,)), ((), ())), "nt": (((1,), (1,)), ((), ())), "tn": (((0,), (0,)), ((), ()))}
_ROW_TILES = (1088, 544, 272, 512, 256, 128, 64, 32, 16, 8)
_COL_TILES = (512, 256, 128)


def _ktile(k, cap):
    if k <= cap:
        return k
    for t in range(cap - cap % LANES, 0, -LANES):
        if k % t == 0:
            return t
    raise ValueError(k)


def _mm(a, b, mode, out_dtype, name, a2=None, b2=None):
    if mode == "nn":
        (M, K), (_, N) = a.shape, b.shape
    elif mode == "nt":
        (M, K), (N, _) = a.shape, b.shape
    else:
        (K, M), (_, N) = a.shape, b.shape
    if mode == "tn":
        tm, tn, tk = _divisor(M, _COL_TILES), _divisor(N, _COL_TILES), _ktile(K, 4608)
    else:
        tm, tn, tk = _divisor(M, _ROW_TILES), _divisor(N, _COL_TILES), _ktile(K, 3072)
    nk = K // tk
    a_spec = {"nn": pl.BlockSpec((tm, tk), lambda i, j, k: (i, k)),
              "nt": pl.BlockSpec((tm, tk), lambda i, j, k: (i, k)),
              "tn": pl.BlockSpec((tk, tm), lambda i, j, k: (k, i))}[mode]
    b_spec = {"nn": pl.BlockSpec((tk, tn), lambda i, j, k: (k, j)),
              "nt": pl.BlockSpec((tn, tk), lambda i, j, k: (j, k)),
              "tn": pl.BlockSpec((tk, tn), lambda i, j, k: (k, j))}[mode]
    dn = _DN[mode]
    extra = a2 is not None
    in_specs, args = [a_spec, b_spec], [a, b]
    if extra:
        k2 = a2.shape[1]
        in_specs += [pl.BlockSpec((tm, k2), lambda i, j, k: (i, 0)), pl.BlockSpec((tn, k2), lambda i, j, k: (j, 0))]
        args += [a2, b2]

    def body(*refs):
        if extra:
            a_ref, b_ref, a2_ref, b2_ref, o_ref, acc = refs
        else:
            a_ref, b_ref, o_ref, acc = refs
        part = lax.dot_general(a_ref[...], b_ref[...], dn, preferred_element_type=F32)
        if nk == 1:
            if extra:
                part = part + lax.dot_general(a2_ref[...], b2_ref[...], _DN["nt"], preferred_element_type=F32)
            o_ref[...] = part.astype(o_ref.dtype)
            return
        kk = pl.program_id(2)

        @pl.when(kk == 0)
        def _():
            if extra:
                acc[...] = part + lax.dot_general(a2_ref[...], b2_ref[...], _DN["nt"], preferred_element_type=F32)
            else:
                acc[...] = part

        @pl.when(kk > 0)
        def _():
            acc[...] += part

        @pl.when(kk == nk - 1)
        def _():
            o_ref[...] = acc[...].astype(o_ref.dtype)

    est = 2 * (tm * tk + tk * tn) * 2 + 2 * _nbytes((tm, tn), out_dtype) + tm * tn * 4
    return pl.pallas_call(
        body, name=name, grid=(M // tm, N // tn, nk), in_specs=in_specs,
        out_specs=pl.BlockSpec((tm, tn), lambda i, j, k: (i, j)),
        out_shape=jax.ShapeDtypeStruct((M, N), out_dtype),
        scratch_shapes=[pltpu.VMEM((tm, tn), F32)],
        compiler_params=_params(("parallel", "parallel", "arbitrary"), est),
    )(*args)


def _rms(x, g):
    r = lax.rsqrt(jnp.mean(x * x, axis=-1, keepdims=True) + EPS)
    return x * r * g


def _rms_bwd(x, g, dout):
    r = lax.rsqrt(jnp.mean(x * x, axis=-1, keepdims=True) + EPS)
    xhat = x * r
    dxh = dout * g
    dx = r * (dxh - xhat * jnp.mean(dxh * xhat, axis=-1, keepdims=True))
    return dx, dout * xhat


def _row_tile(L):
    return _divisor(L, (272, 256, 128, 64, 32, 16))


def _resnorm_fwd(h, y, g_post, g_pre, name):
    L, D = h.shape
    bt = _row_tile(L)
    has_y, has_pre = y is not None, g_pre is not None
    row = pl.BlockSpec((bt, D), lambda i: (i, 0))
    vec = pl.BlockSpec((1, D), lambda i: (0, 0))
    args, in_specs = [h], [row]
    if has_y:
        args += [y, g_post.reshape(1, D)]
        in_specs += [row, vec]
    if has_pre:
        args += [g_pre.reshape(1, D)]
        in_specs += [vec]
    out_shape, out_specs = [], []
    if has_y:
        out_shape.append(jax.ShapeDtypeStruct((L, D), F32))
        out_specs.append(row)
    if has_pre:
        out_shape.append(jax.ShapeDtypeStruct((L, D), BF16))
        out_specs.append(row)

    def body(*refs):
        refs = list(refs)
        h_ref = refs.pop(0)
        hn = h_ref[...]
        if has_y:
            y_ref, gp_ref = refs.pop(0), refs.pop(0)
            hn = hn + _rms(y_ref[...], gp_ref[...])
        if has_pre:
            g_ref = refs.pop(0)
        if has_y:
            refs.pop(0)[...] = hn
        if has_pre:
            refs.pop(0)[...] = _rms(hn, g_ref[...]).astype(BF16)

    outs = pl.pallas_call(
        body, name=name, grid=(L // bt,), in_specs=in_specs, out_specs=out_specs, out_shape=out_shape,
        compiler_params=_params(("parallel",), 10 * bt * D * 4),
    )(*args)
    outs = list(outs)
    h_new = outs.pop(0) if has_y else h
    u = outs.pop(0) if has_pre else None
    return h_new, u


def _resnorm_bwd(dh_direct, du, h_new, y, g_post, g_pre, name):
    L, D = dh_direct.shape
    bt = _row_tile(L)
    has_y, has_pre = y is not None, du is not None
    row = pl.BlockSpec((bt, D), lambda i: (i, 0))
    vec = pl.BlockSpec((1, D), lambda i: (0, 0))
    acc = pl.BlockSpec((8, D), lambda i: (0, 0))
    args, in_specs = [dh_direct], [row]
    if has_pre:
        args += [du, h_new, g_pre.reshape(1, D)]
        in_specs += [row, row, vec]
    if has_y:
        args += [y, g_post.reshape(1, D)]
        in_specs += [row, vec]
    out_shape, out_specs = [], []
    if has_pre:
        out_shape += [jax.ShapeDtypeStruct((L, D), F32), jax.ShapeDtypeStruct((8, D), F32)]
        out_specs += [row, acc]
    if has_y:
        out_shape += [jax.ShapeDtypeStruct((L, D), BF16), jax.ShapeDtypeStruct((8, D), F32)]
        out_specs += [row, acc]

    def colsum8(v):
        return jnp.sum(v.reshape(bt // 8, 8, D), axis=0)

    def body(*refs):
        refs = list(refs)
        first = pl.program_id(0) == 0
        dh = refs.pop(0)[...]
        if has_pre:
            du_ref, hn_ref, g_ref = refs.pop(0), refs.pop(0), refs.pop(0)
        if has_y:
            y_ref, gp_ref = refs.pop(0), refs.pop(0)
        if has_pre:
            dh_ref, dgpre_ref = refs.pop(0), refs.pop(0)
            dx, dgp = _rms_bwd(hn_ref[...], g_ref[...], du_ref[...].astype(F32))
            dh = dh + dx
            dh_ref[...] = dh

            @pl.when(first)
            def _():
                dgpre_ref[...] = jnp.zeros_like(dgpre_ref)
            dgpre_ref[...] += colsum8(dgp)
        if has_y:
            dy_ref, dgpost_ref = refs.pop(0), refs.pop(0)
            dy, dgq = _rms_bwd(y_ref[...], gp_ref[...], dh)
            dy_ref[...] = dy.astype(BF16)

            @pl.when(first)
            def _():
                dgpost_ref[...] = jnp.zeros_like(dgpost_ref)
            dgpost_ref[...] += colsum8(dgq)

    outs = list(pl.pallas_call(
        body, name=name, grid=(L // bt,), in_specs=in_specs, out_specs=out_specs, out_shape=out_shape,
        compiler_params=_params(("arbitrary",), 14 * bt * D * 4),
    )(*args))
    dh, dg_pre, dy, dg_post = dh_direct, None, None, None
    if has_pre:
        dh, dg_pre = outs.pop(0), outs.pop(0).sum(0)
    if has_y:
        dy, dg_post = outs.pop(0), outs.pop(0).sum(0)
    return dh, dy, dg_post, dg_pre


def _loss(h, tgt, n_meta, seq, name):
    L, D = h.shape
    bt = _row_tile(L)
    row = pl.BlockSpec((bt, D), lambda i: (i, 0))

    def body(h_ref, t_ref, dy_ref, loss_ref):
        i = pl.program_id(0)
        r = i * bt + lax.broadcasted_iota(jnp.int32, (bt, 1), 0)
        valid = (r >= n_meta) & (r < n_meta + seq)
        e = jnp.where(valid, h_ref[...] - t_ref[...], 0.0)
        dy_ref[...] = e * (1.0 / D)

        @pl.when(i == 0)
        def _():
            loss_ref[...] = jnp.zeros_like(loss_ref)
        loss_ref[...] += 0.5 * jnp.sum(jnp.sum(e * e, axis=-1, keepdims=True) * (1.0 / D))

    dy, loss = pl.pallas_call(
        body, name=name, grid=(L // bt,), in_specs=[row, row],
        out_specs=[row, pl.BlockSpec((8, LANES), lambda i: (0, 0))],
        out_shape=[jax.ShapeDtypeStruct((L, D), F32), jax.ShapeDtypeStruct((8, LANES), F32)],
        compiler_params=_params(("arbitrary",), 8 * bt * D * 4),
    )(h, tgt)
    return loss[0, 0], dy


def _split3(x):
    x1 = x.astype(BF16)
    r1 = x - x1.astype(F32)
    x2 = r1.astype(BF16)
    x3 = (r1 - x2.astype(F32)).astype(BF16)
    return x1, x2, x3


def _tri_dot3(tri, x):
    x1, x2, x3 = _split3(x)
    d = functools.partial(jnp.dot, preferred_element_type=F32)
    return d(tri, x1) + d(tri, x2) + d(tri, x3)


def _gate_fwd(fl, b_pad, name):
    L = fl.shape[0]
    bt = 128
    blk = pl.BlockSpec((bt, LANES), lambda i: (i, 0))

    def body(fl_ref, b_ref, c_ref, carry):
        @pl.when(pl.program_id(0) == 0)
        def _():
            carry[...] = jnp.zeros_like(carry)
        x = fl_ref[...] + b_ref[...]
        lf = jnp.minimum(x, 0.0) - jnp.log(1.0 + jnp.exp(-jnp.abs(x)))
        r = lax.broadcasted_iota(jnp.int32, (bt, bt), 0)
        s = lax.broadcasted_iota(jnp.int32, (bt, bt), 1)
        tri = (s <= r).astype(BF16)
        c = _tri_dot3(tri, lf) + carry[...]
        c_ref[...] = c
        carry[...] = c[bt - 1:bt, :]

    return pl.pallas_call(
        body, name=name, grid=(L // bt,), in_specs=[blk, pl.BlockSpec((1, LANES), lambda i: (0, 0))],
        out_specs=blk, out_shape=jax.ShapeDtypeStruct((L, LANES), F32),
        scratch_shapes=[pltpu.VMEM((1, LANES), F32)],
        compiler_params=_params(("arbitrary",), 1 << 20),
    )(fl, b_pad)


def _gate_bwd(dc, fl, b_pad, name):
    L = fl.shape[0]
    bt = 128
    n = L // bt
    blk = pl.BlockSpec((bt, LANES), lambda i: (n - 1 - i, 0))

    def body(dc_ref, fl_ref, b_ref, dfl_ref, db_ref, carry):
        @pl.when(pl.program_id(0) == 0)
        def _():
            carry[...] = jnp.zeros_like(carry)
            db_ref[...] = jnp.zeros_like(db_ref)
        r = lax.broadcasted_iota(jnp.int32, (bt, bt), 0)
        s = lax.broadcasted_iota(jnp.int32, (bt, bt), 1)
        tri = (s >= r).astype(BF16)
        dlf = _tri_dot3(tri, dc_ref[...]) + carry[...]
        carry[...] = dlf[0:1, :]
        x = fl_ref[...] + b_ref[...]
        dfl = dlf / (1.0 + jnp.exp(x))
        dfl_ref[...] = dfl.astype(BF16)
        db_ref[...] += jnp.sum(dfl.reshape(bt // 8, 8, LANES), axis=0)

    dfl, db = pl.pallas_call(
        body, name=name, grid=(n,), in_specs=[blk, blk, pl.BlockSpec((1, LANES), lambda i: (0, 0))],
        out_specs=[blk, pl.BlockSpec((8, LANES), lambda i: (0, 0))],
        out_shape=[jax.ShapeDtypeStruct((L, LANES), BF16), jax.ShapeDtypeStruct((8, LANES), F32)],
        scratch_shapes=[pltpu.VMEM((1, LANES), F32)],
        compiler_params=_params(("arbitrary",), 1 << 20),
    )(dc, fl, b_pad)
    return dfl, db.sum(0)


_MESH = pl.DeviceIdType.MESH
_ANY = pl.BlockSpec(memory_space=pl.ANY)


def _flip(v, bit):
    return 1 - v if bit else v


class _GatherRider:
    def __init__(self, blocks):
        self.blocks = list(blocks)
        self.n = len(self.blocks)
        self.in_specs = [_ANY] * self.n
        self.out_specs = [_ANY] * self.n
        self.out_shape = [jax.ShapeDtypeStruct((N_DEV, *b.shape), b.dtype) for b in self.blocks]
        self.scratch = [pltpu.SemaphoreType.DMA((7 * self.n,)), pltpu.SemaphoreType.DMA((7 * self.n,)),
                        pltpu.SemaphoreType.DMA((self.n,))]

    def _copies(self, a, x_ref, out_ref, send_sems, recv_sems):
        mx, my, mc = lax.axis_index("x"), lax.axis_index("y"), lax.axis_index("c")
        me, sibling = (mx, my, mc), (mx, my, 1 - mc)
        chips = [(1 - mx, my), (mx, 1 - my), (1 - mx, 1 - my)]

        def slot(px, py, pc):
            return out_ref.at[4 * px + 2 * py + pc]

        def copy(k, block, to, src=None):
            return pltpu.make_async_remote_copy(
                src_ref=slot(*block) if src is None else src, dst_ref=slot(*block),
                send_sem=send_sems.at[7 * a + k], recv_sem=recv_sems.at[7 * a + k], device_id=to, device_id_type=_MESH)

        first = [copy(0, me, sibling, src=x_ref)] + [copy(1 + j, me, (*chip, mc), src=x_ref) for j, chip in enumerate(chips)]
        landed = [copy(1 + j, (*chip, mc), me) for j, chip in enumerate(chips)]
        passed = [copy(4 + j, (*chip, mc), sibling) for j, chip in enumerate(chips)]
        last = [copy(0, sibling, me)] + [copy(4 + j, (*chip, 1 - mc), me) for j, chip in enumerate(chips)]
        return slot(*me), first, landed, passed, last

    def start(self, ins, outs, send_sems, recv_sems, local_sems):
        for a in range(self.n):
            mine, first, _, _, _ = self._copies(a, ins[a], outs[a], send_sems, recv_sems)
            pltpu.make_async_copy(ins[a], mine, local_sems.at[a]).start()
            for cp in first:
                cp.start()

    def middle(self, ins, outs, send_sems, recv_sems, local_sems):
        for a in range(self.n):
            _, _, landed, passed, _ = self._copies(a, ins[a], outs[a], send_sems, recv_sems)
            for arrived, onward in zip(landed, passed):
                arrived.wait_recv()
                onward.start()

    def finish(self, ins, outs, send_sems, recv_sems, local_sems):
        for a in range(self.n):
            mine, first, _, passed, last = self._copies(a, ins[a], outs[a], send_sems, recv_sems)
            for cp in last:
                cp.wait_recv()
            for cp in first + passed:
                cp.wait_send()
            pltpu.make_async_copy(ins[a], mine, local_sems.at[a]).wait()


class _ExchangeRider:
    def __init__(self, arrays):
        self.blocks = list(arrays)
        self.n = len(self.blocks)
        self.in_specs = [_ANY] * self.n
        self.out_specs = [_ANY] * self.n
        self.out_shape = [jax.ShapeDtypeStruct(b.shape, b.dtype) for b in self.blocks]
        self.scratch = [pltpu.SemaphoreType.DMA((7 * self.n,)), pltpu.SemaphoreType.DMA((7 * self.n,)),
                        pltpu.SemaphoreType.DMA((self.n,))]

    def _copies(self, a, g_ref, land_ref, send_sems, recv_sems):
        mx, my, mc = lax.axis_index("x"), lax.axis_index("y"), lax.axis_index("c")
        me = 4 * mx + 2 * my + mc
        sends, recvs = [], []
        for k in range(1, N_DEV):
            peer = (_flip(mx, k & 4), _flip(my, k & 2), _flip(mc, k & 1))
            p = 4 * peer[0] + 2 * peer[1] + peer[2]
            sems = dict(send_sem=send_sems.at[7 * a + k - 1], recv_sem=recv_sems.at[7 * a + k - 1], device_id=peer, device_id_type=_MESH)
            sends.append(pltpu.make_async_remote_copy(src_ref=g_ref.at[p], dst_ref=land_ref.at[me], **sems))
            recvs.append(pltpu.make_async_remote_copy(src_ref=g_ref.at[p], dst_ref=land_ref.at[p], **sems))
        return g_ref.at[me], land_ref.at[me], sends, recvs

    def start(self, ins, outs, send_sems, recv_sems, local_sems):
        for a in range(self.n):
            src, dst, sends, _ = self._copies(a, ins[a], outs[a], send_sems, recv_sems)
            pltpu.make_async_copy(src, dst, local_sems.at[a]).start()
            for cp in sends:
                cp.start()

    def middle(self, ins, outs, send_sems, recv_sems, local_sems):
        pass

    def finish(self, ins, outs, send_sems, recv_sems, local_sems):
        for a in range(self.n):
            src, dst, sends, recvs = self._copies(a, ins[a], outs[a], send_sems, recv_sems)
            for cp in recvs:
                cp.wait_recv()
            for cp in sends:
                cp.wait_send()
            pltpu.make_async_copy(src, dst, local_sems.at[a]).wait()


class _NoRider:
    blocks, in_specs, out_specs, out_shape, scratch = [], [], [], [], []


_NO_RIDER = _NoRider()


def _ride(rider, refs, n_in, n_out, first, middle, last, work):
    if rider is None:
        return work(*refs)
    r = rider.n
    own = refs[:n_in] + refs[n_in + r:n_in + r + n_out] + refs[n_in + r + n_out + r:len(refs) - 3]
    args = (refs[n_in:n_in + r], refs[n_in + r + n_out:n_in + r + n_out + r], *refs[len(refs) - 3:])
    pl.when(first)(lambda: rider.start(*args))
    pl.when(middle)(lambda: rider.middle(*args))
    work(*own)
    pl.when(last)(lambda: rider.finish(*args))


def _split2(x):
    hi = x.astype(BF16)
    return hi, (x - hi.astype(F32)).astype(BF16)


def _dot_hi_lo(x, w2):
    hi, lo = _split2(x)
    return jnp.dot(jnp.concatenate([hi, lo], axis=1), w2, preferred_element_type=F32)


def _suffix_sums(x, tri2):
    return _dot_hi_lo(x, tri2)


def _dot_nt(a, b):
    return lax.dot_general(a, b, _DN["nt"], preferred_element_type=F32)


def _suffix_matrix(bk):
    j = lax.broadcasted_iota(jnp.int32, (2 * bk, bk), 0) % bk
    s = lax.broadcasted_iota(jnp.int32, (2 * bk, bk), 1)
    return (j >= s).astype(BF16)


LOG2E = 1.4426950408889634


def _sb_window(q, k, t_pos, ks, bk, scale2, carry_c, tri, masked):
    width = k.shape[0]
    z = _dot_nt(q, k) * scale2
    sp = jnp.maximum(z, 0.0) + jnp.log2(1.0 + jnp.exp2(-jnp.abs(z)))
    if masked:
        mask = ks + lax.broadcasted_iota(jnp.int32, (1, width), 1) < t_pos
        lkm = jnp.where(mask, -sp, 0.0)
    else:
        mask, lkm = None, -sp
    sums, run = _window_suffix_sums(lkm, bk, tri, carry_c)
    a = jnp.exp2(z + sums)
    if masked:
        a = jnp.where(mask, a, 0.0)
    return z, sp, mask, a, run


def _window_suffix_sums(x, bk, tri, carry):
    nb = x.shape[1] // bk
    parts, run = [None] * nb, carry
    for b in reversed(range(nb)):
        cs = _suffix_sums(x[:, b * bk:(b + 1) * bk], tri)
        parts[b] = run + cs
        run = run + cs[:, 0:1]
    return (parts[0] if nb == 1 else jnp.concatenate(parts, axis=1)), run


KEY_WINDOWS = (4, 2, 1)


def _key_tiles(i, bk, step, carry):
    carry = step(pl.multiple_of(i * bk, bk), bk, carry, True)
    done = 0
    for w in sorted(KEY_WINDOWS):
        wider = [x for x in KEY_WINDOWS if x > w]
        trips = (i % min(wider)) // w if wider else (i - done) // w

        def body(p, c, w=w, done=done):
            return step(pl.multiple_of((i - done - w * (p + 1)) * bk, bk), w * bk, c, False)

        carry = lax.fori_loop(0, trips, body, carry)
        done = done + trips * w
    return carry


def _attn_specs(H, L, bq):
    W3 = 3 * H

    def col(role):
        return lambda h, i: (h // H) * W3 + role * H + h % H

    q_spec = pl.BlockSpec((bq, HEAD_DIM), lambda h, i: (i, col(0)(h, i)))
    k_spec = pl.BlockSpec((L, HEAD_DIM), lambda h, i: (0, col(1)(h, i)))
    v_spec = pl.BlockSpec((L, HEAD_DIM), lambda h, i: (0, col(2)(h, i)))
    crow_spec = pl.BlockSpec((None, 1, L), lambda h, i: (jnp.maximum(h - H, 0), 0, 0))
    ccol_spec = pl.BlockSpec((None, bq, 1), lambda h, i: (jnp.maximum(h - H, 0), i, 0))
    g_spec = pl.BlockSpec((None, 1, HEAD_DIM), lambda h, i: (h, 0, 0))
    tile = pl.BlockSpec((bq, HEAD_DIM), lambda h, i: (i, h))
    stat = pl.BlockSpec((None, bq, 1), lambda h, i: (h, i, 0))
    return q_spec, k_spec, v_spec, crow_spec, ccol_spec, g_spec, tile, stat


def _attn_fwd(qkv, crow, ccol, g_heads, H, name, rider=None):
    L = qkv.shape[0]
    bq = bk = min(ATT_BLOCK, L)
    nq = L // bq
    scale = HEAD_DIM ** -0.5
    scale2 = scale * LOG2E
    q_spec, k_spec, v_spec, crow_spec, ccol_spec, g_spec, tile, stat = _attn_specs(H, L, bq)

    def work(q_ref, k_ref, v_ref, crow_ref, ccol_ref, g_ref, o_ref, on_ref, lse_ref):
        h, i = pl.program_id(0), pl.program_id(1)
        q = q_ref[...]
        t_pos = i * bq + lax.broadcasted_iota(jnp.int32, (bq, 1), 0)

        def finish(o):
            o_ref[...] = o
            on_ref[...] = _rms(o, g_ref[...]).astype(BF16)

        @pl.when(h < H)
        def _stick_breaking():
            tri = _suffix_matrix(bk)

            def step(ks, rows, carry, masked):
                c, acc = carry
                k = k_ref[pl.ds(ks, rows), :]
                v = v_ref[pl.ds(ks, rows), :]
                _, _, _, a, c = _sb_window(q, k, t_pos, ks, bk, scale2, c, tri, masked)
                acc = acc + _dot_hi_lo(a, jnp.concatenate([v, v], axis=0))
                return c, acc

            _, acc = _key_tiles(i, bk, step, (jnp.zeros((bq, 1), F32), jnp.zeros((bq, HEAD_DIM), F32)))
            finish(acc)
            lse_ref[...] = jnp.zeros_like(lse_ref)

        @pl.when(h >= H)
        def _forgetting():
            cq = ccol_ref[...] * LOG2E

            def step(ks, rows, carry, masked):
                m, l, acc = carry
                k = k_ref[pl.ds(ks, rows), :]
                v = v_ref[pl.ds(ks, rows), :]
                s = _dot_nt(q, k) * scale2 + (cq - crow_ref[:, pl.ds(ks, rows)] * LOG2E)
                if masked:
                    s = jnp.where(ks + lax.broadcasted_iota(jnp.int32, (1, rows), 1) <= t_pos, s, NEG_INF)
                m_new = jnp.maximum(m, jnp.max(s, axis=-1, keepdims=True))
                alpha = jnp.exp2(m - m_new)
                p = jnp.exp2(s - m_new)
                l = alpha * l + jnp.sum(p, axis=-1, keepdims=True)
                acc = alpha * acc + _dot_hi_lo(p, jnp.concatenate([v, v], axis=0))
                return m_new, l, acc

            init = (jnp.full((bq, 1), NEG_INF, F32), jnp.zeros((bq, 1), F32), jnp.zeros((bq, HEAD_DIM), F32))
            m, l, acc = _key_tiles(i, bk, step, init)
            finish(acc / l)
            lse_ref[...] = m + jnp.log2(l)

    def body(*refs):
        h, i = pl.program_id(0), pl.program_id(1)
        _ride(rider, refs, 6, 3, (h == 0) & (i == 0), (h == H) & (i == 0), (h == 2 * H - 1) & (i == nq - 1), work)

    W2 = 2 * H * HEAD_DIM
    est = 4 * L * HEAD_DIM * 2 + 32 * bq * bk * 4 + (4 << 20)
    extra = rider or _NO_RIDER
    outs = pl.pallas_call(
        body, name=name, grid=(2 * H, nq),
        in_specs=[q_spec, k_spec, v_spec, crow_spec, ccol_spec, g_spec] + extra.in_specs,
        out_specs=[tile, tile, stat] + extra.out_specs,
        out_shape=[jax.ShapeDtypeStruct((L, W2), F32), jax.ShapeDtypeStruct((L, W2), BF16),
                   jax.ShapeDtypeStruct((2 * H, L, 1), F32)] + extra.out_shape,
        scratch_shapes=extra.scratch,
        compiler_params=_params(("arbitrary", "arbitrary"), est),
    )(qkv, qkv, qkv, crow, ccol, g_heads, *extra.blocks)
    return outs[0], outs[1], outs[2], list(outs[3:])


def _attn_bwd(qkv, crow, ccol, g_heads, o, lse, d_on, H, name, rider=None):
    L = qkv.shape[0]
    bq = bk = min(ATT_BLOCK, L)
    nq = L // bq
    scale = HEAD_DIM ** -0.5
    scale2 = scale * LOG2E
    q_spec, k_spec, v_spec, crow_spec, ccol_spec, g_spec, tile, stat = _attn_specs(H, L, bq)
    full = pl.BlockSpec((L, HEAD_DIM), lambda h, i: (0, h))
    dg_spec = pl.BlockSpec((None, 1, HEAD_DIM), lambda h, i: (h, 0, 0))
    dc_spec = pl.BlockSpec((None, 1, L), lambda h, i: (h, 0, 0))

    def work(q_ref, k_ref, v_ref, crow_ref, ccol_ref, g_ref, o_ref, lse_ref, don_ref,
             dq_ref, dk_ref, dv_ref, dg_ref, dc_ref, dkt_acc, dvt_acc):
        h, i = pl.program_id(0), pl.program_id(1)

        @pl.when(i == 0)
        def _():
            dkt_acc[...] = jnp.zeros_like(dkt_acc)
            dvt_acc[...] = jnp.zeros_like(dvt_acc)
            dg_ref[...] = jnp.zeros_like(dg_ref)
            dc_ref[...] = jnp.zeros_like(dc_ref)

        q = q_ref[...]
        t_pos = i * bq + lax.broadcasted_iota(jnp.int32, (bq, 1), 0)
        o_t = o_ref[...]
        d_o, dg = _rms_bwd(o_t, g_ref[...], don_ref[...])
        dg_ref[...] += jnp.sum(dg, axis=0, keepdims=True)
        d_ob = d_o.astype(BF16)
        dsum = jnp.sum(d_ob.astype(F32) * o_t, axis=-1, keepdims=True)
        q_t = q.astype(F32).T.astype(BF16)
        d_obt = d_o.T.astype(BF16)

        @pl.when(h < H)
        def _stick_breaking():
            tri = _suffix_matrix(bk)

            def step(ks, rows, carry, masked):
                c, gs, dq = carry
                k = k_ref[pl.ds(ks, rows), :]
                v = v_ref[pl.ds(ks, rows), :]
                z, sp, mask, a, c = _sb_window(q, k, t_pos, ks, bk, scale2, c, tri, masked)
                g_w = a * _dot_nt(d_ob, v)
                later, gs = _window_suffix_sums(g_w, bk, tri, gs)
                prefix = dsum - (later - g_w)
                dz = (g_w - jnp.exp2(z - sp) * prefix) * scale
                if masked:
                    dz = jnp.where(mask, dz, 0.0)
                dzb = dz.astype(BF16)
                dq = dq + jnp.dot(dzb, k, preferred_element_type=F32)
                dkt_acc[:, pl.ds(ks, rows)] += jnp.dot(q_t, dzb, preferred_element_type=F32)
                dvt_acc[:, pl.ds(ks, rows)] += jnp.dot(d_obt, a.astype(BF16), preferred_element_type=F32)
                return c, gs, dq

            z1 = jnp.zeros((bq, 1), F32)
            _, _, dq = _key_tiles(i, bk, step, (z1, z1, jnp.zeros((bq, HEAD_DIM), F32)))
            dq_ref[...] = dq.astype(BF16)

        @pl.when(h >= H)
        def _forgetting():
            cq = ccol_ref[...] * LOG2E - lse_ref[...]

            def step(ks, rows, dq, masked):
                k = k_ref[pl.ds(ks, rows), :]
                v = v_ref[pl.ds(ks, rows), :]
                p = jnp.exp2(_dot_nt(q, k) * scale2 + (cq - crow_ref[:, pl.ds(ks, rows)] * LOG2E))
                if masked:
                    p = jnp.where(ks + lax.broadcasted_iota(jnp.int32, (1, rows), 1) <= t_pos, p, 0.0)
                ds = p * (_dot_nt(d_ob, v) - dsum)
                dsb = (ds * scale).astype(BF16)
                dq = dq + jnp.dot(dsb, k, preferred_element_type=F32)
                dkt_acc[:, pl.ds(ks, rows)] += jnp.dot(q_t, dsb, preferred_element_type=F32)
                dvt_acc[:, pl.ds(ks, rows)] += jnp.dot(d_obt, p.astype(BF16), preferred_element_type=F32)
                dc_ref[:, pl.ds(ks, rows)] += -jnp.sum(ds, axis=0, keepdims=True)
                return dq

            dq = _key_tiles(i, bk, step, jnp.zeros((bq, HEAD_DIM), F32))
            dq_ref[...] = dq.astype(BF16)

        @pl.when(i == nq - 1)
        def _():
            dk_ref[...] = dkt_acc[...].T.astype(BF16)
            dv_ref[...] = dvt_acc[...].T.astype(BF16)

    W2 = 2 * H * HEAD_DIM
    est = 4 * L * HEAD_DIM * 2 + 4 * L * HEAD_DIM * 2 + 2 * L * HEAD_DIM * 4 + 48 * bq * bk * 4 + (4 << 20)
    def body(*refs):
        h, i = pl.program_id(0), pl.program_id(1)
        _ride(rider, refs, 9, 5, (h == 0) & (i == 0), (h == H) & (i == 0), (h == 2 * H - 1) & (i == nq - 1), work)

    extra = rider or _NO_RIDER
    dq, dk, dv, dg, dc, *brought = pl.pallas_call(
        body, name=name, grid=(2 * H, nq),
        in_specs=[q_spec, k_spec, v_spec, crow_spec, ccol_spec, g_spec, tile, stat, tile] + extra.in_specs,
        out_specs=[tile, full, full, dg_spec, dc_spec] + extra.out_specs,
        out_shape=[jax.ShapeDtypeStruct((L, W2), BF16)] * 3
        + [jax.ShapeDtypeStruct((2 * H, 1, HEAD_DIM), F32), jax.ShapeDtypeStruct((2 * H, 1, L), F32)] + extra.out_shape,
        scratch_shapes=[pltpu.VMEM((HEAD_DIM, L), F32), pltpu.VMEM((HEAD_DIM, L), F32)] + extra.scratch,
        compiler_params=_params(("arbitrary", "arbitrary"), est),
    )(qkv, qkv, qkv, crow, ccol, g_heads, o, lse, d_on, *extra.blocks)
    return dq, dk, dv, dg[:, 0, :], dc, brought


HALO = 16


def _conv_tiles(L, F):
    return _row_tile(L), _divisor(F, (512, 256, 128))


def _shift_down(x, halo_last2, first, bt):
    row = lax.broadcasted_iota(jnp.int32, (bt, 1), 0)
    h1 = jnp.where(first, 0.0, halo_last2[1:2, :])
    h2 = jnp.where(first, 0.0, halo_last2[0:1, :])
    x1 = jnp.where(row == 0, h1, pltpu.roll(x, 1, 0))
    x2 = jnp.where(row == 0, h2, jnp.where(row == 1, h1, pltpu.roll(x, 2, 0)))
    return x1, x2


def _conv_rows(p_ref, halo_ref, w_ref, b_ref, first, bt):
    p = p_ref[...].astype(F32)
    hl = halo_ref[...].astype(F32)
    p1, p2 = _shift_down(p, hl[HALO - 2:HALO, :], first, bt)
    w = w_ref[...]
    a = w[0:1, :] * p2 + w[1:2, :] * p1 + w[2:3, :] * p + b_ref[...]
    return a, p, p1, p2


def _sigmoid(x):
    return 1.0 / (1.0 + jnp.exp(-x))


def _conv_in_specs(L, F, bt, bc, order):
    nf = F // bc
    r = bt // HALO
    ix = (lambda a, b: (a, b)) if order == "ij" else (lambda a, b: (b, a))

    def mk(shape, fn):
        return pl.BlockSpec(shape, lambda a, b: fn(*ix(a, b)))

    return [
        mk((bt, bc), lambda i, j: (i, j)), mk((HALO, bc), lambda i, j: (jnp.maximum(i * r - 1, 0), j)),
        mk((bt, bc), lambda i, j: (i, nf + j)), mk((HALO, bc), lambda i, j: (jnp.maximum(i * r - 1, 0), nf + j)),
        mk((3, bc), lambda i, j: (0, j)), mk((3, bc), lambda i, j: (0, nf + j)),
        mk((1, bc), lambda i, j: (0, j)), mk((1, bc), lambda i, j: (0, nf + j)),
    ]


def _convgate_fwd(p, conv_w, conv_b, name):
    L, F2 = p.shape
    F = F2 // 2
    bt, bc = _conv_tiles(L, F)

    def body(pg, hg, pu, hu, wg, wu, bg, bu, act_ref):
        first = pl.program_id(0) == 0
        ag = _conv_rows(pg, hg, wg, bg, first, bt)[0]
        au = _conv_rows(pu, hu, wu, bu, first, bt)[0]
        act_ref[...] = (ag * _sigmoid(ag) * au).astype(BF16)

    return pl.pallas_call(
        body, name=name, grid=(L // bt, F // bc), in_specs=_conv_in_specs(L, F, bt, bc, "ij"),
        out_specs=pl.BlockSpec((bt, bc), lambda i, j: (i, j)), out_shape=jax.ShapeDtypeStruct((L, F), BF16),
        compiler_params=_params(("parallel", "parallel"), 24 * bt * bc * 4),
    )(p, p, p, p, conv_w, conv_w, conv_b.reshape(1, F2), conv_b.reshape(1, F2))


def _convgate_bwd(p, conv_w, conv_b, d_act, name):
    L, F2 = p.shape
    F = F2 // 2
    bt, bc = _conv_tiles(L, F)

    def body(pg, hg, pu, hu, wg, wu, bg, bu, dact_ref, da_ref, dwb_ref):
        first = pl.program_id(1) == 0
        ag, xg, xg1, xg2 = _conv_rows(pg, hg, wg, bg, first, bt)
        au, xu, xu1, xu2 = _conv_rows(pu, hu, wu, bu, first, bt)
        d_act = dact_ref[...].astype(F32)
        sg = _sigmoid(ag)
        dag = d_act * au * sg * (1.0 + ag * (1.0 - sg))
        dau = d_act * ag * sg
        da_ref[0] = dag.astype(BF16)
        da_ref[1] = dau.astype(BF16)

        @pl.when(first)
        def _():
            dwb_ref[...] = jnp.zeros_like(dwb_ref)
        cs = lambda v: jnp.sum(v, axis=0, keepdims=True)
        dwb_ref[0] += jnp.concatenate([cs(dag * xg2), cs(dag * xg1), cs(dag * xg), cs(dag)], axis=0)
        dwb_ref[1] += jnp.concatenate([cs(dau * xu2), cs(dau * xu1), cs(dau * xu), cs(dau)], axis=0)

    da, dwb = pl.pallas_call(
        body, name=name, grid=(F // bc, L // bt),
        in_specs=_conv_in_specs(L, F, bt, bc, "ji") + [pl.BlockSpec((bt, bc), lambda j, i: (i, j))],
        out_specs=[pl.BlockSpec((2, bt, bc), lambda j, i: (0, i, j)), pl.BlockSpec((2, 4, bc), lambda j, i: (0, 0, j))],
        out_shape=[jax.ShapeDtypeStruct((2, L, F), BF16), jax.ShapeDtypeStruct((2, 4, F), F32)],
        compiler_params=_params(("parallel", "arbitrary"), 40 * bt * bc * 4),
    )(p, p, p, p, conv_w, conv_w, conv_b.reshape(1, F2), conv_b.reshape(1, F2), d_act)
    d_w = jnp.concatenate([dwb[0, 0:3], dwb[1, 0:3]], axis=1)
    d_b = jnp.concatenate([dwb[0, 3], dwb[1, 3]], axis=0)
    return da, d_w, d_b


def _conv_bwd_data(da, conv_w, name):
    _, L, F = da.shape
    bt, bc = _conv_tiles(L, F)
    nf, r, nt = F // bc, bt // HALO, L // bt

    def body(da_ref, nxt_ref, w_ref, dp_ref):
        last = pl.program_id(0) == nt - 1
        x = da_ref[...].astype(F32)
        nx = nxt_ref[...].astype(F32)
        n0 = jnp.where(last, 0.0, nx[0:1, :])
        n1 = jnp.where(last, 0.0, nx[1:2, :])
        row = lax.broadcasted_iota(jnp.int32, (bt, 1), 0)
        x1 = jnp.where(row == bt - 1, n0, pltpu.roll(x, bt - 1, 0))
        x2 = jnp.where(row == bt - 1, n1, jnp.where(row == bt - 2, n0, pltpu.roll(x, bt - 2, 0)))
        w = w_ref[...]
        dp_ref[...] = (w[2:3, :] * x + w[1:2, :] * x1 + w[0:1, :] * x2).astype(BF16)

    return pl.pallas_call(
        body, name=name, grid=(nt, 2 * nf),
        in_specs=[pl.BlockSpec((None, bt, bc), lambda i, j: (j // nf, i, j % nf)),
                  pl.BlockSpec((None, HALO, bc), lambda i, j: (j // nf, jnp.minimum((i + 1) * r, nt * r - 1), j % nf)),
                  pl.BlockSpec((3, bc), lambda i, j: (0, j))],
        out_specs=pl.BlockSpec((bt, bc), lambda i, j: (i, j)), out_shape=jax.ShapeDtypeStruct((L, 2 * F), BF16),
        compiler_params=_params(("parallel", "parallel"), 16 * bt * bc * 4),
    )(da, da, conv_w)


def _pad_cols(a, n):
    return jnp.pad(a, ((0, 0), (0, n - a.shape[1])))


def _local_step(x, tgt, meta, W, hooks=None):
    S, D = x.shape
    n_meta = meta.shape[0]
    depth = len(W["w_in"])
    H = D // (2 * HEAD_DIM)
    WQ = 6 * H * HEAD_DIM
    L = -(-(S + n_meta) // ATT_BLOCK) * ATT_BLOCK
    tail = L - S - n_meta
    zeros_tail = jnp.zeros((tail, D), F32)
    h = jnp.concatenate([meta, x, zeros_tail], axis=0)
    tgt_p = jnp.concatenate([jnp.zeros((n_meta, D), F32), tgt, zeros_tail], axis=0)

    saved = []
    _, u1 = _resnorm_fwd(h, None, None, W["g_mix_pre"][0], "prenorm0")
    for l in range(depth):
        w_in = W["w_in"][l]
        w_qkv, w_f = w_in[:, :WQ], _pad_cols(w_in[:, WQ:], LANES)
        b_pad = jnp.pad(W["b_f"][l], (0, LANES - H)).reshape(1, LANES)
        g_heads = jnp.concatenate([W["g_sb"][l], W["g_fox"][l]], axis=0).reshape(2 * H, 1, HEAD_DIM)
        qkv = _mm(u1, w_qkv, "nn", BF16, f"qkv{l}")
        fl = _mm(u1, w_f, "nn", F32, f"flogit{l}")
        c = _gate_fwd(fl, b_pad, f"gate_fwd{l}")
        c_heads = c[:, :H].T
        crow, ccol = c_heads[:, None, :], c_heads[:, :, None]
        o, on, lse, brought = _attn_fwd(qkv, crow, ccol, g_heads, H, f"attn_fwd{l}", hooks and hooks.fwd_rider(l))
        if hooks:
            hooks.fwd_done(l, brought, W)
        mix = _mm(on, W["w_out"][l], "nn", F32, f"mix{l}")
        h_mid, u2 = _resnorm_fwd(h, mix, W["g_mix_post"][l], W["g_ffn_pre"][l], f"resnorm_a{l}")
        p = _mm(u2, W["w_up"][l], "nn", BF16, f"up{l}")
        act = _convgate_fwd(p, W["conv_w"][l], W["conv_b"][l], f"convgate{l}")
        ff = _mm(act, W["w_down"][l], "nn", F32, f"down{l}")
        g_next = W["g_mix_pre"][l + 1] if l + 1 < depth else None
        h_out, u1_next = _resnorm_fwd(h_mid, ff, W["g_ffn_post"][l], g_next, f"resnorm_b{l}")
        saved.append(dict(h_in=h, u1=u1, w_qkv=w_qkv, w_f=w_f, b_pad=b_pad, g_heads=g_heads, qkv=qkv, fl=fl, crow=crow,
                          ccol=ccol, o=o, on=on, lse=lse, mix=mix, h_mid=h_mid, u2=u2, p=p, act=act, ff=ff, h_out=h_out))
        h, u1 = h_out, u1_next

    loss, dh = _loss(h, tgt_p, n_meta, S, "loss")

    grads = {k: [None] * depth for k in ("g_mix_pre", "w_in", "b_f", "g_sb", "g_fox", "w_out", "g_mix_post", "g_ffn_pre",
                                         "w_up", "conv_w", "conv_b", "w_down", "g_ffn_post")}
    du1_next = None
    for l in reversed(range(depth)):
        s = saved[l]
        g_next = W["g_mix_pre"][l + 1] if l + 1 < depth else None
        dh, d_ff, grads["g_ffn_post"][l], dg_pre_next = _resnorm_bwd(
            dh, du1_next, s["h_out"], s["ff"], W["g_ffn_post"][l], g_next, f"resnorm_b_bwd{l}")
        if l + 1 < depth:
            grads["g_mix_pre"][l + 1] = dg_pre_next
        d_act = _mm(d_ff, W["w_down"][l], "nt", BF16, f"d_act{l}")
        grads["w_down"][l] = _mm(s["act"], d_ff, "tn", F32, f"dw_down{l}")
        da, grads["conv_w"][l], grads["conv_b"][l] = _convgate_bwd(s["p"], W["conv_w"][l], W["conv_b"][l], d_act, f"convgate_bwd{l}")
        dp = _conv_bwd_data(da, W["conv_w"][l], f"conv_bwd{l}")
        du2 = _mm(dp, W["w_up"][l], "nt", F32, f"d_u2{l}")
        grads["w_up"][l] = _mm(s["u2"], dp, "tn", F32, f"dw_up{l}")
        dh, d_mix, grads["g_mix_post"][l], grads["g_ffn_pre"][l] = _resnorm_bwd(
            dh, du2, s["h_mid"], s["mix"], W["g_mix_post"][l], W["g_ffn_pre"][l], f"resnorm_a_bwd{l}")
        d_on = _mm(d_mix, W["w_out"][l], "nt", F32, f"d_on{l}")
        grads["w_out"][l] = _mm(s["on"], d_mix, "tn", F32, f"dw_out{l}")
        dq, dk, dv, dg_heads, dcrow, brought = _attn_bwd(s["qkv"], s["crow"], s["ccol"], s["g_heads"], s["o"], s["lse"], d_on, H,
                                                         f"attn_bwd{l}", hooks and hooks.bwd_rider(l, grads))
        if hooks:
            hooks.bwd_done(l, brought)
        grads["g_sb"][l], grads["g_fox"][l] = dg_heads[:H], dg_heads[H:]
        dc = _pad_cols(dcrow[H:, 0, :].T, LANES)
        dfl, db = _gate_bwd(dc, s["fl"], s["b_pad"], f"gate_bwd{l}")
        grads["b_f"][l] = db[:H]
        Wh = H * HEAD_DIM
        d_qkv = jnp.concatenate([dq[:, :Wh], dk[:, :Wh], dv[:, :Wh], dq[:, Wh:], dk[:, Wh:], dv[:, Wh:]], axis=1)
        du1_next = _mm(d_qkv, s["w_qkv"], "nt", F32, f"d_u1{l}", a2=dfl, b2=s["w_f"])
        dw_qkv = _mm(s["u1"], d_qkv, "tn", F32, f"dw_qkv{l}")
        dw_f = _mm(s["u1"], dfl, "tn", F32, f"dw_f{l}")
        grads["w_in"][l] = jnp.concatenate([dw_qkv, dw_f[:, :H]], axis=1)
    dh0, _, _, grads["g_mix_pre"][0] = _resnorm_bwd(dh, du1_next, saved[0]["h_in"], None, None, W["g_mix_pre"][0], "prenorm0_bwd")
    return loss, dh0[n_meta:n_meta + S], dh0[:n_meta], grads


def _all_gather(x, name):
    def body(x_ref, out_ref, send_sems, recv_sems, local_sem):
        mx, my, mc = lax.axis_index("x"), lax.axis_index("y"), lax.axis_index("c")
        me, sibling = (mx, my, mc), (mx, my, 1 - mc)
        chips = [(1 - mx, my), (mx, 1 - my), (1 - mx, 1 - my)]

        def slot(px, py, pc):
            return out_ref.at[4 * px + 2 * py + pc]

        def copy(k, block, to, src=None):
            return pltpu.make_async_remote_copy(
                src_ref=slot(*block) if src is None else src, dst_ref=slot(*block),
                send_sem=send_sems.at[k], recv_sem=recv_sems.at[k], device_id=to, device_id_type=_MESH)

        mine = pltpu.make_async_copy(x_ref, slot(*me), local_sem)
        mine.start()
        first = [copy(0, me, sibling, src=x_ref)]
        first += [copy(1 + j, me, (*chip, mc), src=x_ref) for j, chip in enumerate(chips)]
        for cp in first:
            cp.start()
        passed = [copy(4 + j, (*chip, mc), sibling) for j, chip in enumerate(chips)]
        for j, chip in enumerate(chips):
            copy(1 + j, (*chip, mc), me).wait_recv()
            passed[j].start()
        copy(0, sibling, me).wait_recv()
        for j, chip in enumerate(chips):
            copy(4 + j, (*chip, 1 - mc), me).wait_recv()
        for cp in first + passed:
            cp.wait_send()
        mine.wait()

    return pl.pallas_call(
        body, name=name, out_shape=jax.ShapeDtypeStruct((N_DEV, *x.shape), x.dtype),
        in_specs=[_ANY], out_specs=_ANY,
        scratch_shapes=[pltpu.SemaphoreType.DMA((7,)), pltpu.SemaphoreType.DMA((7,)), pltpu.SemaphoreType.DMA],
    )(x)


def _exchange(g, name):
    def body(g_ref, land_ref, send_sems, recv_sems, local_sem):
        mx, my, mc = lax.axis_index("x"), lax.axis_index("y"), lax.axis_index("c")
        me = 4 * mx + 2 * my + mc
        local = pltpu.make_async_copy(g_ref.at[me], land_ref.at[me], local_sem)
        local.start()
        sends, recvs = [], []
        for k in range(1, N_DEV):
            peer = (_flip(mx, k & 4), _flip(my, k & 2), _flip(mc, k & 1))
            p = 4 * peer[0] + 2 * peer[1] + peer[2]
            sends.append(pltpu.make_async_remote_copy(
                src_ref=g_ref.at[p], dst_ref=land_ref.at[me], send_sem=send_sems.at[k - 1], recv_sem=recv_sems.at[k - 1],
                device_id=peer, device_id_type=_MESH))
            recvs.append(pltpu.make_async_remote_copy(
                src_ref=g_ref.at[p], dst_ref=land_ref.at[p], send_sem=send_sems.at[k - 1], recv_sem=recv_sems.at[k - 1],
                device_id=peer, device_id_type=_MESH))
        for cp in sends:
            cp.start()
        for cp in recvs:
            cp.wait_recv()
        for cp in sends:
            cp.wait_send()
        local.wait()

    return pl.pallas_call(
        body, name=name, out_shape=jax.ShapeDtypeStruct(g.shape, g.dtype), in_specs=[_ANY], out_specs=_ANY,
        scratch_shapes=[pltpu.SemaphoreType.DMA((7,)), pltpu.SemaphoreType.DMA((7,)), pltpu.SemaphoreType.DMA],
    )(g)


def _exchange_cores(g, name):
    def body(g_ref, land_ref, send_sems, recv_sems):
        mx, my, mc = lax.axis_index("x"), lax.axis_index("y"), lax.axis_index("c")
        copies = [pltpu.make_async_remote_copy(
            src_ref=g_ref.at[2 * q + (1 - mc)], dst_ref=land_ref.at[q], send_sem=send_sems.at[q], recv_sem=recv_sems.at[q],
            device_id=(mx, my, 1 - mc), device_id_type=_MESH) for q in range(4)]
        for cp in copies:
            cp.start()
        for cp in copies:
            cp.wait_recv()
        for cp in copies:
            cp.wait_send()

    return pl.pallas_call(
        body, name=name, out_shape=jax.ShapeDtypeStruct((4, *g.shape[1:]), g.dtype), in_specs=[_ANY], out_specs=_ANY,
        scratch_shapes=[pltpu.SemaphoreType.DMA((4,)), pltpu.SemaphoreType.DMA((4,))],
    )(g)


def _pair_sum(g, land, name):
    _, n, R, C = g.shape
    tr = _divisor(R, (128, 64, 32, 16, 8))
    core = lax.axis_index("c").astype(jnp.int32).reshape(1)

    def body(c_ref, g_ref, land_ref, o_ref):
        o_ref[...] = (g_ref[...].astype(F32) + land_ref[...].astype(F32)).astype(o_ref.dtype)

    blk = pl.BlockSpec((None, None, tr, C), lambda q, l, r, c_ref: (q, l, r, 0))
    return pl.pallas_call(
        body, name=name, out_shape=jax.ShapeDtypeStruct(land.shape, g.dtype),
        grid_spec=pltpu.PrefetchScalarGridSpec(
            num_scalar_prefetch=1, grid=(4, n, R // tr),
            in_specs=[pl.BlockSpec((None, None, tr, C), lambda q, l, r, c_ref: (2 * q + c_ref[0], l, r, 0)), blk],
            out_specs=blk),
        compiler_params=_params(("parallel", "parallel", "parallel"), 8 * tr * C * 4),
    )(core, g, land)


def _exchange_chips(part, name):
    def body(p_ref, land_ref, send_sems, recv_sems, local_sem):
        mx, my, mc = lax.axis_index("x"), lax.axis_index("y"), lax.axis_index("c")
        me = 2 * mx + my
        local = pltpu.make_async_copy(p_ref.at[me], land_ref.at[me], local_sem)
        local.start()
        sends, recvs = [], []
        for k in range(1, 4):
            px, py = _flip(mx, k & 2), _flip(my, k & 1)
            p = 2 * px + py
            sends.append(pltpu.make_async_remote_copy(
                src_ref=p_ref.at[p], dst_ref=land_ref.at[me], send_sem=send_sems.at[k - 1], recv_sem=recv_sems.at[k - 1],
                device_id=(px, py, mc), device_id_type=_MESH))
            recvs.append(pltpu.make_async_remote_copy(
                src_ref=p_ref.at[p], dst_ref=land_ref.at[p], send_sem=send_sems.at[k - 1], recv_sem=recv_sems.at[k - 1],
                device_id=(px, py, mc), device_id_type=_MESH))
        for cp in sends:
            cp.start()
        for cp in recvs:
            cp.wait_recv()
        for cp in sends:
            cp.wait_send()
        local.wait()

    return pl.pallas_call(
        body, name=name, out_shape=jax.ShapeDtypeStruct(part.shape, part.dtype), in_specs=[_ANY], out_specs=_ANY,
        scratch_shapes=[pltpu.SemaphoreType.DMA((3,)), pltpu.SemaphoreType.DMA((3,)), pltpu.SemaphoreType.DMA],
    )(part)


def _adamw_landed(lands, w, m, v, name):
    n, R, C = w.shape
    tr = _divisor(R, (64, 32, 16, 8))
    nr = R // tr
    blk = pl.BlockSpec((None, tr, C), lambda l, r: (l, r, 0))

    def land_spec(k, S):
        return pl.BlockSpec((S, tr, C), lambda l, r: (0, jnp.where(l == k, r, jnp.where(l < k, 0, nr - 1)), 0))

    def body(*refs):
        land_refs = refs[:n]
        w_ref, m_ref, v_ref, g_ref, d_ref, mo_ref, vo_ref = refs[n:]
        for k in range(n):
            @pl.when(pl.program_id(0) == k)
            def _(k=k):
                g = land_refs[k][0].astype(F32)
                for s in range(1, lands[k].shape[0]):
                    g = g + land_refs[k][s].astype(F32)
                m_new = ADAM_B1 * m_ref[...] + (1.0 - ADAM_B1) * g
                v_new = ADAM_B2 * v_ref[...] + (1.0 - ADAM_B2) * (g * g)
                m_hat = m_new / (1.0 - ADAM_B1 ** ADAM_STEP)
                v_hat = v_new / (1.0 - ADAM_B2 ** ADAM_STEP)
                g_ref[...] = g
                d_ref[...] = -ADAM_LR * (m_hat / (jnp.sqrt(v_hat) + ADAM_EPS) + ADAM_WD * w_ref[...])
                mo_ref[...] = m_new
                vo_ref[...] = v_new

    est = 2 * tr * C * (sum(x.shape[0] * jnp.dtype(x.dtype).itemsize for x in lands) + 7 * 4) * 9 // 8
    return pl.pallas_call(
        body, name=name, grid=(n, nr),
        in_specs=[land_spec(k, x.shape[0]) for k, x in enumerate(lands)] + [blk, blk, blk],
        out_specs=[blk] * 4, out_shape=[jax.ShapeDtypeStruct((n, R, C), F32)] * 4,
        compiler_params=_params(("arbitrary", "arbitrary"), est),
    )(*lands, w, m, v)


def _pack(arrs):
    flat = jnp.concatenate([a.reshape(-1).astype(F32) for a in arrs])
    rows = -(-flat.shape[0] // (8 * LANES)) * 8
    return jnp.pad(flat, (0, rows * LANES - flat.shape[0])).reshape(rows, LANES)


def _unpack(packed, shapes):
    flat, out, at = packed.reshape(-1), [], 0
    for s in shapes:
        n = math.prod(s)
        out.append(flat[at:at + n].reshape(s))
        at += n
    return out


_BIG = ("w_in", "w_out", "w_up", "w_down")
_REPLICATED = ("g_mix_pre", "b_f", "g_sb", "g_fox", "g_mix_post", "g_ffn_pre", "conv_b", "g_ffn_post")
_ORDER = ("meta", "g_mix_pre", "w_in", "b_f", "g_sb", "g_fox", "w_out", "g_mix_post", "g_ffn_pre", "w_up", "conv_w",
          "conv_b", "w_down", "g_ffn_post")
_COLUMN_SHARDED = ("w_in", "w_up")


def kernel(x, meta, g_mix_pre, w_in, b_f, g_sb, g_fox, w_out, g_mix_post, g_ffn_pre, w_up, conv_w, conv_b, w_down, g_ffn_post, loss_target, m_meta, m_g_mix_pre, m_w_in, m_b_f, m_g_sb, m_g_fox, m_w_out, m_g_mix_post, m_g_ffn_pre, m_w_up, m_conv_w, m_conv_b, m_w_down, m_g_ffn_post, v_meta, v_g_mix_pre, v_w_in, v_b_f, v_g_sb, v_g_fox, v_w_out, v_g_mix_post, v_g_ffn_pre, v_w_up, v_conv_w, v_conv_b, v_w_down, v_g_ffn_post):
    w = dict(meta=meta, g_mix_pre=g_mix_pre, w_in=w_in, b_f=b_f, g_sb=g_sb, g_fox=g_fox, w_out=w_out, g_mix_post=g_mix_post,
             g_ffn_pre=g_ffn_pre, w_up=w_up, conv_w=conv_w, conv_b=conv_b, w_down=w_down, g_ffn_post=g_ffn_post)
    m = dict(meta=m_meta, g_mix_pre=m_g_mix_pre, w_in=m_w_in, b_f=m_b_f, g_sb=m_g_sb, g_fox=m_g_fox, w_out=m_w_out,
             g_mix_post=m_g_mix_post, g_ffn_pre=m_g_ffn_pre, w_up=m_w_up, conv_w=m_conv_w, conv_b=m_conv_b, w_down=m_w_down,
             g_ffn_post=m_g_ffn_post)
    v = dict(meta=v_meta, g_mix_pre=v_g_mix_pre, w_in=v_w_in, b_f=v_b_f, g_sb=v_g_sb, g_fox=v_g_fox, w_out=v_w_out,
             g_mix_post=v_g_mix_post, g_ffn_pre=v_g_ffn_pre, w_up=v_w_up, conv_w=v_conv_w, conv_b=v_conv_b, w_down=v_w_down,
             g_ffn_post=v_g_ffn_post)
    depth = w_in.shape[0]
    shards = {name: [w[name][l].astype(BF16) for l in range(depth)] for name in _BIG}

    def assemble(name, g):
        if name in _COLUMN_SHARDED:
            return jnp.transpose(g, (1, 0, 2)).reshape(g.shape[1], -1)
        return g.reshape(-1, g.shape[2])

    def blocks_of(name, grad):
        R, C = w[name].shape[1:]
        if name in _COLUMN_SHARDED:
            return jnp.transpose(grad.reshape(R, N_DEV, C), (1, 0, 2)).astype(BF16)
        return grad.reshape(N_DEV, R, C).astype(BF16)

    class Hooks:
        def __init__(self):
            self.landed = {}

        def fwd_keys(self, l):
            return ([("w_in", l + 1)] if l + 1 < depth else []) + [("w_out", l), ("w_up", l), ("w_down", l)]

        def bwd_keys(self, l):
            return ([("w_in", l + 1)] if l + 1 < depth else []) + [("w_down", l), ("w_up", l), ("w_out", l)]

        def fwd_rider(self, l):
            return _GatherRider([shards[name][ll] for name, ll in self.fwd_keys(l)])

        def fwd_done(self, l, brought, W):
            for (name, ll), g in zip(self.fwd_keys(l), brought):
                W[name][ll] = assemble(name, g)

        def bwd_rider(self, l, grads):
            return _ExchangeRider([blocks_of(name, grads[name][ll]) for name, ll in self.bwd_keys(l)])

        def bwd_done(self, l, brought):
            for key, land in zip(self.bwd_keys(l), brought):
                self.landed[key] = land

    small_shapes = [conv_w.shape, meta.shape]
    gs = _all_gather(_pack([conv_w, meta]), "gather_small")
    parts = [_unpack(gs[d], small_shapes) for d in range(N_DEV)]
    conv_full = jnp.concatenate([p[0] for p in parts], axis=2)
    meta_full = jnp.concatenate([p[1] for p in parts], axis=1)

    W = {name: [None] * depth for name in _BIG}
    W["w_in"][0] = assemble("w_in", _all_gather(shards["w_in"][0], "gather_w_in0"))
    W["conv_w"] = [conv_full[l] for l in range(depth)]
    for name in _REPLICATED:
        W[name] = [w[name][l] for l in range(depth)]
    hooks = Hooks()
    loss, grad_x, d_meta, grads = _local_step(x[0], loss_target[0], meta_full, W, hooks)

    first = blocks_of("w_in", grads["w_in"][0])[:, None]
    pairs = _pair_sum(first, _exchange_cores(first, "exchange_cores_w_in0"), "pair_sum_w_in0")
    hooks.landed[("w_in", 0)] = _exchange_chips(pairs, "exchange_chips_w_in0")[:, 0]
    out = {}
    for name in _BIG:
        out[name] = _adamw_landed([hooks.landed[(name, l)] for l in range(depth)], w[name], m[name], v[name], f"adamw_{name}")

    Fs, Ms = conv_w.shape[2], meta.shape[1]
    d_conv = jnp.stack(grads["conv_w"], axis=0)
    blocks = jnp.stack([_pack([d_conv[:, :, d * Fs:(d + 1) * Fs], d_meta[:, d * Ms:(d + 1) * Ms]]) for d in range(N_DEV)])
    land = _exchange(blocks, "exchange_small")
    res = _adamw_landed([land], _pack([conv_w, meta])[None], _pack([m_conv_w, m_meta])[None],
                        _pack([v_conv_w, v_meta])[None], "adamw_small")
    for i, r in enumerate(res):
        cw, mt = _unpack(r[0], small_shapes)
        out.setdefault("conv_w", [None] * 4)[i] = cw
        out.setdefault("meta", [None] * 4)[i] = mt

    rep_shapes = [()] + [w[name].shape for name in _REPLICATED]
    mine = _pack([loss] + [jnp.stack(grads[name], axis=0) for name in _REPLICATED])
    land = _all_gather(mine, "gather_replicated")
    zero = jnp.zeros((), F32)
    res = _adamw_landed([land], _pack([zero] + [w[n] for n in _REPLICATED])[None],
                        _pack([zero] + [m[n] for n in _REPLICATED])[None],
                        _pack([zero + 1.0] + [v[n] for n in _REPLICATED])[None], "adamw_replicated")
    for i, r in enumerate(res):
        vals = _unpack(r[0], rep_shapes)
        if i == 0:
            loss_total = vals[0]
        for name, val in zip(_REPLICATED, vals[1:]):
            out.setdefault(name, [None] * 4)[i] = val

    return (loss_total, grad_x[None], *[out[n][0] for n in _ORDER], *[out[n][1] for n in _ORDER],
            *[out[n][2] for n in _ORDER], *[out[n][3] for n in _ORDER])
```

```python
import functools
import math

import jax
import jax.numpy as jnp
from jax import lax
from jax.experimental import pallas as pl
from jax.experimental.pallas import tpu as pltpu

F32, BF16 = jnp.float32, jnp.bfloat16
HEAD_DIM = 128
LANES = 128
EPS = 1e-6
NEG_INF = -1e30
ATT_BLOCK = 256
N_DEV = 8
V7X_VMEM_BUDGET = 56 * 1024 * 1024

ADAM_LR, ADAM_B1, ADAM_B2, ADAM_EPS, ADAM_WD, ADAM_STEP = 0.001, 0.9, 0.999, 1e-08, 0.01, 10


def _divisor(n, cands):
    for c in cands:
        if c <= n and n % c == 0:
            return c
    raise ValueError(f"no tile for {n} among {cands}")


def _params(sem, est_bytes):
    limit = int(min(V7X_VMEM_BUDGET, max(16 * 1024 * 1024, est_bytes * 5 // 4 + (2 << 20))))
    return pltpu.CompilerParams(dimension_semantics=sem, vmem_limit_bytes=limit)


def _nbytes(shape, dtype):
    return math.prod(shape) * jnp.dtype(dtype).itemsize


_DN = {"nn": (((1,), (0,)), ((), ())), "nt": (((1,), (1,)), ((), ())), "tn": (((0,), (0,)), ((), ()))}
_ROW_TILES = (1088, 544, 272, 512, 256, 128, 64, 32, 16, 8)
_COL_TILES = (512, 256, 128)


def _ktile(k, cap):
    if k <= cap:
        return k
    for t in range(cap - cap % LANES, 0, -LANES):
        if k % t == 0:
            return t
    raise ValueError(k)


def _mm(a, b, mode, out_dtype, name, a2=None, b2=None):
    if mode == "nn":
        (M, K), (_, N) = a.shape, b.shape
    elif mode == "nt":
        (M, K), (N, _) = a.shape, b.shape
    else:
        (K, M), (_, N) = a.shape, b.shape
    if mode == "tn":
        tm, tn, tk = _divisor(M, _COL_TILES), _divisor(N, _COL_TILES), _ktile(K, 4608)
    else:
        tm, tn, tk = _divisor(M, _ROW_TILES), _divisor(N, _COL_TILES), _ktile(K, 3072)
    nk = K // tk
    a_spec = {"nn": pl.BlockSpec((tm, tk), lambda i, j, k: (i, k)),
              "nt": pl.BlockSpec((tm, tk), lambda i, j, k: (i, k)),
              "tn": pl.BlockSpec((tk, tm), lambda i, j, k: (k, i))}[mode]
    b_spec = {"nn": pl.BlockSpec((tk, tn), lambda i, j, k: (k, j)),
              "nt": pl.BlockSpec((tn, tk), lambda i, j, k: (j, k)),
              "tn": pl.BlockSpec((tk, tn), lambda i, j, k: (k, j))}[mode]
    dn = _DN[mode]
    extra = a2 is not None
    in_specs, args = [a_spec, b_spec], [a, b]
    if extra:
        k2 = a2.shape[1]
        in_specs += [pl.BlockSpec((tm, k2), lambda i, j, k: (i, 0)), pl.BlockSpec((tn, k2), lambda i, j, k: (j, 0))]
        args += [a2, b2]

    def body(*refs):
        if extra:
            a_ref, b_ref, a2_ref, b2_ref, o_ref, acc = refs
        else:
            a_ref, b_ref, o_ref, acc = refs
        part = lax.dot_general(a_ref[...], b_ref[...], dn, preferred_element_type=F32)
        if nk == 1:
            if extra:
                part = part + lax.dot_general(a2_ref[...], b2_ref[...], _DN["nt"], preferred_element_type=F32)
            o_ref[...] = part.astype(o_ref.dtype)
            return
        kk = pl.program_id(2)

        @pl.when(kk == 0)
        def _():
            if extra:
                acc[...] = part + lax.dot_general(a2_ref[...], b2_ref[...], _DN["nt"], preferred_element_type=F32)
            else:
                acc[...] = part

        @pl.when(kk > 0)
        def _():
            acc[...] += part

        @pl.when(kk == nk - 1)
        def _():
            o_ref[...] = acc[...].astype(o_ref.dtype)

    est = 2 * (tm * tk + tk * tn) * 2 + 2 * _nbytes((tm, tn), out_dtype) + tm * tn * 4
    return pl.pallas_call(
        body, name=name, grid=(M // tm, N // tn, nk), in_specs=in_specs,
        out_specs=pl.BlockSpec((tm, tn), lambda i, j, k: (i, j)),
        out_shape=jax.ShapeDtypeStruct((M, N), out_dtype),
        scratch_shapes=[pltpu.VMEM((tm, tn), F32)],
        compiler_params=_params(("parallel", "parallel", "arbitrary"), est),
    )(*args)


def _rms(x, g):
    r = lax.rsqrt(jnp.mean(x * x, axis=-1, keepdims=True) + EPS)
    return x * r * g


def _rms_bwd(x, g, dout):
    r = lax.rsqrt(jnp.mean(x * x, axis=-1, keepdims=True) + EPS)
    xhat = x * r
    dxh = dout * g
    dx = r * (dxh - xhat * jnp.mean(dxh * xhat, axis=-1, keepdims=True))
    return dx, dout * xhat


def _row_tile(L):
    return _divisor(L, (272, 256, 128, 64, 32, 16))


def _resnorm_fwd(h, y, g_post, g_pre, name):
    L, D = h.shape
    bt = _row_tile(L)
    has_y, has_pre = y is not None, g_pre is not None
    row = pl.BlockSpec((bt, D), lambda i: (i, 0))
    vec = pl.BlockSpec((1, D), lambda i: (0, 0))
    args, in_specs = [h], [row]
    if has_y:
        args += [y, g_post.reshape(1, D)]
        in_specs += [row, vec]
    if has_pre:
        args += [g_pre.reshape(1, D)]
        in_specs += [vec]
    out_shape, out_specs = [], []
    if has_y:
        out_shape.append(jax.ShapeDtypeStruct((L, D), F32))
        out_specs.append(row)
    if has_pre:
        out_shape.append(jax.ShapeDtypeStruct((L, D), BF16))
        out_specs.append(row)

    def body(*refs):
        refs = list(refs)
        h_ref = refs.pop(0)
        hn = h_ref[...]
        if has_y:
            y_ref, gp_ref = refs.pop(0), refs.pop(0)
            hn = hn + _rms(y_ref[...], gp_ref[...])
        if has_pre:
            g_ref = refs.pop(0)
        if has_y:
            refs.pop(0)[...] = hn
        if has_pre:
            refs.pop(0)[...] = _rms(hn, g_ref[...]).astype(BF16)

    outs = pl.pallas_call(
        body, name=name, grid=(L // bt,), in_specs=in_specs, out_specs=out_specs, out_shape=out_shape,
        compiler_params=_params(("parallel",), 10 * bt * D * 4),
    )(*args)
    outs = list(outs)
    h_new = outs.pop(0) if has_y else h
    u = outs.pop(0) if has_pre else None
    return h_new, u


def _resnorm_bwd(dh_direct, du, h_new, y, g_post, g_pre, name):
    L, D = dh_direct.shape
    bt = _row_tile(L)
    has_y, has_pre = y is not None, du is not None
    row = pl.BlockSpec((bt, D), lambda i: (i, 0))
    vec = pl.BlockSpec((1, D), lambda i: (0, 0))
    acc = pl.BlockSpec((8, D), lambda i: (0, 0))
    args, in_specs = [dh_direct], [row]
    if has_pre:
        args += [du, h_new, g_pre.reshape(1, D)]
        in_specs += [row, row, vec]
    if has_y:
        args += [y, g_post.reshape(1, D)]
        in_specs += [row, vec]
    out_shape, out_specs = [], []
    if has_pre:
        out_shape += [jax.ShapeDtypeStruct((L, D), F32), jax.ShapeDtypeStruct((8, D), F32)]
        out_specs += [row, acc]
    if has_y:
        out_shape += [jax.ShapeDtypeStruct((L, D), BF16), jax.ShapeDtypeStruct((8, D), F32)]
        out_specs += [row, acc]

    def colsum8(v):
        return jnp.sum(v.reshape(bt // 8, 8, D), axis=0)

    def body(*refs):
        refs = list(refs)
        first = pl.program_id(0) == 0
        dh = refs.pop(0)[...]
        if has_pre:
            du_ref, hn_ref, g_ref = refs.pop(0), refs.pop(0), refs.pop(0)
        if has_y:
            y_ref, gp_ref = refs.pop(0), refs.pop(0)
        if has_pre:
            dh_ref, dgpre_ref = refs.pop(0), refs.pop(0)
            dx, dgp = _rms_bwd(hn_ref[...], g_ref[...], du_ref[...].astype(F32))
            dh = dh + dx
            dh_ref[...] = dh

            @pl.when(first)
            def _():
                dgpre_ref[...] = jnp.zeros_like(dgpre_ref)
            dgpre_ref[...] += colsum8(dgp)
        if has_y:
            dy_ref, dgpost_ref = refs.pop(0), refs.pop(0)
            dy, dgq = _rms_bwd(y_ref[...], gp_ref[...], dh)
            dy_ref[...] = dy.astype(BF16)

            @pl.when(first)
            def _():
                dgpost_ref[...] = jnp.zeros_like(dgpost_ref)
            dgpost_ref[...] += colsum8(dgq)

    outs = list(pl.pallas_call(
        body, name=name, grid=(L // bt,), in_specs=in_specs, out_specs=out_specs, out_shape=out_shape,
        compiler_params=_params(("arbitrary",), 14 * bt * D * 4),
    )(*args))
    dh, dg_pre, dy, dg_post = dh_direct, None, None, None
    if has_pre:
        dh, dg_pre = outs.pop(0), outs.pop(0).sum(0)
    if has_y:
        dy, dg_post = outs.pop(0), outs.pop(0).sum(0)
    return dh, dy, dg_post, dg_pre


def _loss(h, tgt, n_meta, seq, name):
    L, D = h.shape
    bt = _row_tile(L)
    row = pl.BlockSpec((bt, D), lambda i: (i, 0))

    def body(h_ref, t_ref, dy_ref, loss_ref):
        i = pl.program_id(0)
        r = i * bt + lax.broadcasted_iota(jnp.int32, (bt, 1), 0)
        valid = (r >= n_meta) & (r < n_meta + seq)
        e = jnp.where(valid, h_ref[...] - t_ref[...], 0.0)
        dy_ref[...] = e * (1.0 / D)

        @pl.when(i == 0)
        def _():
            loss_ref[...] = jnp.zeros_like(loss_ref)
        loss_ref[...] += 0.5 * jnp.sum(jnp.sum(e * e, axis=-1, keepdims=True) * (1.0 / D))

    dy, loss = pl.pallas_call(
        body, name=name, grid=(L // bt,), in_specs=[row, row],
        out_specs=[row, pl.BlockSpec((8, LANES), lambda i: (0, 0))],
        out_shape=[jax.ShapeDtypeStruct((L, D), F32), jax.ShapeDtypeStruct((8, LANES), F32)],
        compiler_params=_params(("arbitrary",), 8 * bt * D * 4),
    )(h, tgt)
    return loss[0, 0], dy


def _split3(x):
    x1 = x.astype(BF16)
    r1 = x - x1.astype(F32)
    x2 = r1.astype(BF16)
    x3 = (r1 - x2.astype(F32)).astype(BF16)
    return x1, x2, x3


def _tri_dot3(tri, x):
    x1, x2, x3 = _split3(x)
    d = functools.partial(jnp.dot, preferred_element_type=F32)
    return d(tri, x1) + d(tri, x2) + d(tri, x3)


def _gate_fwd(fl, b_pad, name):
    L = fl.shape[0]
    bt = 128
    blk = pl.BlockSpec((bt, LANES), lambda i: (i, 0))

    def body(fl_ref, b_ref, c_ref, carry):
        @pl.when(pl.program_id(0) == 0)
        def _():
            carry[...] = jnp.zeros_like(carry)
        x = fl_ref[...] + b_ref[...]
        lf = jnp.minimum(x, 0.0) - jnp.log(1.0 + jnp.exp(-jnp.abs(x)))
        r = lax.broadcasted_iota(jnp.int32, (bt, bt), 0)
        s = lax.broadcasted_iota(jnp.int32, (bt, bt), 1)
        tri = (s <= r).astype(BF16)
        c = _tri_dot3(tri, lf) + carry[...]
        c_ref[...] = c
        carry[...] = c[bt - 1:bt, :]

    return pl.pallas_call(
        body, name=name, grid=(L // bt,), in_specs=[blk, pl.BlockSpec((1, LANES), lambda i: (0, 0))],
        out_specs=blk, out_shape=jax.ShapeDtypeStruct((L, LANES), F32),
        scratch_shapes=[pltpu.VMEM((1, LANES), F32)],
        compiler_params=_params(("arbitrary",), 1 << 20),
    )(fl, b_pad)


def _gate_bwd(dc, fl, b_pad, name):
    L = fl.shape[0]
    bt = 128
    n = L // bt
    blk = pl.BlockSpec((bt, LANES), lambda i: (n - 1 - i, 0))

    def body(dc_ref, fl_ref, b_ref, dfl_ref, db_ref, carry):
        @pl.when(pl.program_id(0) == 0)
        def _():
            carry[...] = jnp.zeros_like(carry)
            db_ref[...] = jnp.zeros_like(db_ref)
        r = lax.broadcasted_iota(jnp.int32, (bt, bt), 0)
        s = lax.broadcasted_iota(jnp.int32, (bt, bt), 1)
        tri = (s >= r).astype(BF16)
        dlf = _tri_dot3(tri, dc_ref[...]) + carry[...]
        carry[...] = dlf[0:1, :]
        x = fl_ref[...] + b_ref[...]
        dfl = dlf / (1.0 + jnp.exp(x))
        dfl_ref[...] = dfl.astype(BF16)
        db_ref[...] += jnp.sum(dfl.reshape(bt // 8, 8, LANES), axis=0)

    dfl, db = pl.pallas_call(
        body, name=name, grid=(n,), in_specs=[blk, blk, pl.BlockSpec((1, LANES), lambda i: (0, 0))],
        out_specs=[blk, pl.BlockSpec((8, LANES), lambda i: (0, 0))],
        out_shape=[jax.ShapeDtypeStruct((L, LANES), BF16), jax.ShapeDtypeStruct((8, LANES), F32)],
        scratch_shapes=[pltpu.VMEM((1, LANES), F32)],
        compiler_params=_params(("arbitrary",), 1 << 20),
    )(dc, fl, b_pad)
    return dfl, db.sum(0)


_MESH = pl.DeviceIdType.MESH
_ANY = pl.BlockSpec(memory_space=pl.ANY)


def _flip(v, bit):
    return 1 - v if bit else v


class _GatherRider:
    def __init__(self, blocks):
        self.blocks = list(blocks)
        self.n = len(self.blocks)
        self.in_specs = [_ANY] * self.n
        self.out_specs = [_ANY] * self.n
        self.out_shape = [jax.ShapeDtypeStruct((N_DEV, *b.shape), b.dtype) for b in self.blocks]
        self.scratch = [pltpu.SemaphoreType.DMA((7 * self.n,)), pltpu.SemaphoreType.DMA((7 * self.n,)),
                        pltpu.SemaphoreType.DMA((self.n,))]

    def _copies(self, a, x_ref, out_ref, send_sems, recv_sems):
        mx, my, mc = lax.axis_index("x"), lax.axis_index("y"), lax.axis_index("c")
        me, sibling = (mx, my, mc), (mx, my, 1 - mc)
        chips = [(1 - mx, my), (mx, 1 - my), (1 - mx, 1 - my)]

        def slot(px, py, pc):
            return out_ref.at[4 * px + 2 * py + pc]

        def copy(k, block, to, src=None):
            return pltpu.make_async_remote_copy(
                src_ref=slot(*block) if src is None else src, dst_ref=slot(*block),
                send_sem=send_sems.at[7 * a + k], recv_sem=recv_sems.at[7 * a + k], device_id=to, device_id_type=_MESH)

        first = [copy(0, me, sibling, src=x_ref)] + [copy(1 + j, me, (*chip, mc), src=x_ref) for j, chip in enumerate(chips)]
        landed = [copy(1 + j, (*chip, mc), me) for j, chip in enumerate(chips)]
        passed = [copy(4 + j, (*chip, mc), sibling) for j, chip in enumerate(chips)]
        last = [copy(0, sibling, me)] + [copy(4 + j, (*chip, 1 - mc), me) for j, chip in enumerate(chips)]
        return slot(*me), first, landed, passed, last

    def start(self, ins, outs, send_sems, recv_sems, local_sems):
        for a in range(self.n):
            mine, first, _, _, _ = self._copies(a, ins[a], outs[a], send_sems, recv_sems)
            pltpu.make_async_copy(ins[a], mine, local_sems.at[a]).start()
            for cp in first:
                cp.start()

    def middle(self, ins, outs, send_sems, recv_sems, local_sems):
        for a in range(self.n):
            _, _, landed, passed, _ = self._copies(a, ins[a], outs[a], send_sems, recv_sems)
            for arrived, onward in zip(landed, passed):
                arrived.wait_recv()
                onward.start()

    def finish(self, ins, outs, send_sems, recv_sems, local_sems):
        for a in range(self.n):
            mine, first, _, passed, last = self._copies(a, ins[a], outs[a], send_sems, recv_sems)
            for cp in last:
                cp.wait_recv()
            for cp in first + passed:
                cp.wait_send()
            pltpu.make_async_copy(ins[a], mine, local_sems.at[a]).wait()


class _ExchangeRider:
    def __init__(self, arrays):
        self.blocks = list(arrays)
        self.n = len(self.blocks)
        self.in_specs = [_ANY] * self.n
        self.out_specs = [_ANY] * self.n
        self.out_shape = [jax.ShapeDtypeStruct(b.shape, b.dtype) for b in self.blocks]
        self.scratch = [pltpu.SemaphoreType.DMA((7 * self.n,)), pltpu.SemaphoreType.DMA((7 * self.n,)),
                        pltpu.SemaphoreType.DMA((self.n,))]

    def _copies(self, a, g_ref, land_ref, send_sems, recv_sems):
        mx, my, mc = lax.axis_index("x"), lax.axis_index("y"), lax.axis_index("c")
        me = 4 * mx + 2 * my + mc
        sends, recvs = [], []
        for k in range(1, N_DEV):
            peer = (_flip(mx, k & 4), _flip(my, k & 2), _flip(mc, k & 1))
            p = 4 * peer[0] + 2 * peer[1] + peer[2]
            sems = dict(send_sem=send_sems.at[7 * a + k - 1], recv_sem=recv_sems.at[7 * a + k - 1], device_id=peer, device_id_type=_MESH)
            sends.append(pltpu.make_async_remote_copy(src_ref=g_ref.at[p], dst_ref=land_ref.at[me], **sems))
            recvs.append(pltpu.make_async_remote_copy(src_ref=g_ref.at[p], dst_ref=land_ref.at[p], **sems))
        return g_ref.at[me], land_ref.at[me], sends, recvs

    def start(self, ins, outs, send_sems, recv_sems, local_sems):
        for a in range(self.n):
            src, dst, sends, _ = self._copies(a, ins[a], outs[a], send_sems, recv_sems)
            pltpu.make_async_copy(src, dst, local_sems.at[a]).start()
            for cp in sends:
                cp.start()

    def middle(self, ins, outs, send_sems, recv_sems, local_sems):
        pass

    def finish(self, ins, outs, send_sems, recv_sems, local_sems):
        for a in range(self.n):
            src, dst, sends, recvs = self._copies(a, ins[a], outs[a], send_sems, recv_sems)
            for cp in recvs:
                cp.wait_recv()
            for cp in sends:
                cp.wait_send()
            pltpu.make_async_copy(src, dst, local_sems.at[a]).wait()


class _NoRider:
    blocks, in_specs, out_specs, out_shape, scratch = [], [], [], [], []


_NO_RIDER = _NoRider()


def _ride(rider, refs, n_in, n_out, first, middle, last, work):
    if rider is None:
        return work(*refs)
    r = rider.n
    own = refs[:n_in] + refs[n_in + r:n_in + r + n_out] + refs[n_in + r + n_out + r:len(refs) - 3]
    args = (refs[n_in:n_in + r], refs[n_in + r + n_out:n_in + r + n_out + r], *refs[len(refs) - 3:])
    pl.when(first)(lambda: rider.start(*args))
    pl.when(middle)(lambda: rider.middle(*args))
    work(*own)
    pl.when(last)(lambda: rider.finish(*args))


def _split2(x):
    hi = x.astype(BF16)
    return hi, (x - hi.astype(F32)).astype(BF16)


def _dot_hi_lo(x, w2):
    hi, lo = _split2(x)
    return jnp.dot(jnp.concatenate([hi, lo], axis=1), w2, preferred_element_type=F32)


def _suffix_sums(x, tri2, exact):
    if exact:
        return _dot_hi_lo(x, tri2)
    return jnp.dot(x.astype(BF16), tri2[:x.shape[1]], preferred_element_type=F32)


def _dot_nt(a, b):
    return lax.dot_general(a, b, _DN["nt"], preferred_element_type=F32)


def _suffix_matrix(bk):
    j = lax.broadcasted_iota(jnp.int32, (2 * bk, bk), 0) % bk
    s = lax.broadcasted_iota(jnp.int32, (2 * bk, bk), 1)
    return (j >= s).astype(BF16)


LOG2E = 1.4426950408889634


def _sb_window(q, k, t_pos, ks, bk, scale2, carry_c, tri, masked):
    width = k.shape[0]
    z = _dot_nt(q, k) * scale2
    sp = jnp.maximum(z, 0.0) + jnp.log2(1.0 + jnp.exp2(-jnp.abs(z)))
    if masked:
        mask = ks + lax.broadcasted_iota(jnp.int32, (1, width), 1) < t_pos
        lkm = jnp.where(mask, -sp, 0.0)
    else:
        mask, lkm = None, -sp
    sums, run = _window_suffix_sums(lkm, bk, tri, carry_c, False)
    a = jnp.exp2(z + sums)
    if masked:
        a = jnp.where(mask, a, 0.0)
    return z, sp, mask, a, run


def _window_suffix_sums(x, bk, tri, carry, exact):
    nb = x.shape[1] // bk
    parts, run = [None] * nb, carry
    for b in reversed(range(nb)):
        cs = _suffix_sums(x[:, b * bk:(b + 1) * bk], tri, exact)
        parts[b] = run + cs
        run = run + cs[:, 0:1]
    return (parts[0] if nb == 1 else jnp.concatenate(parts, axis=1)), run


KEY_WINDOW = 4


def _key_tiles(i, bk, step, carry):
    r = i % KEY_WINDOW

    def first(w):
        return lambda c: step(pl.multiple_of((i - w) * bk, bk), (w + 1) * bk, c, True)

    carry = lax.switch(r, [first(w) for w in range(KEY_WINDOW)], carry)
    top = i - r
    return lax.fori_loop(
        0, top // KEY_WINDOW,
        lambda p, c: step(pl.multiple_of((top - KEY_WINDOW * (p + 1)) * bk, bk), KEY_WINDOW * bk, c, False), carry)


def _attn_specs(H, L, bq):
    W3 = 3 * H

    def col(role):
        return lambda h, i: (h // H) * W3 + role * H + h % H

    q_spec = pl.BlockSpec((bq, HEAD_DIM), lambda h, i: (i, col(0)(h, i)))
    k_spec = pl.BlockSpec((L, HEAD_DIM), lambda h, i: (0, col(1)(h, i)))
    v_spec = pl.BlockSpec((L, HEAD_DIM), lambda h, i: (0, col(2)(h, i)))
    crow_spec = pl.BlockSpec((None, 1, L), lambda h, i: (jnp.maximum(h - H, 0), 0, 0))
    ccol_spec = pl.BlockSpec((None, bq, 1), lambda h, i: (jnp.maximum(h - H, 0), i, 0))
    g_spec = pl.BlockSpec((None, 1, HEAD_DIM), lambda h, i: (h, 0, 0))
    tile = pl.BlockSpec((bq, HEAD_DIM), lambda h, i: (i, h))
    stat = pl.BlockSpec((None, bq, 1), lambda h, i: (h, i, 0))
    return q_spec, k_spec, v_spec, crow_spec, ccol_spec, g_spec, tile, stat


def _attn_fwd(qkv, crow, ccol, g_heads, H, name, rider=None):
    L = qkv.shape[0]
    bq = bk = min(ATT_BLOCK, L)
    nq = L // bq
    scale = HEAD_DIM ** -0.5
    scale2 = scale * LOG2E
    q_spec, k_spec, v_spec, crow_spec, ccol_spec, g_spec, tile, stat = _attn_specs(H, L, bq)

    def work(q_ref, k_ref, v_ref, crow_ref, ccol_ref, g_ref, o_ref, on_ref, lse_ref):
        h, i = pl.program_id(0), pl.program_id(1)
        q = q_ref[...]
        t_pos = i * bq + lax.broadcasted_iota(jnp.int32, (bq, 1), 0)

        def finish(o):
            o_ref[...] = o
            on_ref[...] = _rms(o, g_ref[...]).astype(BF16)

        @pl.when(h < H)
        def _stick_breaking():
            tri = _suffix_matrix(bk)

            def step(ks, rows, carry, masked):
                c, acc = carry
                k = k_ref[pl.ds(ks, rows), :]
                v = v_ref[pl.ds(ks, rows), :]
                _, _, _, a, c = _sb_window(q, k, t_pos, ks, bk, scale2, c, tri, masked)
                acc = acc + jnp.dot(a.astype(BF16), v, preferred_element_type=F32)
                return c, acc

            _, acc = _key_tiles(i, bk, step, (jnp.zeros((bq, 1), F32), jnp.zeros((bq, HEAD_DIM), F32)))
            finish(acc)
            lse_ref[...] = jnp.zeros_like(lse_ref)

        @pl.when(h >= H)
        def _forgetting():
            cq = ccol_ref[...] * LOG2E

            def step(ks, rows, carry, masked):
                m, l, acc = carry
                k = k_ref[pl.ds(ks, rows), :]
                v = v_ref[pl.ds(ks, rows), :]
                s = _dot_nt(q, k) * scale2 + (cq - crow_ref[:, pl.ds(ks, rows)] * LOG2E)
                if masked:
                    s = jnp.where(ks + lax.broadcasted_iota(jnp.int32, (1, rows), 1) <= t_pos, s, NEG_INF)
                m_new = jnp.maximum(m, jnp.max(s, axis=-1, keepdims=True))
                alpha = jnp.exp2(m - m_new)
                p = jnp.exp2(s - m_new)
                l = alpha * l + jnp.sum(p, axis=-1, keepdims=True)
                acc = alpha * acc + _dot_hi_lo(p, jnp.concatenate([v, v], axis=0))
                return m_new, l, acc

            init = (jnp.full((bq, 1), NEG_INF, F32), jnp.zeros((bq, 1), F32), jnp.zeros((bq, HEAD_DIM), F32))
            m, l, acc = _key_tiles(i, bk, step, init)
            finish(acc / l)
            lse_ref[...] = m + jnp.log2(l)

    def body(*refs):
        h, i = pl.program_id(0), pl.program_id(1)
        _ride(rider, refs, 6, 3, (h == 0) & (i == 0), (h == H) & (i == 0), (h == 2 * H - 1) & (i == nq - 1), work)

    W2 = 2 * H * HEAD_DIM
    est = 4 * L * HEAD_DIM * 2 + 32 * bq * bk * 4 + (4 << 20)
    extra = rider or _NO_RIDER
    outs = pl.pallas_call(
        body, name=name, grid=(2 * H, nq),
        in_specs=[q_spec, k_spec, v_spec, crow_spec, ccol_spec, g_spec] + extra.in_specs,
        out_specs=[tile, tile, stat] + extra.out_specs,
        out_shape=[jax.ShapeDtypeStruct((L, W2), F32), jax.ShapeDtypeStruct((L, W2), BF16),
                   jax.ShapeDtypeStruct((2 * H, L, 1), F32)] + extra.out_shape,
        scratch_shapes=extra.scratch,
        compiler_params=_params(("arbitrary", "arbitrary"), est),
    )(qkv, qkv, qkv, crow, ccol, g_heads, *extra.blocks)
    return outs[0], outs[1], outs[2], list(outs[3:])


def _attn_bwd(qkv, crow, ccol, g_heads, o, lse, d_on, H, name, rider=None):
    L = qkv.shape[0]
    bq = bk = min(ATT_BLOCK, L)
    nq = L // bq
    scale = HEAD_DIM ** -0.5
    scale2 = scale * LOG2E
    q_spec, k_spec, v_spec, crow_spec, ccol_spec, g_spec, tile, stat = _attn_specs(H, L, bq)
    full = pl.BlockSpec((L, HEAD_DIM), lambda h, i: (0, h))
    dg_spec = pl.BlockSpec((None, 1, HEAD_DIM), lambda h, i: (h, 0, 0))
    dc_spec = pl.BlockSpec((None, 1, L), lambda h, i: (h, 0, 0))

    def work(q_ref, k_ref, v_ref, crow_ref, ccol_ref, g_ref, o_ref, lse_ref, don_ref,
             dq_ref, dk_ref, dv_ref, dg_ref, dc_ref, dkt_acc, dvt_acc):
        h, i = pl.program_id(0), pl.program_id(1)

        @pl.when(i == 0)
        def _():
            dkt_acc[...] = jnp.zeros_like(dkt_acc)
            dvt_acc[...] = jnp.zeros_like(dvt_acc)
            dg_ref[...] = jnp.zeros_like(dg_ref)
            dc_ref[...] = jnp.zeros_like(dc_ref)

        q = q_ref[...]
        t_pos = i * bq + lax.broadcasted_iota(jnp.int32, (bq, 1), 0)
        o_t = o_ref[...]
        d_o, dg = _rms_bwd(o_t, g_ref[...], don_ref[...])
        dg_ref[...] += jnp.sum(dg, axis=0, keepdims=True)
        d_ob = d_o.astype(BF16)
        dsum = jnp.sum(d_ob.astype(F32) * o_t, axis=-1, keepdims=True)
        q_t = q.astype(F32).T.astype(BF16)
        d_obt = d_o.T.astype(BF16)

        @pl.when(h < H)
        def _stick_breaking():
            tri = _suffix_matrix(bk)

            def step(ks, rows, carry, masked):
                c, gs, dq = carry
                k = k_ref[pl.ds(ks, rows), :]
                v = v_ref[pl.ds(ks, rows), :]
                z, sp, mask, a, c = _sb_window(q, k, t_pos, ks, bk, scale2, c, tri, masked)
                a_b = a.astype(BF16)
                g_w = a_b.astype(F32) * _dot_nt(d_ob, v)
                later, gs = _window_suffix_sums(g_w, bk, tri, gs, True)
                prefix = dsum - (later - g_w)
                dz = (g_w - jnp.exp2(z - sp) * prefix) * scale
                if masked:
                    dz = jnp.where(mask, dz, 0.0)
                dzb = dz.astype(BF16)
                dq = dq + jnp.dot(dzb, k, preferred_element_type=F32)
                dkt_acc[:, pl.ds(ks, rows)] += jnp.dot(q_t, dzb, preferred_element_type=F32)
                dvt_acc[:, pl.ds(ks, rows)] += jnp.dot(d_obt, a_b, preferred_element_type=F32)
                return c, gs, dq

            z1 = jnp.zeros((bq, 1), F32)
            _, _, dq = _key_tiles(i, bk, step, (z1, z1, jnp.zeros((bq, HEAD_DIM), F32)))
            dq_ref[...] = dq.astype(BF16)

        @pl.when(h >= H)
        def _forgetting():
            cq = ccol_ref[...] * LOG2E - lse_ref[...]

            def step(ks, rows, dq, masked):
                k = k_ref[pl.ds(ks, rows), :]
                v = v_ref[pl.ds(ks, rows), :]
                p = jnp.exp2(_dot_nt(q, k) * scale2 + (cq - crow_ref[:, pl.ds(ks, rows)] * LOG2E))
                if masked:
                    p = jnp.where(ks + lax.broadcasted_iota(jnp.int32, (1, rows), 1) <= t_pos, p, 0.0)
                ds = p * (_dot_nt(d_ob, v) - dsum)
                dsb = (ds * scale).astype(BF16)
                dq = dq + jnp.dot(dsb, k, preferred_element_type=F32)
                dkt_acc[:, pl.ds(ks, rows)] += jnp.dot(q_t, dsb, preferred_element_type=F32)
                dvt_acc[:, pl.ds(ks, rows)] += jnp.dot(d_obt, p.astype(BF16), preferred_element_type=F32)
                dc_ref[:, pl.ds(ks, rows)] += -jnp.sum(ds, axis=0, keepdims=True)
                return dq

            dq = _key_tiles(i, bk, step, jnp.zeros((bq, HEAD_DIM), F32))
            dq_ref[...] = dq.astype(BF16)

        @pl.when(i == nq - 1)
        def _():
            dk_ref[...] = dkt_acc[...].T.astype(BF16)
            dv_ref[...] = dvt_acc[...].T.astype(BF16)

    W2 = 2 * H * HEAD_DIM
    est = 4 * L * HEAD_DIM * 2 + 4 * L * HEAD_DIM * 2 + 2 * L * HEAD_DIM * 4 + 48 * bq * bk * 4 + (4 << 20)
    def body(*refs):
        h, i = pl.program_id(0), pl.program_id(1)
        _ride(rider, refs, 9, 5, (h == 0) & (i == 0), (h == H) & (i == 0), (h == 2 * H - 1) & (i == nq - 1), work)

    extra = rider or _NO_RIDER
    dq, dk, dv, dg, dc, *brought = pl.pallas_call(
        body, name=name, grid=(2 * H, nq),
        in_specs=[q_spec, k_spec, v_spec, crow_spec, ccol_spec, g_spec, tile, stat, tile] + extra.in_specs,
        out_specs=[tile, full, full, dg_spec, dc_spec] + extra.out_specs,
        out_shape=[jax.ShapeDtypeStruct((L, W2), BF16)] * 3
        + [jax.ShapeDtypeStruct((2 * H, 1, HEAD_DIM), F32), jax.ShapeDtypeStruct((2 * H, 1, L), F32)] + extra.out_shape,
        scratch_shapes=[pltpu.VMEM((HEAD_DIM, L), F32), pltpu.VMEM((HEAD_DIM, L), F32)] + extra.scratch,
        compiler_params=_params(("arbitrary", "arbitrary"), est),
    )(qkv, qkv, qkv, crow, ccol, g_heads, o, lse, d_on, *extra.blocks)
    return dq, dk, dv, dg[:, 0, :], dc, brought


HALO = 16


def _conv_tiles(L, F):
    return _row_tile(L), _divisor(F, (512, 256, 128))


SUBLANES = 8


def _shift_down(x, before, bt):
    ext = jnp.concatenate([before, x], axis=0)
    return pltpu.roll(ext, 1, 0)[SUBLANES:], pltpu.roll(ext, 2, 0)[SUBLANES:]


def _shift_up(x, after, bt):
    ext = jnp.concatenate([x, after], axis=0)
    return pltpu.roll(ext, bt + SUBLANES - 1, 0)[:bt], pltpu.roll(ext, bt + SUBLANES - 2, 0)[:bt]


def _conv_rows(p_ref, halo_ref, w_ref, b_ref, first, bt):
    p = p_ref[...].astype(F32)
    before = jnp.where(first, 0.0, halo_ref[...].astype(F32)[HALO - SUBLANES:HALO, :])
    p1, p2 = _shift_down(p, before, bt)
    w = w_ref[...]
    a = w[0:1, :] * p2 + w[1:2, :] * p1 + w[2:3, :] * p + b_ref[...]
    return a, p, p1, p2


def _sigmoid(x):
    return 0.5 + 0.5 * jnp.tanh(0.5 * x)


def _conv_in_specs(L, F, bt, bc, order):
    nf = F // bc
    r = bt // HALO
    ix = (lambda a, b: (a, b)) if order == "ij" else (lambda a, b: (b, a))

    def mk(shape, fn):
        return pl.BlockSpec(shape, lambda a, b: fn(*ix(a, b)))

    return [
        mk((bt, bc), lambda i, j: (i, j)), mk((HALO, bc), lambda i, j: (jnp.maximum(i * r - 1, 0), j)),
        mk((bt, bc), lambda i, j: (i, nf + j)), mk((HALO, bc), lambda i, j: (jnp.maximum(i * r - 1, 0), nf + j)),
        mk((3, bc), lambda i, j: (0, j)), mk((3, bc), lambda i, j: (0, nf + j)),
        mk((1, bc), lambda i, j: (0, j)), mk((1, bc), lambda i, j: (0, nf + j)),
    ]


def _convgate_fwd(p, conv_w, conv_b, name):
    L, F2 = p.shape
    F = F2 // 2
    bt, bc = _conv_tiles(L, F)

    def body(pg, hg, pu, hu, wg, wu, bg, bu, act_ref):
        first = pl.program_id(0) == 0
        ag = _conv_rows(pg, hg, wg, bg, first, bt)[0]
        au = _conv_rows(pu, hu, wu, bu, first, bt)[0]
        act_ref[...] = (ag * _sigmoid(ag) * au).astype(BF16)

    return pl.pallas_call(
        body, name=name, grid=(L // bt, F // bc), in_specs=_conv_in_specs(L, F, bt, bc, "ij"),
        out_specs=pl.BlockSpec((bt, bc), lambda i, j: (i, j)), out_shape=jax.ShapeDtypeStruct((L, F), BF16),
        compiler_params=_params(("parallel", "parallel"), 24 * bt * bc * 4),
    )(p, p, p, p, conv_w, conv_w, conv_b.reshape(1, F2), conv_b.reshape(1, F2))


def _convgate_bwd(p, conv_w, conv_b, d_act, name):
    L, F2 = p.shape
    F = F2 // 2
    bt, bc = _conv_tiles(L, F)

    def body(pg, hg, pu, hu, wg, wu, bg, bu, dact_ref, da_ref, dwb_ref):
        first = pl.program_id(1) == 0
        ag, xg, xg1, xg2 = _conv_rows(pg, hg, wg, bg, first, bt)
        au, xu, xu1, xu2 = _conv_rows(pu, hu, wu, bu, first, bt)
        d_act = dact_ref[...].astype(F32)
        sg = _sigmoid(ag)
        dag = d_act * au * sg * (1.0 + ag * (1.0 - sg))
        dau = d_act * ag * sg
        da_ref[0] = dag.astype(BF16)
        da_ref[1] = dau.astype(BF16)

        @pl.when(first)
        def _():
            dwb_ref[...] = jnp.zeros_like(dwb_ref)
        cs = lambda v: jnp.sum(v, axis=0, keepdims=True)
        dwb_ref[0] += jnp.concatenate([cs(dag * xg2), cs(dag * xg1), cs(dag * xg), cs(dag)], axis=0)
        dwb_ref[1] += jnp.concatenate([cs(dau * xu2), cs(dau * xu1), cs(dau * xu), cs(dau)], axis=0)

    da, dwb = pl.pallas_call(
        body, name=name, grid=(F // bc, L // bt),
        in_specs=_conv_in_specs(L, F, bt, bc, "ji") + [pl.BlockSpec((bt, bc), lambda j, i: (i, j))],
        out_specs=[pl.BlockSpec((2, bt, bc), lambda j, i: (0, i, j)), pl.BlockSpec((2, 4, bc), lambda j, i: (0, 0, j))],
        out_shape=[jax.ShapeDtypeStruct((2, L, F), BF16), jax.ShapeDtypeStruct((2, 4, F), F32)],
        compiler_params=_params(("parallel", "arbitrary"), 40 * bt * bc * 4),
    )(p, p, p, p, conv_w, conv_w, conv_b.reshape(1, F2), conv_b.reshape(1, F2), d_act)
    d_w = jnp.concatenate([dwb[0, 0:3], dwb[1, 0:3]], axis=1)
    d_b = jnp.concatenate([dwb[0, 3], dwb[1, 3]], axis=0)
    return da, d_w, d_b


def _conv_bwd_data(da, conv_w, name):
    _, L, F = da.shape
    bt, bc = _conv_tiles(L, F)
    nf, r, nt = F // bc, bt // HALO, L // bt

    def body(da_ref, nxt_ref, w_ref, dp_ref):
        last = pl.program_id(0) == nt - 1
        x = da_ref[...].astype(F32)
        after = jnp.where(last, 0.0, nxt_ref[...].astype(F32)[0:SUBLANES, :])
        x1, x2 = _shift_up(x, after, bt)
        w = w_ref[...]
        dp_ref[...] = (w[2:3, :] * x + w[1:2, :] * x1 + w[0:1, :] * x2).astype(BF16)

    return pl.pallas_call(
        body, name=name, grid=(nt, 2 * nf),
        in_specs=[pl.BlockSpec((None, bt, bc), lambda i, j: (j // nf, i, j % nf)),
                  pl.BlockSpec((None, HALO, bc), lambda i, j: (j // nf, jnp.minimum((i + 1) * r, nt * r - 1), j % nf)),
                  pl.BlockSpec((3, bc), lambda i, j: (0, j))],
        out_specs=pl.BlockSpec((bt, bc), lambda i, j: (i, j)), out_shape=jax.ShapeDtypeStruct((L, 2 * F), BF16),
        compiler_params=_params(("parallel", "parallel"), 16 * bt * bc * 4),
    )(da, da, conv_w)


def _pad_cols(a, n):
    return jnp.pad(a, ((0, 0), (0, n - a.shape[1])))


def _local_step(x, tgt, meta, W, hooks=None):
    S, D = x.shape
    n_meta = meta.shape[0]
    depth = len(W["w_in"])
    H = D // (2 * HEAD_DIM)
    WQ = 6 * H * HEAD_DIM
    L = -(-(S + n_meta) // ATT_BLOCK) * ATT_BLOCK
    tail = L - S - n_meta
    zeros_tail = jnp.zeros((tail, D), F32)
    h = jnp.concatenate([meta, x, zeros_tail], axis=0)
    tgt_p = jnp.concatenate([jnp.zeros((n_meta, D), F32), tgt, zeros_tail], axis=0)

    saved = []
    _, u1 = _resnorm_fwd(h, None, None, W["g_mix_pre"][0], "prenorm0")
    for l in range(depth):
        w_in = W["w_in"][l]
        w_qkv, w_f = w_in[:, :WQ], _pad_cols(w_in[:, WQ:], LANES)
        b_pad = jnp.pad(W["b_f"][l], (0, LANES - H)).reshape(1, LANES)
        g_heads = jnp.concatenate([W["g_sb"][l], W["g_fox"][l]], axis=0).reshape(2 * H, 1, HEAD_DIM)
        qkv = _mm(u1, w_qkv, "nn", BF16, f"qkv{l}")
        fl = _mm(u1, w_f, "nn", F32, f"flogit{l}")
        c = _gate_fwd(fl, b_pad, f"gate_fwd{l}")
        c_heads = c[:, :H].T
        crow, ccol = c_heads[:, None, :], c_heads[:, :, None]
        o, on, lse, brought = _attn_fwd(qkv, crow, ccol, g_heads, H, f"attn_fwd{l}", hooks and hooks.fwd_rider(l))
        if hooks:
            hooks.fwd_done(l, brought, W)
        mix = _mm(on, W["w_out"][l], "nn", F32, f"mix{l}")
        h_mid, u2 = _resnorm_fwd(h, mix, W["g_mix_post"][l], W["g_ffn_pre"][l], f"resnorm_a{l}")
        p = _mm(u2, W["w_up"][l], "nn", BF16, f"up{l}")
        act = _convgate_fwd(p, W["conv_w"][l], W["conv_b"][l], f"convgate{l}")
        ff = _mm(act, W["w_down"][l], "nn", F32, f"down{l}")
        g_next = W["g_mix_pre"][l + 1] if l + 1 < depth else None
        h_out, u1_next = _resnorm_fwd(h_mid, ff, W["g_ffn_post"][l], g_next, f"resnorm_b{l}")
        saved.append(dict(h_in=h, u1=u1, w_qkv=w_qkv, w_f=w_f, b_pad=b_pad, g_heads=g_heads, qkv=qkv, fl=fl, crow=crow,
                          ccol=ccol, o=o, on=on, lse=lse, mix=mix, h_mid=h_mid, u2=u2, p=p, act=act, ff=ff, h_out=h_out))
        h, u1 = h_out, u1_next

    loss, dh = _loss(h, tgt_p, n_meta, S, "loss")

    grads = {k: [None] * depth for k in ("g_mix_pre", "w_in", "b_f", "g_sb", "g_fox", "w_out", "g_mix_post", "g_ffn_pre",
                                         "w_up", "conv_w", "conv_b", "w_down", "g_ffn_post")}
    du1_next = None
    for l in reversed(range(depth)):
        s = saved[l]
        g_next = W["g_mix_pre"][l + 1] if l + 1 < depth else None
        dh, d_ff, grads["g_ffn_post"][l], dg_pre_next = _resnorm_bwd(
            dh, du1_next, s["h_out"], s["ff"], W["g_ffn_post"][l], g_next, f"resnorm_b_bwd{l}")
        if l + 1 < depth:
            grads["g_mix_pre"][l + 1] = dg_pre_next
        d_act = _mm(d_ff, W["w_down"][l], "nt", BF16, f"d_act{l}")
        grads["w_down"][l] = _mm(s["act"], d_ff, "tn", BF16, f"dw_down{l}")
        da, grads["conv_w"][l], grads["conv_b"][l] = _convgate_bwd(s["p"], W["conv_w"][l], W["conv_b"][l], d_act, f"convgate_bwd{l}")
        dp = _conv_bwd_data(da, W["conv_w"][l], f"conv_bwd{l}")
        du2 = _mm(dp, W["w_up"][l], "nt", F32, f"d_u2{l}")
        grads["w_up"][l] = _mm(s["u2"], dp, "tn", BF16, f"dw_up{l}")
        dh, d_mix, grads["g_mix_post"][l], grads["g_ffn_pre"][l] = _resnorm_bwd(
            dh, du2, s["h_mid"], s["mix"], W["g_mix_post"][l], W["g_ffn_pre"][l], f"resnorm_a_bwd{l}")
        d_on = _mm(d_mix, W["w_out"][l], "nt", F32, f"d_on{l}")
        grads["w_out"][l] = _mm(s["on"], d_mix, "tn", BF16, f"dw_out{l}")
        dq, dk, dv, dg_heads, dcrow, brought = _attn_bwd(s["qkv"], s["crow"], s["ccol"], s["g_heads"], s["o"], s["lse"], d_on, H,
                                                         f"attn_bwd{l}", hooks and hooks.bwd_rider(l, grads))
        if hooks:
            hooks.bwd_done(l, brought)
        grads["g_sb"][l], grads["g_fox"][l] = dg_heads[:H], dg_heads[H:]
        dc = _pad_cols(dcrow[H:, 0, :].T, LANES)
        dfl, db = _gate_bwd(dc, s["fl"], s["b_pad"], f"gate_bwd{l}")
        grads["b_f"][l] = db[:H]
        Wh = H * HEAD_DIM
        d_qkv = jnp.concatenate([dq[:, :Wh], dk[:, :Wh], dv[:, :Wh], dq[:, Wh:], dk[:, Wh:], dv[:, Wh:]], axis=1)
        du1_next = _mm(d_qkv, s["w_qkv"], "nt", F32, f"d_u1{l}", a2=dfl, b2=s["w_f"])
        dw_qkv = _mm(s["u1"], d_qkv, "tn", BF16, f"dw_qkv{l}")
        dw_f = _mm(s["u1"], dfl, "tn", BF16, f"dw_f{l}")
        grads["w_in"][l] = jnp.concatenate([dw_qkv, dw_f[:, :H]], axis=1)
    dh0, _, _, grads["g_mix_pre"][0] = _resnorm_bwd(dh, du1_next, saved[0]["h_in"], None, None, W["g_mix_pre"][0], "prenorm0_bwd")
    return loss, dh0[n_meta:n_meta + S], dh0[:n_meta], grads


def _all_gather(x, name):
    def body(x_ref, out_ref, send_sems, recv_sems, local_sem):
        mx, my, mc = lax.axis_index("x"), lax.axis_index("y"), lax.axis_index("c")
        me, sibling = (mx, my, mc), (mx, my, 1 - mc)
        chips = [(1 - mx, my), (mx, 1 - my), (1 - mx, 1 - my)]

        def slot(px, py, pc):
            return out_ref.at[4 * px + 2 * py + pc]

        def copy(k, block, to, src=None):
            return pltpu.make_async_remote_copy(
                src_ref=slot(*block) if src is None else src, dst_ref=slot(*block),
                send_sem=send_sems.at[k], recv_sem=recv_sems.at[k], device_id=to, device_id_type=_MESH)

        mine = pltpu.make_async_copy(x_ref, slot(*me), local_sem)
        mine.start()
        first = [copy(0, me, sibling, src=x_ref)]
        first += [copy(1 + j, me, (*chip, mc), src=x_ref) for j, chip in enumerate(chips)]
        for cp in first:
            cp.start()
        passed = [copy(4 + j, (*chip, mc), sibling) for j, chip in enumerate(chips)]
        for j, chip in enumerate(chips):
            copy(1 + j, (*chip, mc), me).wait_recv()
            passed[j].start()
        copy(0, sibling, me).wait_recv()
        for j, chip in enumerate(chips):
            copy(4 + j, (*chip, 1 - mc), me).wait_recv()
        for cp in first + passed:
            cp.wait_send()
        mine.wait()

    return pl.pallas_call(
        body, name=name, out_shape=jax.ShapeDtypeStruct((N_DEV, *x.shape), x.dtype),
        in_specs=[_ANY], out_specs=_ANY,
        scratch_shapes=[pltpu.SemaphoreType.DMA((7,)), pltpu.SemaphoreType.DMA((7,)), pltpu.SemaphoreType.DMA],
    )(x)


def _exchange(g, name):
    def body(g_ref, land_ref, send_sems, recv_sems, local_sem):
        mx, my, mc = lax.axis_index("x"), lax.axis_index("y"), lax.axis_index("c")
        me = 4 * mx + 2 * my + mc
        local = pltpu.make_async_copy(g_ref.at[me], land_ref.at[me], local_sem)
        local.start()
        sends, recvs = [], []
        for k in range(1, N_DEV):
            peer = (_flip(mx, k & 4), _flip(my, k & 2), _flip(mc, k & 1))
            p = 4 * peer[0] + 2 * peer[1] + peer[2]
            sends.append(pltpu.make_async_remote_copy(
                src_ref=g_ref.at[p], dst_ref=land_ref.at[me], send_sem=send_sems.at[k - 1], recv_sem=recv_sems.at[k - 1],
                device_id=peer, device_id_type=_MESH))
            recvs.append(pltpu.make_async_remote_copy(
                src_ref=g_ref.at[p], dst_ref=land_ref.at[p], send_sem=send_sems.at[k - 1], recv_sem=recv_sems.at[k - 1],
                device_id=peer, device_id_type=_MESH))
        for cp in sends:
            cp.start()
        for cp in recvs:
            cp.wait_recv()
        for cp in sends:
            cp.wait_send()
        local.wait()

    return pl.pallas_call(
        body, name=name, out_shape=jax.ShapeDtypeStruct(g.shape, g.dtype), in_specs=[_ANY], out_specs=_ANY,
        scratch_shapes=[pltpu.SemaphoreType.DMA((7,)), pltpu.SemaphoreType.DMA((7,)), pltpu.SemaphoreType.DMA],
    )(g)


def _exchange_cores(g, name):
    def body(g_ref, land_ref, send_sems, recv_sems):
        mx, my, mc = lax.axis_index("x"), lax.axis_index("y"), lax.axis_index("c")
        copies = [pltpu.make_async_remote_copy(
            src_ref=g_ref.at[2 * q + (1 - mc)], dst_ref=land_ref.at[q], send_sem=send_sems.at[q], recv_sem=recv_sems.at[q],
            device_id=(mx, my, 1 - mc), device_id_type=_MESH) for q in range(4)]
        for cp in copies:
            cp.start()
        for cp in copies:
            cp.wait_recv()
        for cp in copies:
            cp.wait_send()

    return pl.pallas_call(
        body, name=name, out_shape=jax.ShapeDtypeStruct((4, *g.shape[1:]), g.dtype), in_specs=[_ANY], out_specs=_ANY,
        scratch_shapes=[pltpu.SemaphoreType.DMA((4,)), pltpu.SemaphoreType.DMA((4,))],
    )(g)


def _pair_sum(g, land, name):
    _, n, R, C = g.shape
    tr = _divisor(R, (128, 64, 32, 16, 8))
    core = lax.axis_index("c").astype(jnp.int32).reshape(1)

    def body(c_ref, g_ref, land_ref, o_ref):
        o_ref[...] = (g_ref[...].astype(F32) + land_ref[...].astype(F32)).astype(o_ref.dtype)

    blk = pl.BlockSpec((None, None, tr, C), lambda q, l, r, c_ref: (q, l, r, 0))
    return pl.pallas_call(
        body, name=name, out_shape=jax.ShapeDtypeStruct(land.shape, g.dtype),
        grid_spec=pltpu.PrefetchScalarGridSpec(
            num_scalar_prefetch=1, grid=(4, n, R // tr),
            in_specs=[pl.BlockSpec((None, None, tr, C), lambda q, l, r, c_ref: (2 * q + c_ref[0], l, r, 0)), blk],
            out_specs=blk),
        compiler_params=_params(("parallel", "parallel", "parallel"), 8 * tr * C * 4),
    )(core, g, land)


def _exchange_chips(part, name):
    def body(p_ref, land_ref, send_sems, recv_sems, local_sem):
        mx, my, mc = lax.axis_index("x"), lax.axis_index("y"), lax.axis_index("c")
        me = 2 * mx + my
        local = pltpu.make_async_copy(p_ref.at[me], land_ref.at[me], local_sem)
        local.start()
        sends, recvs = [], []
        for k in range(1, 4):
            px, py = _flip(mx, k & 2), _flip(my, k & 1)
            p = 2 * px + py
            sends.append(pltpu.make_async_remote_copy(
                src_ref=p_ref.at[p], dst_ref=land_ref.at[me], send_sem=send_sems.at[k - 1], recv_sem=recv_sems.at[k - 1],
                device_id=(px, py, mc), device_id_type=_MESH))
            recvs.append(pltpu.make_async_remote_copy(
                src_ref=p_ref.at[p], dst_ref=land_ref.at[p], send_sem=send_sems.at[k - 1], recv_sem=recv_sems.at[k - 1],
                device_id=(px, py, mc), device_id_type=_MESH))
        for cp in sends:
            cp.start()
        for cp in recvs:
            cp.wait_recv()
        for cp in sends:
            cp.wait_send()
        local.wait()

    return pl.pallas_call(
        body, name=name, out_shape=jax.ShapeDtypeStruct(part.shape, part.dtype), in_specs=[_ANY], out_specs=_ANY,
        scratch_shapes=[pltpu.SemaphoreType.DMA((3,)), pltpu.SemaphoreType.DMA((3,)), pltpu.SemaphoreType.DMA],
    )(part)


def _adamw_landed(lands, w, m, v, name):
    n, R, C = w.shape
    tr = _divisor(R, (64, 32, 16, 8))
    nr = R // tr
    blk = pl.BlockSpec((None, tr, C), lambda l, r: (l, r, 0))

    def land_spec(k, S):
        return pl.BlockSpec((S, tr, C), lambda l, r: (0, jnp.where(l == k, r, jnp.where(l < k, 0, nr - 1)), 0))

    def body(*refs):
        land_refs = refs[:n]
        w_ref, m_ref, v_ref, g_ref, d_ref, mo_ref, vo_ref = refs[n:]
        for k in range(n):
            @pl.when(pl.program_id(0) == k)
            def _(k=k):
                g = land_refs[k][0].astype(F32)
                for s in range(1, lands[k].shape[0]):
                    g = g + land_refs[k][s].astype(F32)
                m_new = ADAM_B1 * m_ref[...] + (1.0 - ADAM_B1) * g
                v_new = ADAM_B2 * v_ref[...] + (1.0 - ADAM_B2) * (g * g)
                m_hat = m_new / (1.0 - ADAM_B1 ** ADAM_STEP)
                v_hat = v_new / (1.0 - ADAM_B2 ** ADAM_STEP)
                g_ref[...] = g
                d_ref[...] = -ADAM_LR * (m_hat / (jnp.sqrt(v_hat) + ADAM_EPS) + ADAM_WD * w_ref[...])
                mo_ref[...] = m_new
                vo_ref[...] = v_new

    est = 2 * tr * C * (sum(x.shape[0] * jnp.dtype(x.dtype).itemsize for x in lands) + 7 * 4) * 9 // 8
    return pl.pallas_call(
        body, name=name, grid=(n, nr),
        in_specs=[land_spec(k, x.shape[0]) for k, x in enumerate(lands)] + [blk, blk, blk],
        out_specs=[blk] * 4, out_shape=[jax.ShapeDtypeStruct((n, R, C), F32)] * 4,
        compiler_params=_params(("arbitrary", "arbitrary"), est),
    )(*lands, w, m, v)


def _pack(arrs):
    flat = jnp.concatenate([a.reshape(-1).astype(F32) for a in arrs])
    rows = -(-flat.shape[0] // (8 * LANES)) * 8
    return jnp.pad(flat, (0, rows * LANES - flat.shape[0])).reshape(rows, LANES)


def _unpack(packed, shapes):
    flat, out, at = packed.reshape(-1), [], 0
    for s in shapes:
        n = math.prod(s)
        out.append(flat[at:at + n].reshape(s))
        at += n
    return out


_BIG = ("w_in", "w_out", "w_up", "w_down")
_REPLICATED = ("g_mix_pre", "b_f", "g_sb", "g_fox", "g_mix_post", "g_ffn_pre", "conv_b", "g_ffn_post")
_ORDER = ("meta", "g_mix_pre", "w_in", "b_f", "g_sb", "g_fox", "w_out", "g_mix_post", "g_ffn_pre", "w_up", "conv_w",
          "conv_b", "w_down", "g_ffn_post")
_COLUMN_SHARDED = ("w_in", "w_up")


def kernel(x, meta, g_mix_pre, w_in, b_f, g_sb, g_fox, w_out, g_mix_post, g_ffn_pre, w_up, conv_w, conv_b, w_down, g_ffn_post, loss_target, m_meta, m_g_mix_pre, m_w_in, m_b_f, m_g_sb, m_g_fox, m_w_out, m_g_mix_post, m_g_ffn_pre, m_w_up, m_conv_w, m_conv_b, m_w_down, m_g_ffn_post, v_meta, v_g_mix_pre, v_w_in, v_b_f, v_g_sb, v_g_fox, v_w_out, v_g_mix_post, v_g_ffn_pre, v_w_up, v_conv_w, v_conv_b, v_w_down, v_g_ffn_post):
    w = dict(meta=meta, g_mix_pre=g_mix_pre, w_in=w_in, b_f=b_f, g_sb=g_sb, g_fox=g_fox, w_out=w_out, g_mix_post=g_mix_post,
             g_ffn_pre=g_ffn_pre, w_up=w_up, conv_w=conv_w, conv_b=conv_b, w_down=w_down, g_ffn_post=g_ffn_post)
    m = dict(meta=m_meta, g_mix_pre=m_g_mix_pre, w_in=m_w_in, b_f=m_b_f, g_sb=m_g_sb, g_fox=m_g_fox, w_out=m_w_out,
             g_mix_post=m_g_mix_post, g_ffn_pre=m_g_ffn_pre, w_up=m_w_up, conv_w=m_conv_w, conv_b=m_conv_b, w_down=m_w_down,
             g_ffn_post=m_g_ffn_post)
    v = dict(meta=v_meta, g_mix_pre=v_g_mix_pre, w_in=v_w_in, b_f=v_b_f, g_sb=v_g_sb, g_fox=v_g_fox, w_out=v_w_out,
             g_mix_post=v_g_mix_post, g_ffn_pre=v_g_ffn_pre, w_up=v_w_up, conv_w=v_conv_w, conv_b=v_conv_b, w_down=v_w_down,
             g_ffn_post=v_g_ffn_post)
    depth = w_in.shape[0]
    shards = {name: [w[name][l].astype(BF16) for l in range(depth)] for name in _BIG}

    def assemble(name, g):
        if name in _COLUMN_SHARDED:
            return jnp.transpose(g, (1, 0, 2)).reshape(g.shape[1], -1)
        return g.reshape(-1, g.shape[2])

    def blocks_of(name, grad):
        R, C = w[name].shape[1:]
        if name in _COLUMN_SHARDED:
            return jnp.transpose(grad.reshape(R, N_DEV, C), (1, 0, 2)).astype(BF16)
        return grad.reshape(N_DEV, R, C).astype(BF16)

    class Hooks:
        def __init__(self):
            self.landed = {}

        def fwd_keys(self, l):
            return ([("w_in", l + 1)] if l + 1 < depth else []) + [("w_out", l), ("w_up", l), ("w_down", l)]

        def bwd_keys(self, l):
            return ([("w_in", l + 1)] if l + 1 < depth else []) + [("w_down", l), ("w_up", l), ("w_out", l)]

        def fwd_rider(self, l):
            return _GatherRider([shards[name][ll] for name, ll in self.fwd_keys(l)])

        def fwd_done(self, l, brought, W):
            for (name, ll), g in zip(self.fwd_keys(l), brought):
                W[name][ll] = assemble(name, g)

        def bwd_rider(self, l, grads):
            return _ExchangeRider([blocks_of(name, grads[name][ll]) for name, ll in self.bwd_keys(l)])

        def bwd_done(self, l, brought):
            for key, land in zip(self.bwd_keys(l), brought):
                self.landed[key] = land

    small_shapes = [conv_w.shape, meta.shape]
    gs = _all_gather(_pack([conv_w, meta]), "gather_small")
    parts = [_unpack(gs[d], small_shapes) for d in range(N_DEV)]
    conv_full = jnp.concatenate([p[0] for p in parts], axis=2)
    meta_full = jnp.concatenate([p[1] for p in parts], axis=1)

    W = {name: [None] * depth for name in _BIG}
    W["w_in"][0] = assemble("w_in", _all_gather(shards["w_in"][0], "gather_w_in0"))
    W["conv_w"] = [conv_full[l] for l in range(depth)]
    for name in _REPLICATED:
        W[name] = [w[name][l] for l in range(depth)]
    hooks = Hooks()
    loss, grad_x, d_meta, grads = _local_step(x[0], loss_target[0], meta_full, W, hooks)

    first = blocks_of("w_in", grads["w_in"][0])[:, None]
    pairs = _pair_sum(first, _exchange_cores(first, "exchange_cores_w_in0"), "pair_sum_w_in0")
    hooks.landed[("w_in", 0)] = _exchange_chips(pairs, "exchange_chips_w_in0")[:, 0]
    out = {}
    for name in _BIG:
        out[name] = _adamw_landed([hooks.landed[(name, l)] for l in range(depth)], w[name], m[name], v[name], f"adamw_{name}")

    Fs, Ms = conv_w.shape[2], meta.shape[1]
    d_conv = jnp.stack(grads["conv_w"], axis=0)
    blocks = jnp.stack([_pack([d_conv[:, :, d * Fs:(d + 1) * Fs], d_meta[:, d * Ms:(d + 1) * Ms]]) for d in range(N_DEV)])
    land = _exchange(blocks, "exchange_small")
    res = _adamw_landed([land], _pack([conv_w, meta])[None], _pack([m_conv_w, m_meta])[None],
                        _pack([v_conv_w, v_meta])[None], "adamw_small")
    for i, r in enumerate(res):
        cw, mt = _unpack(r[0], small_shapes)
        out.setdefault("conv_w", [None] * 4)[i] = cw
        out.setdefault("meta", [None] * 4)[i] = mt

    rep_shapes = [()] + [w[name].shape for name in _REPLICATED]
    mine = _pack([loss] + [jnp.stack(grads[name], axis=0) for name in _REPLICATED])
    land = _all_gather(mine, "gather_replicated")
    zero = jnp.zeros((), F32)
    res = _adamw_landed([land], _pack([zero] + [w[n] for n in _REPLICATED])[None],
                        _pack([zero] + [m[n] for n in _REPLICATED])[None],
                        _pack([zero + 1.0] + [v[n] for n in _REPLICATED])[None], "adamw_replicated")
    for i, r in enumerate(res):
        vals = _unpack(r[0], rep_shapes)
        if i == 0:
            loss_total = vals[0]
        for name, val in zip(_REPLICATED, vals[1:]):
            out.setdefault(name, [None] * 4)[i] = val

    return (loss_total, grad_x[None], *[out[n][0] for n in _ORDER], *[out[n][1] for n in _ORDER],
            *[out[n][2] for n in _ORDER], *[out[n][3] for n in _ORDER])
```

```python
import functools
import math

import jax
import jax.numpy as jnp
from jax import lax
from jax.experimental import pallas as pl
from jax.experimental.pallas import tpu as pltpu

F32, BF16 = jnp.float32, jnp.bfloat16
HEAD_DIM = 128
LANES = 128
EPS = 1e-6
NEG_INF = -1e30
ATT_BLOCK = 256
N_DEV = 8
V7X_VMEM_BUDGET = 56 * 1024 * 1024

ADAM_LR, ADAM_B1, ADAM_B2, ADAM_EPS, ADAM_WD, ADAM_STEP = 0.001, 0.9, 0.999, 1e-08, 0.01, 10


def _divisor(n, cands):
    for c in cands:
        if c <= n and n % c == 0:
            return c
    raise ValueError(f"no tile for {n} among {cands}")


def _params(sem, est_bytes):
    limit = int(min(V7X_VMEM_BUDGET, max(16 * 1024 * 1024, est_bytes * 5 // 4 + (2 << 20))))
    return pltpu.CompilerParams(dimension_semantics=sem, vmem_limit_bytes=limit)


def _nbytes(shape, dtype):
    return math.prod(shape) * jnp.dtype(dtype).itemsize


_DN = {"nn": (((1,), (0,)), ((), ())), "nt": (((1,), (1,)), ((), ())), "tn": (((0,), (0,)), ((), ()))}
_ROW_TILES = (1088, 544, 272, 512, 256, 128, 64, 32, 16, 8)
_COL_TILES = (1024, 512, 256, 128)
MM_VMEM_BLOCKS = 40 * 1024 * 1024


def _mm_tiles(M, N, K, rows, out_bytes):
    best = None
    for tm in rows:
        for tn in _COL_TILES:
            if tm > M or tn > N or M % tm or N % tn:
                continue
            if 4 * (tm + tn) * K + 2 * tm * tn * out_bytes > MM_VMEM_BLOCKS:
                continue
            if best is None or tm * tn > best[0] * best[1]:
                best = (tm, tn, K)
    if best is not None:
        return best
    return _divisor(M, rows), _divisor(N, _COL_TILES[1:]), _ktile(K, 3072)


def _ktile(k, cap):
    if k <= cap:
        return k
    for t in range(cap - cap % LANES, 0, -LANES):
        if k % t == 0:
            return t
    raise ValueError(k)


def _mm(a, b, mode, out_dtype, name, a2=None, b2=None):
    if mode == "nn":
        (M, K), (_, N) = a.shape, b.shape
    elif mode == "nt":
        (M, K), (N, _) = a.shape, b.shape
    else:
        (K, M), (_, N) = a.shape, b.shape
    tm, tn, tk = _mm_tiles(M, N, K, _COL_TILES if mode == "tn" else _ROW_TILES, jnp.dtype(out_dtype).itemsize)
    nk = K // tk
    a_spec = {"nn": pl.BlockSpec((tm, tk), lambda i, j, k: (i, k)),
              "nt": pl.BlockSpec((tm, tk), lambda i, j, k: (i, k)),
              "tn": pl.BlockSpec((tk, tm), lambda i, j, k: (k, i))}[mode]
    b_spec = {"nn": pl.BlockSpec((tk, tn), lambda i, j, k: (k, j)),
              "nt": pl.BlockSpec((tn, tk), lambda i, j, k: (j, k)),
              "tn": pl.BlockSpec((tk, tn), lambda i, j, k: (k, j))}[mode]
    dn = _DN[mode]
    extra = a2 is not None
    in_specs, args = [a_spec, b_spec], [a, b]
    if extra:
        k2 = a2.shape[1]
        in_specs += [pl.BlockSpec((tm, k2), lambda i, j, k: (i, 0)), pl.BlockSpec((tn, k2), lambda i, j, k: (j, 0))]
        args += [a2, b2]

    def body(*refs):
        if extra:
            a_ref, b_ref, a2_ref, b2_ref, o_ref, acc = refs
        else:
            a_ref, b_ref, o_ref, acc = refs
        part = lax.dot_general(a_ref[...], b_ref[...], dn, preferred_element_type=F32)
        if nk == 1:
            if extra:
                part = part + lax.dot_general(a2_ref[...], b2_ref[...], _DN["nt"], preferred_element_type=F32)
            o_ref[...] = part.astype(o_ref.dtype)
            return
        kk = pl.program_id(2)

        @pl.when(kk == 0)
        def _():
            if extra:
                acc[...] = part + lax.dot_general(a2_ref[...], b2_ref[...], _DN["nt"], preferred_element_type=F32)
            else:
                acc[...] = part

        @pl.when(kk > 0)
        def _():
            acc[...] += part

        @pl.when(kk == nk - 1)
        def _():
            o_ref[...] = acc[...].astype(o_ref.dtype)

    est = 2 * (tm * tk + tk * tn) * 2 + 2 * _nbytes((tm, tn), out_dtype) + (tm * tn * 4 if nk > 1 else 0)
    return pl.pallas_call(
        body, name=name, grid=(M // tm, N // tn, nk), in_specs=in_specs,
        out_specs=pl.BlockSpec((tm, tn), lambda i, j, k: (i, j)),
        out_shape=jax.ShapeDtypeStruct((M, N), out_dtype),
        scratch_shapes=[pltpu.VMEM((tm, tn) if nk > 1 else (8, LANES), F32)],
        compiler_params=_params(("parallel", "parallel", "arbitrary"), est),
    )(*args)


def _rms(x, g):
    r = lax.rsqrt(jnp.mean(x * x, axis=-1, keepdims=True) + EPS)
    return x * r * g


def _rms_bwd(x, g, dout):
    r = lax.rsqrt(jnp.mean(x * x, axis=-1, keepdims=True) + EPS)
    xhat = x * r
    dxh = dout * g
    dx = r * (dxh - xhat * jnp.mean(dxh * xhat, axis=-1, keepdims=True))
    return dx, dout * xhat


def _row_tile(L):
    return _divisor(L, (272, 256, 128, 64, 32, 16))


def _resnorm_fwd(h, y, g_post, g_pre, name):
    L, D = h.shape
    bt = _row_tile(L)
    has_y, has_pre = y is not None, g_pre is not None
    row = pl.BlockSpec((bt, D), lambda i: (i, 0))
    vec = pl.BlockSpec((1, D), lambda i: (0, 0))
    args, in_specs = [h], [row]
    if has_y:
        args += [y, g_post.reshape(1, D)]
        in_specs += [row, vec]
    if has_pre:
        args += [g_pre.reshape(1, D)]
        in_specs += [vec]
    out_shape, out_specs = [], []
    if has_y:
        out_shape.append(jax.ShapeDtypeStruct((L, D), F32))
        out_specs.append(row)
    if has_pre:
        out_shape.append(jax.ShapeDtypeStruct((L, D), BF16))
        out_specs.append(row)

    def body(*refs):
        refs = list(refs)
        h_ref = refs.pop(0)
        hn = h_ref[...]
        if has_y:
            y_ref, gp_ref = refs.pop(0), refs.pop(0)
            hn = hn + _rms(y_ref[...], gp_ref[...])
        if has_pre:
            g_ref = refs.pop(0)
        if has_y:
            refs.pop(0)[...] = hn
        if has_pre:
            refs.pop(0)[...] = _rms(hn, g_ref[...]).astype(BF16)

    outs = pl.pallas_call(
        body, name=name, grid=(L // bt,), in_specs=in_specs, out_specs=out_specs, out_shape=out_shape,
        compiler_params=_params(("parallel",), 10 * bt * D * 4),
    )(*args)
    outs = list(outs)
    h_new = outs.pop(0) if has_y else h
    u = outs.pop(0) if has_pre else None
    return h_new, u


def _resnorm_bwd(dh_direct, du, h_new, y, g_post, g_pre, name):
    L, D = dh_direct.shape
    bt = _row_tile(L)
    has_y, has_pre = y is not None, du is not None
    row = pl.BlockSpec((bt, D), lambda i: (i, 0))
    vec = pl.BlockSpec((1, D), lambda i: (0, 0))
    acc = pl.BlockSpec((8, D), lambda i: (0, 0))
    args, in_specs = [dh_direct], [row]
    if has_pre:
        args += [du, h_new, g_pre.reshape(1, D)]
        in_specs += [row, row, vec]
    if has_y:
        args += [y, g_post.reshape(1, D)]
        in_specs += [row, vec]
    out_shape, out_specs = [], []
    if has_pre:
        out_shape += [jax.ShapeDtypeStruct((L, D), F32), jax.ShapeDtypeStruct((8, D), F32)]
        out_specs += [row, acc]
    if has_y:
        out_shape += [jax.ShapeDtypeStruct((L, D), BF16), jax.ShapeDtypeStruct((8, D), F32)]
        out_specs += [row, acc]

    def colsum8(v):
        return jnp.sum(v.reshape(bt // 8, 8, D), axis=0)

    def body(*refs):
        refs = list(refs)
        first = pl.program_id(0) == 0
        dh = refs.pop(0)[...]
        if has_pre:
            du_ref, hn_ref, g_ref = refs.pop(0), refs.pop(0), refs.pop(0)
        if has_y:
            y_ref, gp_ref = refs.pop(0), refs.pop(0)
        if has_pre:
            dh_ref, dgpre_ref = refs.pop(0), refs.pop(0)
            dx, dgp = _rms_bwd(hn_ref[...], g_ref[...], du_ref[...].astype(F32))
            dh = dh + dx
            dh_ref[...] = dh

            @pl.when(first)
            def _():
                dgpre_ref[...] = jnp.zeros_like(dgpre_ref)
            dgpre_ref[...] += colsum8(dgp)
        if has_y:
            dy_ref, dgpost_ref = refs.pop(0), refs.pop(0)
            dy, dgq = _rms_bwd(y_ref[...], gp_ref[...], dh)
            dy_ref[...] = dy.astype(BF16)

            @pl.when(first)
            def _():
                dgpost_ref[...] = jnp.zeros_like(dgpost_ref)
            dgpost_ref[...] += colsum8(dgq)

    outs = list(pl.pallas_call(
        body, name=name, grid=(L // bt,), in_specs=in_specs, out_specs=out_specs, out_shape=out_shape,
        compiler_params=_params(("arbitrary",), 14 * bt * D * 4),
    )(*args))
    dh, dg_pre, dy, dg_post = dh_direct, None, None, None
    if has_pre:
        dh, dg_pre = outs.pop(0), outs.pop(0).sum(0)
    if has_y:
        dy, dg_post = outs.pop(0), outs.pop(0).sum(0)
    return dh, dy, dg_post, dg_pre


def _loss(h, tgt, n_meta, seq, name):
    L, D = h.shape
    bt = _row_tile(L)
    row = pl.BlockSpec((bt, D), lambda i: (i, 0))

    def body(h_ref, t_ref, dy_ref, loss_ref):
        i = pl.program_id(0)
        r = i * bt + lax.broadcasted_iota(jnp.int32, (bt, 1), 0)
        valid = (r >= n_meta) & (r < n_meta + seq)
        e = jnp.where(valid, h_ref[...] - t_ref[...], 0.0)
        dy_ref[...] = e * (1.0 / D)

        @pl.when(i == 0)
        def _():
            loss_ref[...] = jnp.zeros_like(loss_ref)
        loss_ref[...] += 0.5 * jnp.sum(jnp.sum(e * e, axis=-1, keepdims=True) * (1.0 / D))

    dy, loss = pl.pallas_call(
        body, name=name, grid=(L // bt,), in_specs=[row, row],
        out_specs=[row, pl.BlockSpec((8, LANES), lambda i: (0, 0))],
        out_shape=[jax.ShapeDtypeStruct((L, D), F32), jax.ShapeDtypeStruct((8, LANES), F32)],
        compiler_params=_params(("arbitrary",), 8 * bt * D * 4),
    )(h, tgt)
    return loss[0, 0], dy


def _split3(x):
    x1 = x.astype(BF16)
    r1 = x - x1.astype(F32)
    x2 = r1.astype(BF16)
    x3 = (r1 - x2.astype(F32)).astype(BF16)
    return x1, x2, x3


def _tri_dot3(tri, x):
    x1, x2, x3 = _split3(x)
    d = functools.partial(jnp.dot, preferred_element_type=F32)
    return d(tri, x1) + d(tri, x2) + d(tri, x3)


def _gate_fwd(fl, b_pad, name):
    L = fl.shape[0]
    bt = 128
    blk = pl.BlockSpec((bt, LANES), lambda i: (i, 0))

    def body(fl_ref, b_ref, c_ref, carry):
        @pl.when(pl.program_id(0) == 0)
        def _():
            carry[...] = jnp.zeros_like(carry)
        x = fl_ref[...] + b_ref[...]
        lf = jnp.minimum(x, 0.0) - jnp.log(1.0 + jnp.exp(-jnp.abs(x)))
        r = lax.broadcasted_iota(jnp.int32, (bt, bt), 0)
        s = lax.broadcasted_iota(jnp.int32, (bt, bt), 1)
        tri = (s <= r).astype(BF16)
        c = _tri_dot3(tri, lf) + carry[...]
        c_ref[...] = c
        carry[...] = c[bt - 1:bt, :]

    return pl.pallas_call(
        body, name=name, grid=(L // bt,), in_specs=[blk, pl.BlockSpec((1, LANES), lambda i: (0, 0))],
        out_specs=blk, out_shape=jax.ShapeDtypeStruct((L, LANES), F32),
        scratch_shapes=[pltpu.VMEM((1, LANES), F32)],
        compiler_params=_params(("arbitrary",), 1 << 20),
    )(fl, b_pad)


def _gate_bwd(dc, fl, b_pad, name):
    L = fl.shape[0]
    bt = 128
    n = L // bt
    blk = pl.BlockSpec((bt, LANES), lambda i: (n - 1 - i, 0))

    def body(dc_ref, fl_ref, b_ref, dfl_ref, db_ref, carry):
        @pl.when(pl.program_id(0) == 0)
        def _():
            carry[...] = jnp.zeros_like(carry)
            db_ref[...] = jnp.zeros_like(db_ref)
        r = lax.broadcasted_iota(jnp.int32, (bt, bt), 0)
        s = lax.broadcasted_iota(jnp.int32, (bt, bt), 1)
        tri = (s >= r).astype(BF16)
        dlf = _tri_dot3(tri, dc_ref[...]) + carry[...]
        carry[...] = dlf[0:1, :]
        x = fl_ref[...] + b_ref[...]
        dfl = dlf / (1.0 + jnp.exp(x))
        dfl_ref[...] = dfl.astype(BF16)
        db_ref[...] += jnp.sum(dfl.reshape(bt // 8, 8, LANES), axis=0)

    dfl, db = pl.pallas_call(
        body, name=name, grid=(n,), in_specs=[blk, blk, pl.BlockSpec((1, LANES), lambda i: (0, 0))],
        out_specs=[blk, pl.BlockSpec((8, LANES), lambda i: (0, 0))],
        out_shape=[jax.ShapeDtypeStruct((L, LANES), BF16), jax.ShapeDtypeStruct((8, LANES), F32)],
        scratch_shapes=[pltpu.VMEM((1, LANES), F32)],
        compiler_params=_params(("arbitrary",), 1 << 20),
    )(dc, fl, b_pad)
    return dfl, db.sum(0)


_MESH = pl.DeviceIdType.MESH
_ANY = pl.BlockSpec(memory_space=pl.ANY)


def _flip(v, bit):
    return 1 - v if bit else v


class _GatherRider:
    def __init__(self, blocks, side_by_side):
        self.blocks = list(blocks)
        self.side_by_side = list(side_by_side)
        self.n = len(self.blocks)
        self.in_specs = [_ANY] * self.n
        self.out_specs = [_ANY] * self.n
        self.out_shape = [jax.ShapeDtypeStruct((b.shape[0], N_DEV * b.shape[1]) if cols else (N_DEV, *b.shape), b.dtype)
                          for b, cols in zip(self.blocks, self.side_by_side)]
        self.scratch = [pltpu.SemaphoreType.DMA((7 * self.n,)), pltpu.SemaphoreType.DMA((7 * self.n,)),
                        pltpu.SemaphoreType.DMA((self.n,))]

    def _copies(self, a, x_ref, out_ref, send_sems, recv_sems):
        mx, my, mc = lax.axis_index("x"), lax.axis_index("y"), lax.axis_index("c")
        me, sibling = (mx, my, mc), (mx, my, 1 - mc)
        chips = [(1 - mx, my), (mx, 1 - my), (1 - mx, 1 - my)]
        width = self.blocks[a].shape[1]

        def slot(px, py, pc):
            d = 4 * px + 2 * py + pc
            if self.side_by_side[a]:
                return out_ref.at[:, pl.ds(pl.multiple_of(d * width, LANES), width)]
            return out_ref.at[d]

        def copy(k, block, to, src=None):
            return pltpu.make_async_remote_copy(
                src_ref=slot(*block) if src is None else src, dst_ref=slot(*block),
                send_sem=send_sems.at[7 * a + k], recv_sem=recv_sems.at[7 * a + k], device_id=to, device_id_type=_MESH)

        first = [copy(0, me, sibling, src=x_ref)] + [copy(1 + j, me, (*chip, mc), src=x_ref) for j, chip in enumerate(chips)]
        landed = [copy(1 + j, (*chip, mc), me) for j, chip in enumerate(chips)]
        passed = [copy(4 + j, (*chip, mc), sibling) for j, chip in enumerate(chips)]
        last = [copy(0, sibling, me)] + [copy(4 + j, (*chip, 1 - mc), me) for j, chip in enumerate(chips)]
        return slot(*me), first, landed, passed, last

    def start(self, ins, outs, send_sems, recv_sems, local_sems):
        for a in range(self.n):
            mine, first, _, _, _ = self._copies(a, ins[a], outs[a], send_sems, recv_sems)
            pltpu.make_async_copy(ins[a], mine, local_sems.at[a]).start()
            for cp in first:
                cp.start()

    def middle(self, ins, outs, send_sems, recv_sems, local_sems):
        for a in range(self.n):
            _, _, landed, passed, _ = self._copies(a, ins[a], outs[a], send_sems, recv_sems)
            for arrived, onward in zip(landed, passed):
                arrived.wait_recv()
                onward.start()

    def finish(self, ins, outs, send_sems, recv_sems, local_sems):
        for a in range(self.n):
            mine, first, _, passed, last = self._copies(a, ins[a], outs[a], send_sems, recv_sems)
            for cp in last:
                cp.wait_recv()
            for cp in first + passed:
                cp.wait_send()
            pltpu.make_async_copy(ins[a], mine, local_sems.at[a]).wait()


class _ExchangeRider:
    def __init__(self, arrays, side_by_side):
        self.blocks = list(arrays)
        self.side_by_side = list(side_by_side)
        self.n = len(self.blocks)
        self.in_specs = [_ANY] * self.n
        self.out_specs = [_ANY] * self.n
        self.out_shape = [jax.ShapeDtypeStruct((N_DEV, b.shape[0], b.shape[1] // N_DEV) if cols else b.shape, b.dtype)
                          for b, cols in zip(self.blocks, self.side_by_side)]
        self.scratch = [pltpu.SemaphoreType.DMA((7 * self.n,)), pltpu.SemaphoreType.DMA((7 * self.n,)),
                        pltpu.SemaphoreType.DMA((self.n,))]

    def _copies(self, a, g_ref, land_ref, send_sems, recv_sems):
        mx, my, mc = lax.axis_index("x"), lax.axis_index("y"), lax.axis_index("c")
        me = 4 * mx + 2 * my + mc
        width = self.out_shape[a].shape[2]

        def block(j):
            if self.side_by_side[a]:
                return g_ref.at[:, pl.ds(pl.multiple_of(j * width, LANES), width)]
            return g_ref.at[j]

        sends, recvs = [], []
        for k in range(1, N_DEV):
            peer = (_flip(mx, k & 4), _flip(my, k & 2), _flip(mc, k & 1))
            p = 4 * peer[0] + 2 * peer[1] + peer[2]
            sems = dict(send_sem=send_sems.at[7 * a + k - 1], recv_sem=recv_sems.at[7 * a + k - 1], device_id=peer, device_id_type=_MESH)
            sends.append(pltpu.make_async_remote_copy(src_ref=block(p), dst_ref=land_ref.at[me], **sems))
            recvs.append(pltpu.make_async_remote_copy(src_ref=block(p), dst_ref=land_ref.at[p], **sems))
        return block(me), land_ref.at[me], sends, recvs

    def start(self, ins, outs, send_sems, recv_sems, local_sems):
        for a in range(self.n):
            src, dst, sends, _ = self._copies(a, ins[a], outs[a], send_sems, recv_sems)
            pltpu.make_async_copy(src, dst, local_sems.at[a]).start()
            for cp in sends:
                cp.start()

    def middle(self, ins, outs, send_sems, recv_sems, local_sems):
        pass

    def finish(self, ins, outs, send_sems, recv_sems, local_sems):
        for a in range(self.n):
            src, dst, sends, recvs = self._copies(a, ins[a], outs[a], send_sems, recv_sems)
            for cp in recvs:
                cp.wait_recv()
            for cp in sends:
                cp.wait_send()
            pltpu.make_async_copy(src, dst, local_sems.at[a]).wait()


class _NoRider:
    blocks, in_specs, out_specs, out_shape, scratch = [], [], [], [], []


_NO_RIDER = _NoRider()


def _ride(rider, refs, n_in, n_out, first, middle, last, work):
    if rider is None:
        return work(*refs)
    r = rider.n
    own = refs[:n_in] + refs[n_in + r:n_in + r + n_out] + refs[n_in + r + n_out + r:len(refs) - 3]
    args = (refs[n_in:n_in + r], refs[n_in + r + n_out:n_in + r + n_out + r], *refs[len(refs) - 3:])
    pl.when(first)(lambda: rider.start(*args))
    pl.when(middle)(lambda: rider.middle(*args))
    work(*own)
    pl.when(last)(lambda: rider.finish(*args))


def _split2(x):
    hi = x.astype(BF16)
    return hi, (x - hi.astype(F32)).astype(BF16)


def _dot_hi_lo(x, w2):
    hi, lo = _split2(x)
    return jnp.dot(jnp.concatenate([hi, lo], axis=1), w2, preferred_element_type=F32)


def _suffix_sums(x, tri2, exact):
    if exact:
        return _dot_hi_lo(x, tri2)
    return jnp.dot(x.astype(BF16), tri2[:x.shape[1]], preferred_element_type=F32)


def _dot_nt(a, b):
    return lax.dot_general(a, b, _DN["nt"], preferred_element_type=F32)


def _suffix_matrix(bk):
    j = lax.broadcasted_iota(jnp.int32, (2 * bk, bk), 0) % bk
    s = lax.broadcasted_iota(jnp.int32, (2 * bk, bk), 1)
    return (j >= s).astype(BF16)


LOG2E = 1.4426950408889634


def _sb_window(q, k, t_pos, ks, bk, scale2, carry_c, tri, masked):
    width = k.shape[0]
    z = _dot_nt(q, k) * scale2
    sp = jnp.maximum(z, 0.0) + jnp.log2(1.0 + jnp.exp2(-jnp.abs(z)))
    if masked:
        mask = ks + lax.broadcasted_iota(jnp.int32, (1, width), 1) < t_pos
        lkm = jnp.where(mask, -sp, 0.0)
    else:
        mask, lkm = None, -sp
    sums, run = _window_suffix_sums(lkm, bk, tri, carry_c, False)
    a = jnp.exp2(z + sums)
    if masked:
        a = jnp.where(mask, a, 0.0)
    return z, sp, mask, a, run


def _window_suffix_sums(x, bk, tri, carry, exact):
    nb = x.shape[1] // bk
    parts, run = [None] * nb, carry
    for b in reversed(range(nb)):
        cs = _suffix_sums(x[:, b * bk:(b + 1) * bk], tri, exact)
        parts[b] = run + cs
        run = run + cs[:, 0:1]
    return (parts[0] if nb == 1 else jnp.concatenate(parts, axis=1)), run


KEY_WINDOW = 4


def _key_tiles(i, bk, step, carry):
    r = i % KEY_WINDOW

    def first(w):
        return lambda c: step(pl.multiple_of((i - w) * bk, bk), (w + 1) * bk, c, True)

    carry = lax.switch(r, [first(w) for w in range(KEY_WINDOW)], carry)
    top = i - r
    return lax.fori_loop(
        0, top // KEY_WINDOW,
        lambda p, c: step(pl.multiple_of((top - KEY_WINDOW * (p + 1)) * bk, bk), KEY_WINDOW * bk, c, False), carry)


def _attn_specs(H, L, bq):
    W3 = 3 * H

    def col(role):
        return lambda h, i: (h // H) * W3 + role * H + h % H

    q_spec = pl.BlockSpec((bq, HEAD_DIM), lambda h, i: (i, col(0)(h, i)))
    k_spec = pl.BlockSpec((L, HEAD_DIM), lambda h, i: (0, col(1)(h, i)))
    v_spec = pl.BlockSpec((L, HEAD_DIM), lambda h, i: (0, col(2)(h, i)))
    crow_spec = pl.BlockSpec((None, 1, L), lambda h, i: (jnp.maximum(h - H, 0), 0, 0))
    ccol_spec = pl.BlockSpec((None, bq, 1), lambda h, i: (jnp.maximum(h - H, 0), i, 0))
    g_spec = pl.BlockSpec((None, 1, HEAD_DIM), lambda h, i: (h, 0, 0))
    tile = pl.BlockSpec((bq, HEAD_DIM), lambda h, i: (i, h))
    stat = pl.BlockSpec((None, bq, 1), lambda h, i: (h, i, 0))
    return q_spec, k_spec, v_spec, crow_spec, ccol_spec, g_spec, tile, stat


def _attn_fwd(qkv, crow, ccol, g_heads, H, name, rider=None):
    L = qkv.shape[0]
    bq = bk = min(ATT_BLOCK, L)
    nq = L // bq
    scale = HEAD_DIM ** -0.5
    scale2 = scale * LOG2E
    q_spec, k_spec, v_spec, crow_spec, ccol_spec, g_spec, tile, stat = _attn_specs(H, L, bq)

    def work(q_ref, k_ref, v_ref, crow_ref, ccol_ref, g_ref, o_ref, on_ref, lse_ref):
        h, i = pl.program_id(0), pl.program_id(1)
        q = q_ref[...]
        t_pos = i * bq + lax.broadcasted_iota(jnp.int32, (bq, 1), 0)

        def finish(o):
            o_ref[...] = o
            on_ref[...] = _rms(o, g_ref[...]).astype(BF16)

        @pl.when(h < H)
        def _stick_breaking():
            tri = _suffix_matrix(bk)

            def step(ks, rows, carry, masked):
                c, acc = carry
                k = k_ref[pl.ds(ks, rows), :]
                v = v_ref[pl.ds(ks, rows), :]
                _, _, _, a, c = _sb_window(q, k, t_pos, ks, bk, scale2, c, tri, masked)
                acc = acc + jnp.dot(a.astype(BF16), v, preferred_element_type=F32)
                return c, acc

            _, acc = _key_tiles(i, bk, step, (jnp.zeros((bq, 1), F32), jnp.zeros((bq, HEAD_DIM), F32)))
            finish(acc)
            lse_ref[...] = jnp.zeros_like(lse_ref)

        @pl.when(h >= H)
        def _forgetting():
            cq = ccol_ref[...] * LOG2E

            def step(ks, rows, carry, masked):
                m, l, acc = carry
                k = k_ref[pl.ds(ks, rows), :]
                v = v_ref[pl.ds(ks, rows), :]
                s = _dot_nt(q, k) * scale2 + (cq - crow_ref[:, pl.ds(ks, rows)] * LOG2E)
                if masked:
                    s = jnp.where(ks + lax.broadcasted_iota(jnp.int32, (1, rows), 1) <= t_pos, s, NEG_INF)
                m_new = jnp.maximum(m, jnp.max(s, axis=-1, keepdims=True))
                alpha = jnp.exp2(m - m_new)
                p = jnp.exp2(s - m_new)
                l = alpha * l + jnp.sum(p, axis=-1, keepdims=True)
                acc = alpha * acc + _dot_hi_lo(p, jnp.concatenate([v, v], axis=0))
                return m_new, l, acc

            init = (jnp.full((bq, 1), NEG_INF, F32), jnp.zeros((bq, 1), F32), jnp.zeros((bq, HEAD_DIM), F32))
            m, l, acc = _key_tiles(i, bk, step, init)
            finish(acc / l)
            lse_ref[...] = m + jnp.log2(l)

    def body(*refs):
        h, i = pl.program_id(0), pl.program_id(1)
        _ride(rider, refs, 6, 3, (h == 0) & (i == 0), (h == 2 * H - 2) & (i == 0), (h == 2 * H - 1) & (i == nq - 1), work)

    W2 = 2 * H * HEAD_DIM
    est = 4 * L * HEAD_DIM * 2 + 32 * bq * bk * 4 + (4 << 20)
    extra = rider or _NO_RIDER
    outs = pl.pallas_call(
        body, name=name, grid=(2 * H, nq),
        in_specs=[q_spec, k_spec, v_spec, crow_spec, ccol_spec, g_spec] + extra.in_specs,
        out_specs=[tile, tile, stat] + extra.out_specs,
        out_shape=[jax.ShapeDtypeStruct((L, W2), F32), jax.ShapeDtypeStruct((L, W2), BF16),
                   jax.ShapeDtypeStruct((2 * H, L, 1), F32)] + extra.out_shape,
        scratch_shapes=extra.scratch,
        compiler_params=_params(("arbitrary", "arbitrary"), est),
    )(qkv, qkv, qkv, crow, ccol, g_heads, *extra.blocks)
    return outs[0], outs[1], outs[2], list(outs[3:])


def _attn_bwd(qkv, crow, ccol, g_heads, o, lse, d_on, H, name, rider=None):
    L = qkv.shape[0]
    bq = bk = min(ATT_BLOCK, L)
    nq = L // bq
    scale = HEAD_DIM ** -0.5
    scale2 = scale * LOG2E
    q_spec, k_spec, v_spec, crow_spec, ccol_spec, g_spec, tile, stat = _attn_specs(H, L, bq)
    full = pl.BlockSpec((L, HEAD_DIM), lambda h, i: (0, h))
    dg_spec = pl.BlockSpec((None, 1, HEAD_DIM), lambda h, i: (h, 0, 0))
    dc_spec = pl.BlockSpec((None, 1, L), lambda h, i: (h, 0, 0))

    def work(q_ref, k_ref, v_ref, crow_ref, ccol_ref, g_ref, o_ref, lse_ref, don_ref,
             dq_ref, dk_ref, dv_ref, dg_ref, dc_ref, dkt_acc, dvt_acc):
        h, i = pl.program_id(0), pl.program_id(1)

        @pl.when(i == 0)
        def _():
            dkt_acc[...] = jnp.zeros_like(dkt_acc)
            dvt_acc[...] = jnp.zeros_like(dvt_acc)
            dg_ref[...] = jnp.zeros_like(dg_ref)
            dc_ref[...] = jnp.zeros_like(dc_ref)

        q = q_ref[...]
        t_pos = i * bq + lax.broadcasted_iota(jnp.int32, (bq, 1), 0)
        o_t = o_ref[...]
        d_o, dg = _rms_bwd(o_t, g_ref[...], don_ref[...])
        dg_ref[...] += jnp.sum(dg, axis=0, keepdims=True)
        d_ob = d_o.astype(BF16)
        dsum = jnp.sum(d_ob.astype(F32) * o_t, axis=-1, keepdims=True)
        q_t = q.astype(F32).T.astype(BF16)
        d_obt = d_o.T.astype(BF16)

        @pl.when(h < H)
        def _stick_breaking():
            tri = _suffix_matrix(bk)

            def step(ks, rows, carry, masked):
                c, gs, dq = carry
                k = k_ref[pl.ds(ks, rows), :]
                v = v_ref[pl.ds(ks, rows), :]
                z, sp, mask, a, c = _sb_window(q, k, t_pos, ks, bk, scale2, c, tri, masked)
                a_b = a.astype(BF16)
                g_w = a_b.astype(F32) * _dot_nt(d_ob, v)
                later, gs = _window_suffix_sums(g_w, bk, tri, gs, True)
                prefix = dsum - (later - g_w)
                dz = (g_w - jnp.exp2(z - sp) * prefix) * scale
                if masked:
                    dz = jnp.where(mask, dz, 0.0)
                dzb = dz.astype(BF16)
                dq = dq + jnp.dot(dzb, k, preferred_element_type=F32)
                dkt_acc[:, pl.ds(ks, rows)] += jnp.dot(q_t, dzb, preferred_element_type=F32)
                dvt_acc[:, pl.ds(ks, rows)] += jnp.dot(d_obt, a_b, preferred_element_type=F32)
                return c, gs, dq

            z1 = jnp.zeros((bq, 1), F32)
            _, _, dq = _key_tiles(i, bk, step, (z1, z1, jnp.zeros((bq, HEAD_DIM), F32)))
            dq_ref[...] = dq.astype(BF16)

        @pl.when(h >= H)
        def _forgetting():
            cq = ccol_ref[...] * LOG2E - lse_ref[...]

            def step(ks, rows, dq, masked):
                k = k_ref[pl.ds(ks, rows), :]
                v = v_ref[pl.ds(ks, rows), :]
                p = jnp.exp2(_dot_nt(q, k) * scale2 + (cq - crow_ref[:, pl.ds(ks, rows)] * LOG2E))
                if masked:
                    p = jnp.where(ks + lax.broadcasted_iota(jnp.int32, (1, rows), 1) <= t_pos, p, 0.0)
                ds = p * (_dot_nt(d_ob, v) - dsum)
                dsb = (ds * scale).astype(BF16)
                dq = dq + jnp.dot(dsb, k, preferred_element_type=F32)
                dkt_acc[:, pl.ds(ks, rows)] += jnp.dot(q_t, dsb, preferred_element_type=F32)
                dvt_acc[:, pl.ds(ks, rows)] += jnp.dot(d_obt, p.astype(BF16), preferred_element_type=F32)
                dc_ref[:, pl.ds(ks, rows)] += -jnp.sum(ds, axis=0, keepdims=True)
                return dq

            dq = _key_tiles(i, bk, step, jnp.zeros((bq, HEAD_DIM), F32))
            dq_ref[...] = dq.astype(BF16)

        @pl.when(i == nq - 1)
        def _():
            dk_ref[...] = dkt_acc[...].T.astype(BF16)
            dv_ref[...] = dvt_acc[...].T.astype(BF16)

    W2 = 2 * H * HEAD_DIM
    est = 4 * L * HEAD_DIM * 2 + 4 * L * HEAD_DIM * 2 + 2 * L * HEAD_DIM * 4 + 48 * bq * bk * 4 + (4 << 20)
    def body(*refs):
        h, i = pl.program_id(0), pl.program_id(1)
        _ride(rider, refs, 9, 5, (h == 0) & (i == 0), (h == H) & (i == 0), (h == 2 * H - 1) & (i == nq - 1), work)

    extra = rider or _NO_RIDER
    dq, dk, dv, dg, dc, *brought = pl.pallas_call(
        body, name=name, grid=(2 * H, nq),
        in_specs=[q_spec, k_spec, v_spec, crow_spec, ccol_spec, g_spec, tile, stat, tile] + extra.in_specs,
        out_specs=[tile, full, full, dg_spec, dc_spec] + extra.out_specs,
        out_shape=[jax.ShapeDtypeStruct((L, W2), BF16)] * 3
        + [jax.ShapeDtypeStruct((2 * H, 1, HEAD_DIM), F32), jax.ShapeDtypeStruct((2 * H, 1, L), F32)] + extra.out_shape,
        scratch_shapes=[pltpu.VMEM((HEAD_DIM, L), F32), pltpu.VMEM((HEAD_DIM, L), F32)] + extra.scratch,
        compiler_params=_params(("arbitrary", "arbitrary"), est),
    )(qkv, qkv, qkv, crow, ccol, g_heads, o, lse, d_on, *extra.blocks)
    return dq, dk, dv, dg[:, 0, :], dc, brought


HALO = 16


def _conv_tiles(L, F):
    return _row_tile(L), _divisor(F, (512, 256, 128))


SUBLANES = 8


def _shift_down(x, before, bt):
    ext = jnp.concatenate([before, x], axis=0)
    return pltpu.roll(ext, 1, 0)[SUBLANES:], pltpu.roll(ext, 2, 0)[SUBLANES:]


def _shift_up(x, after, bt):
    ext = jnp.concatenate([x, after], axis=0)
    return pltpu.roll(ext, bt + SUBLANES - 1, 0)[:bt], pltpu.roll(ext, bt + SUBLANES - 2, 0)[:bt]


def _conv_rows(p_ref, halo_ref, w_ref, b_ref, first, bt):
    p = p_ref[...].astype(F32)
    before = jnp.where(first, 0.0, halo_ref[...].astype(F32)[HALO - SUBLANES:HALO, :])
    p1, p2 = _shift_down(p, before, bt)
    w = w_ref[...]
    a = w[0:1, :] * p2 + w[1:2, :] * p1 + w[2:3, :] * p + b_ref[...]
    return a, p, p1, p2


def _sigmoid(x):
    return 0.5 + 0.5 * jnp.tanh(0.5 * x)


def _conv_in_specs(L, F, bt, bc, order):
    nf = F // bc
    r = bt // HALO
    ix = (lambda a, b: (a, b)) if order == "ij" else (lambda a, b: (b, a))

    def mk(shape, fn):
        return pl.BlockSpec(shape, lambda a, b: fn(*ix(a, b)))

    return [
        mk((bt, bc), lambda i, j: (i, j)), mk((HALO, bc), lambda i, j: (jnp.maximum(i * r - 1, 0), j)),
        mk((bt, bc), lambda i, j: (i, nf + j)), mk((HALO, bc), lambda i, j: (jnp.maximum(i * r - 1, 0), nf + j)),
        mk((3, bc), lambda i, j: (0, j)), mk((3, bc), lambda i, j: (0, nf + j)),
        mk((1, bc), lambda i, j: (0, j)), mk((1, bc), lambda i, j: (0, nf + j)),
    ]


def _convgate_fwd(p, conv_w, conv_b, name):
    L, F2 = p.shape
    F = F2 // 2
    bt, bc = _conv_tiles(L, F)

    def body(pg, hg, pu, hu, wg, wu, bg, bu, act_ref):
        first = pl.program_id(0) == 0
        ag = _conv_rows(pg, hg, wg, bg, first, bt)[0]
        au = _conv_rows(pu, hu, wu, bu, first, bt)[0]
        act_ref[...] = (ag * _sigmoid(ag) * au).astype(BF16)

    return pl.pallas_call(
        body, name=name, grid=(L // bt, F // bc), in_specs=_conv_in_specs(L, F, bt, bc, "ij"),
        out_specs=pl.BlockSpec((bt, bc), lambda i, j: (i, j)), out_shape=jax.ShapeDtypeStruct((L, F), BF16),
        compiler_params=_params(("parallel", "parallel"), 24 * bt * bc * 4),
    )(p, p, p, p, conv_w, conv_w, conv_b.reshape(1, F2), conv_b.reshape(1, F2))


def _convgate_bwd(p, conv_w, conv_b, d_act, name):
    L, F2 = p.shape
    F = F2 // 2
    bt, bc = _conv_tiles(L, F)

    def body(pg, hg, pu, hu, wg, wu, bg, bu, dact_ref, da_ref, dwb_ref):
        first = pl.program_id(1) == 0
        ag, xg, xg1, xg2 = _conv_rows(pg, hg, wg, bg, first, bt)
        au, xu, xu1, xu2 = _conv_rows(pu, hu, wu, bu, first, bt)
        d_act = dact_ref[...].astype(F32)
        sg = _sigmoid(ag)
        dag = d_act * au * sg * (1.0 + ag * (1.0 - sg))
        dau = d_act * ag * sg
        da_ref[0] = dag.astype(BF16)
        da_ref[1] = dau.astype(BF16)

        @pl.when(first)
        def _():
            dwb_ref[...] = jnp.zeros_like(dwb_ref)
        cs = lambda v: jnp.sum(v, axis=0, keepdims=True)
        dwb_ref[0] += jnp.concatenate([cs(dag * xg2), cs(dag * xg1), cs(dag * xg), cs(dag)], axis=0)
        dwb_ref[1] += jnp.concatenate([cs(dau * xu2), cs(dau * xu1), cs(dau * xu), cs(dau)], axis=0)

    da, dwb = pl.pallas_call(
        body, name=name, grid=(F // bc, L // bt),
        in_specs=_conv_in_specs(L, F, bt, bc, "ji") + [pl.BlockSpec((bt, bc), lambda j, i: (i, j))],
        out_specs=[pl.BlockSpec((2, bt, bc), lambda j, i: (0, i, j)), pl.BlockSpec((2, 4, bc), lambda j, i: (0, 0, j))],
        out_shape=[jax.ShapeDtypeStruct((2, L, F), BF16), jax.ShapeDtypeStruct((2, 4, F), F32)],
        compiler_params=_params(("parallel", "arbitrary"), 40 * bt * bc * 4),
    )(p, p, p, p, conv_w, conv_w, conv_b.reshape(1, F2), conv_b.reshape(1, F2), d_act)
    d_w = jnp.concatenate([dwb[0, 0:3], dwb[1, 0:3]], axis=1)
    d_b = jnp.concatenate([dwb[0, 3], dwb[1, 3]], axis=0)
    return da, d_w, d_b


def _conv_bwd_data(da, conv_w, name):
    _, L, F = da.shape
    bt, bc = _conv_tiles(L, F)
    nf, r, nt = F // bc, bt // HALO, L // bt

    def body(da_ref, nxt_ref, w_ref, dp_ref):
        last = pl.program_id(0) == nt - 1
        x = da_ref[...].astype(F32)
        after = jnp.where(last, 0.0, nxt_ref[...].astype(F32)[0:SUBLANES, :])
        x1, x2 = _shift_up(x, after, bt)
        w = w_ref[...]
        dp_ref[...] = (w[2:3, :] * x + w[1:2, :] * x1 + w[0:1, :] * x2).astype(BF16)

    return pl.pallas_call(
        body, name=name, grid=(nt, 2 * nf),
        in_specs=[pl.BlockSpec((None, bt, bc), lambda i, j: (j // nf, i, j % nf)),
                  pl.BlockSpec((None, HALO, bc), lambda i, j: (j // nf, jnp.minimum((i + 1) * r, nt * r - 1), j % nf)),
                  pl.BlockSpec((3, bc), lambda i, j: (0, j))],
        out_specs=pl.BlockSpec((bt, bc), lambda i, j: (i, j)), out_shape=jax.ShapeDtypeStruct((L, 2 * F), BF16),
        compiler_params=_params(("parallel", "parallel"), 16 * bt * bc * 4),
    )(da, da, conv_w)


def _pad_cols(a, n):
    return jnp.pad(a, ((0, 0), (0, n - a.shape[1])))


def _local_step(x, tgt, meta, W, hooks=None):
    S, D = x.shape
    n_meta = meta.shape[0]
    depth = len(W["w_in"])
    H = D // (2 * HEAD_DIM)
    WQ = 6 * H * HEAD_DIM
    L = -(-(S + n_meta) // ATT_BLOCK) * ATT_BLOCK
    tail = L - S - n_meta
    zeros_tail = jnp.zeros((tail, D), F32)
    h = jnp.concatenate([meta, x, zeros_tail], axis=0)
    tgt_p = jnp.concatenate([jnp.zeros((n_meta, D), F32), tgt, zeros_tail], axis=0)

    saved = []
    _, u1 = _resnorm_fwd(h, None, None, W["g_mix_pre"][0], "prenorm0")
    for l in range(depth):
        w_in = W["w_in"][l]
        w_qkv, w_f = w_in[:, :WQ], _pad_cols(w_in[:, WQ:], LANES)
        b_pad = jnp.pad(W["b_f"][l], (0, LANES - H)).reshape(1, LANES)
        g_heads = jnp.concatenate([W["g_sb"][l], W["g_fox"][l]], axis=0).reshape(2 * H, 1, HEAD_DIM)
        qkv = _mm(u1, w_qkv, "nn", BF16, f"qkv{l}")
        fl = _mm(u1, w_f, "nn", F32, f"flogit{l}")
        c = _gate_fwd(fl, b_pad, f"gate_fwd{l}")
        c_heads = c[:, :H].T
        crow, ccol = c_heads[:, None, :], c_heads[:, :, None]
        o, on, lse, brought = _attn_fwd(qkv, crow, ccol, g_heads, H, f"attn_fwd{l}", hooks and hooks.fwd_rider(l))
        if hooks:
            hooks.fwd_done(l, brought, W)
        mix = _mm(on, W["w_out"][l], "nn", F32, f"mix{l}")
        h_mid, u2 = _resnorm_fwd(h, mix, W["g_mix_post"][l], W["g_ffn_pre"][l], f"resnorm_a{l}")
        p = _mm(u2, W["w_up"][l], "nn", BF16, f"up{l}")
        act = _convgate_fwd(p, W["conv_w"][l], W["conv_b"][l], f"convgate{l}")
        ff = _mm(act, W["w_down"][l], "nn", F32, f"down{l}")
        g_next = W["g_mix_pre"][l + 1] if l + 1 < depth else None
        h_out, u1_next = _resnorm_fwd(h_mid, ff, W["g_ffn_post"][l], g_next, f"resnorm_b{l}")
        saved.append(dict(h_in=h, u1=u1, w_qkv=w_qkv, w_f=w_f, b_pad=b_pad, g_heads=g_heads, qkv=qkv, fl=fl, crow=crow,
                          ccol=ccol, o=o, on=on, lse=lse, mix=mix, h_mid=h_mid, u2=u2, p=p, act=act, ff=ff, h_out=h_out))
        h, u1 = h_out, u1_next

    loss, dh = _loss(h, tgt_p, n_meta, S, "loss")

    grads = {k: [None] * depth for k in ("g_mix_pre", "w_in", "b_f", "g_sb", "g_fox", "w_out", "g_mix_post", "g_ffn_pre",
                                         "w_up", "conv_w", "conv_b", "w_down", "g_ffn_post")}
    du1_next = None
    for l in reversed(range(depth)):
        s = saved[l]
        g_next = W["g_mix_pre"][l + 1] if l + 1 < depth else None
        dh, d_ff, grads["g_ffn_post"][l], dg_pre_next = _resnorm_bwd(
            dh, du1_next, s["h_out"], s["ff"], W["g_ffn_post"][l], g_next, f"resnorm_b_bwd{l}")
        if l + 1 < depth:
            grads["g_mix_pre"][l + 1] = dg_pre_next
        d_act = _mm(d_ff, W["w_down"][l], "nt", BF16, f"d_act{l}")
        grads["w_down"][l] = _mm(s["act"], d_ff, "tn", BF16, f"dw_down{l}")
        da, grads["conv_w"][l], grads["conv_b"][l] = _convgate_bwd(s["p"], W["conv_w"][l], W["conv_b"][l], d_act, f"convgate_bwd{l}")
        dp = _conv_bwd_data(da, W["conv_w"][l], f"conv_bwd{l}")
        du2 = _mm(dp, W["w_up"][l], "nt", F32, f"d_u2{l}")
        grads["w_up"][l] = _mm(s["u2"], dp, "tn", BF16, f"dw_up{l}")
        dh, d_mix, grads["g_mix_post"][l], grads["g_ffn_pre"][l] = _resnorm_bwd(
            dh, du2, s["h_mid"], s["mix"], W["g_mix_post"][l], W["g_ffn_pre"][l], f"resnorm_a_bwd{l}")
        d_on = _mm(d_mix, W["w_out"][l], "nt", F32, f"d_on{l}")
        grads["w_out"][l] = _mm(s["on"], d_mix, "tn", BF16, f"dw_out{l}")
        dq, dk, dv, dg_heads, dcrow, brought = _attn_bwd(s["qkv"], s["crow"], s["ccol"], s["g_heads"], s["o"], s["lse"], d_on, H,
                                                         f"attn_bwd{l}", hooks and hooks.bwd_rider(l, grads))
        if hooks:
            hooks.bwd_done(l, brought)
        grads["g_sb"][l], grads["g_fox"][l] = dg_heads[:H], dg_heads[H:]
        dc = _pad_cols(dcrow[H:, 0, :].T, LANES)
        dfl, db = _gate_bwd(dc, s["fl"], s["b_pad"], f"gate_bwd{l}")
        grads["b_f"][l] = db[:H]
        Wh = H * HEAD_DIM
        d_qkv = jnp.concatenate([dq[:, :Wh], dk[:, :Wh], dv[:, :Wh], dq[:, Wh:], dk[:, Wh:], dv[:, Wh:]], axis=1)
        du1_next = _mm(d_qkv, s["w_qkv"], "nt", F32, f"d_u1{l}", a2=dfl, b2=s["w_f"])
        dw_qkv = _mm(s["u1"], d_qkv, "tn", BF16, f"dw_qkv{l}")
        dw_f = _mm(s["u1"], dfl, "tn", BF16, f"dw_f{l}")
        grads["w_in"][l] = jnp.concatenate([dw_qkv, dw_f[:, :H]], axis=1)
    dh0, _, _, grads["g_mix_pre"][0] = _resnorm_bwd(dh, du1_next, saved[0]["h_in"], None, None, W["g_mix_pre"][0], "prenorm0_bwd")
    return loss, dh0[n_meta:n_meta + S], dh0[:n_meta], grads


def _all_gather(x, name):
    def body(x_ref, out_ref, send_sems, recv_sems, local_sem):
        mx, my, mc = lax.axis_index("x"), lax.axis_index("y"), lax.axis_index("c")
        me, sibling = (mx, my, mc), (mx, my, 1 - mc)
        chips = [(1 - mx, my), (mx, 1 - my), (1 - mx, 1 - my)]

        def slot(px, py, pc):
            return out_ref.at[4 * px + 2 * py + pc]

        def copy(k, block, to, src=None):
            return pltpu.make_async_remote_copy(
                src_ref=slot(*block) if src is None else src, dst_ref=slot(*block),
                send_sem=send_sems.at[k], recv_sem=recv_sems.at[k], device_id=to, device_id_type=_MESH)

        mine = pltpu.make_async_copy(x_ref, slot(*me), local_sem)
        mine.start()
        first = [copy(0, me, sibling, src=x_ref)]
        first += [copy(1 + j, me, (*chip, mc), src=x_ref) for j, chip in enumerate(chips)]
        for cp in first:
            cp.start()
        passed = [copy(4 + j, (*chip, mc), sibling) for j, chip in enumerate(chips)]
        for j, chip in enumerate(chips):
            copy(1 + j, (*chip, mc), me).wait_recv()
            passed[j].start()
        copy(0, sibling, me).wait_recv()
        for j, chip in enumerate(chips):
            copy(4 + j, (*chip, 1 - mc), me).wait_recv()
        for cp in first + passed:
            cp.wait_send()
        mine.wait()

    return pl.pallas_call(
        body, name=name, out_shape=jax.ShapeDtypeStruct((N_DEV, *x.shape), x.dtype),
        in_specs=[_ANY], out_specs=_ANY,
        scratch_shapes=[pltpu.SemaphoreType.DMA((7,)), pltpu.SemaphoreType.DMA((7,)), pltpu.SemaphoreType.DMA],
    )(x)


def _exchange(g, name):
    def body(g_ref, land_ref, send_sems, recv_sems, local_sem):
        mx, my, mc = lax.axis_index("x"), lax.axis_index("y"), lax.axis_index("c")
        me = 4 * mx + 2 * my + mc
        local = pltpu.make_async_copy(g_ref.at[me], land_ref.at[me], local_sem)
        local.start()
        sends, recvs = [], []
        for k in range(1, N_DEV):
            peer = (_flip(mx, k & 4), _flip(my, k & 2), _flip(mc, k & 1))
            p = 4 * peer[0] + 2 * peer[1] + peer[2]
            sends.append(pltpu.make_async_remote_copy(
                src_ref=g_ref.at[p], dst_ref=land_ref.at[me], send_sem=send_sems.at[k - 1], recv_sem=recv_sems.at[k - 1],
                device_id=peer, device_id_type=_MESH))
            recvs.append(pltpu.make_async_remote_copy(
                src_ref=g_ref.at[p], dst_ref=land_ref.at[p], send_sem=send_sems.at[k - 1], recv_sem=recv_sems.at[k - 1],
                device_id=peer, device_id_type=_MESH))
        for cp in sends:
            cp.start()
        for cp in recvs:
            cp.wait_recv()
        for cp in sends:
            cp.wait_send()
        local.wait()

    return pl.pallas_call(
        body, name=name, out_shape=jax.ShapeDtypeStruct(g.shape, g.dtype), in_specs=[_ANY], out_specs=_ANY,
        scratch_shapes=[pltpu.SemaphoreType.DMA((7,)), pltpu.SemaphoreType.DMA((7,)), pltpu.SemaphoreType.DMA],
    )(g)


def _exchange_cores(g, name):
    def body(g_ref, land_ref, send_sems, recv_sems):
        mx, my, mc = lax.axis_index("x"), lax.axis_index("y"), lax.axis_index("c")
        copies = [pltpu.make_async_remote_copy(
            src_ref=g_ref.at[2 * q + (1 - mc)], dst_ref=land_ref.at[q], send_sem=send_sems.at[q], recv_sem=recv_sems.at[q],
            device_id=(mx, my, 1 - mc), device_id_type=_MESH) for q in range(4)]
        for cp in copies:
            cp.start()
        for cp in copies:
            cp.wait_recv()
        for cp in copies:
            cp.wait_send()

    return pl.pallas_call(
        body, name=name, out_shape=jax.ShapeDtypeStruct((4, *g.shape[1:]), g.dtype), in_specs=[_ANY], out_specs=_ANY,
        scratch_shapes=[pltpu.SemaphoreType.DMA((4,)), pltpu.SemaphoreType.DMA((4,))],
    )(g)


def _pair_sum(g, land, name):
    _, n, R, C = g.shape
    tr = _divisor(R, (128, 64, 32, 16, 8))
    core = lax.axis_index("c").astype(jnp.int32).reshape(1)

    def body(c_ref, g_ref, land_ref, o_ref):
        o_ref[...] = (g_ref[...].astype(F32) + land_ref[...].astype(F32)).astype(o_ref.dtype)

    blk = pl.BlockSpec((None, None, tr, C), lambda q, l, r, c_ref: (q, l, r, 0))
    return pl.pallas_call(
        body, name=name, out_shape=jax.ShapeDtypeStruct(land.shape, g.dtype),
        grid_spec=pltpu.PrefetchScalarGridSpec(
            num_scalar_prefetch=1, grid=(4, n, R // tr),
            in_specs=[pl.BlockSpec((None, None, tr, C), lambda q, l, r, c_ref: (2 * q + c_ref[0], l, r, 0)), blk],
            out_specs=blk),
        compiler_params=_params(("parallel", "parallel", "parallel"), 8 * tr * C * 4),
    )(core, g, land)


def _exchange_chips(part, name):
    def body(p_ref, land_ref, send_sems, recv_sems, local_sem):
        mx, my, mc = lax.axis_index("x"), lax.axis_index("y"), lax.axis_index("c")
        me = 2 * mx + my
        local = pltpu.make_async_copy(p_ref.at[me], land_ref.at[me], local_sem)
        local.start()
        sends, recvs = [], []
        for k in range(1, 4):
            px, py = _flip(mx, k & 2), _flip(my, k & 1)
            p = 2 * px + py
            sends.append(pltpu.make_async_remote_copy(
                src_ref=p_ref.at[p], dst_ref=land_ref.at[me], send_sem=send_sems.at[k - 1], recv_sem=recv_sems.at[k - 1],
                device_id=(px, py, mc), device_id_type=_MESH))
            recvs.append(pltpu.make_async_remote_copy(
                src_ref=p_ref.at[p], dst_ref=land_ref.at[p], send_sem=send_sems.at[k - 1], recv_sem=recv_sems.at[k - 1],
                device_id=(px, py, mc), device_id_type=_MESH))
        for cp in sends:
            cp.start()
        for cp in recvs:
            cp.wait_recv()
        for cp in sends:
            cp.wait_send()
        local.wait()

    return pl.pallas_call(
        body, name=name, out_shape=jax.ShapeDtypeStruct(part.shape, part.dtype), in_specs=[_ANY], out_specs=_ANY,
        scratch_shapes=[pltpu.SemaphoreType.DMA((3,)), pltpu.SemaphoreType.DMA((3,)), pltpu.SemaphoreType.DMA],
    )(part)


def _adamw_landed(lands, w, m, v, name):
    n, R, C = w.shape
    tr = _divisor(R, (64, 32, 16, 8))
    nr = R // tr
    blk = pl.BlockSpec((None, tr, C), lambda l, r: (l, r, 0))

    def land_spec(k, S):
        return pl.BlockSpec((S, tr, C), lambda l, r: (0, jnp.where(l == k, r, jnp.where(l < k, 0, nr - 1)), 0))

    def body(*refs):
        land_refs = refs[:n]
        w_ref, m_ref, v_ref, g_ref, d_ref, mo_ref, vo_ref = refs[n:]
        for k in range(n):
            @pl.when(pl.program_id(0) == k)
            def _(k=k):
                g = land_refs[k][0].astype(F32)
                for s in range(1, lands[k].shape[0]):
                    g = g + land_refs[k][s].astype(F32)
                m_new = ADAM_B1 * m_ref[...] + (1.0 - ADAM_B1) * g
                v_new = ADAM_B2 * v_ref[...] + (1.0 - ADAM_B2) * (g * g)
                m_hat = m_new / (1.0 - ADAM_B1 ** ADAM_STEP)
                v_hat = v_new / (1.0 - ADAM_B2 ** ADAM_STEP)
                g_ref[...] = g
                d_ref[...] = -ADAM_LR * (m_hat / (jnp.sqrt(v_hat) + ADAM_EPS) + ADAM_WD * w_ref[...])
                mo_ref[...] = m_new
                vo_ref[...] = v_new

    est = 2 * tr * C * (sum(x.shape[0] * jnp.dtype(x.dtype).itemsize for x in lands) + 7 * 4) * 9 // 8
    return pl.pallas_call(
        body, name=name, grid=(n, nr),
        in_specs=[land_spec(k, x.shape[0]) for k, x in enumerate(lands)] + [blk, blk, blk],
        out_specs=[blk] * 4, out_shape=[jax.ShapeDtypeStruct((n, R, C), F32)] * 4,
        compiler_params=_params(("arbitrary", "arbitrary"), est),
    )(*lands, w, m, v)


def _pack(arrs):
    flat = jnp.concatenate([a.reshape(-1).astype(F32) for a in arrs])
    rows = -(-flat.shape[0] // (8 * LANES)) * 8
    return jnp.pad(flat, (0, rows * LANES - flat.shape[0])).reshape(rows, LANES)


def _unpack(packed, shapes):
    flat, out, at = packed.reshape(-1), [], 0
    for s in shapes:
        n = math.prod(s)
        out.append(flat[at:at + n].reshape(s))
        at += n
    return out


_BIG = ("w_in", "w_out", "w_up", "w_down")
_REPLICATED = ("g_mix_pre", "b_f", "g_sb", "g_fox", "g_mix_post", "g_ffn_pre", "conv_b", "g_ffn_post")
_ORDER = ("meta", "g_mix_pre", "w_in", "b_f", "g_sb", "g_fox", "w_out", "g_mix_post", "g_ffn_pre", "w_up", "conv_w",
          "conv_b", "w_down", "g_ffn_post")
_COLUMN_SHARDED = ("w_in", "w_up")


def kernel(x, meta, g_mix_pre, w_in, b_f, g_sb, g_fox, w_out, g_mix_post, g_ffn_pre, w_up, conv_w, conv_b, w_down, g_ffn_post, loss_target, m_meta, m_g_mix_pre, m_w_in, m_b_f, m_g_sb, m_g_fox, m_w_out, m_g_mix_post, m_g_ffn_pre, m_w_up, m_conv_w, m_conv_b, m_w_down, m_g_ffn_post, v_meta, v_g_mix_pre, v_w_in, v_b_f, v_g_sb, v_g_fox, v_w_out, v_g_mix_post, v_g_ffn_pre, v_w_up, v_conv_w, v_conv_b, v_w_down, v_g_ffn_post):
    w = dict(meta=meta, g_mix_pre=g_mix_pre, w_in=w_in, b_f=b_f, g_sb=g_sb, g_fox=g_fox, w_out=w_out, g_mix_post=g_mix_post,
             g_ffn_pre=g_ffn_pre, w_up=w_up, conv_w=conv_w, conv_b=conv_b, w_down=w_down, g_ffn_post=g_ffn_post)
    m = dict(meta=m_meta, g_mix_pre=m_g_mix_pre, w_in=m_w_in, b_f=m_b_f, g_sb=m_g_sb, g_fox=m_g_fox, w_out=m_w_out,
             g_mix_post=m_g_mix_post, g_ffn_pre=m_g_ffn_pre, w_up=m_w_up, conv_w=m_conv_w, conv_b=m_conv_b, w_down=m_w_down,
             g_ffn_post=m_g_ffn_post)
    v = dict(meta=v_meta, g_mix_pre=v_g_mix_pre, w_in=v_w_in, b_f=v_b_f, g_sb=v_g_sb, g_fox=v_g_fox, w_out=v_w_out,
             g_mix_post=v_g_mix_post, g_ffn_pre=v_g_ffn_pre, w_up=v_w_up, conv_w=v_conv_w, conv_b=v_conv_b, w_down=v_w_down,
             g_ffn_post=v_g_ffn_post)
    depth = w_in.shape[0]
    shards = {name: [w[name][l].astype(BF16) for l in range(depth)] for name in _BIG}

    def assemble(name, g):
        if name in _COLUMN_SHARDED:
            return jnp.transpose(g, (1, 0, 2)).reshape(g.shape[1], -1)
        return g.reshape(-1, g.shape[2])

    def blocks_of(name, grad):
        R, C = w[name].shape[1:]
        if name in _COLUMN_SHARDED:
            return jnp.transpose(grad.reshape(R, N_DEV, C), (1, 0, 2)).astype(BF16)
        return grad.reshape(N_DEV, R, C).astype(BF16)

    def side_by_side(name):
        return name in _COLUMN_SHARDED and w[name].shape[2] % LANES == 0

    class Hooks:
        def __init__(self):
            self.landed = {}

        def fwd_keys(self, l):
            return ([("w_in", l + 1)] if l + 1 < depth else []) + [("w_out", l), ("w_up", l), ("w_down", l)]

        def bwd_keys(self, l):
            return ([("w_in", l + 1)] if l + 1 < depth else []) + [("w_down", l), ("w_up", l), ("w_out", l)]

        def fwd_rider(self, l):
            keys = self.fwd_keys(l)
            return _GatherRider([shards[name][ll] for name, ll in keys], [side_by_side(name) for name, _ in keys])

        def fwd_done(self, l, brought, W):
            for (name, ll), g in zip(self.fwd_keys(l), brought):
                W[name][ll] = g if side_by_side(name) else assemble(name, g)

        def bwd_rider(self, l, grads):
            keys = self.bwd_keys(l)
            return _ExchangeRider([grads[name][ll] if side_by_side(name) else blocks_of(name, grads[name][ll]) for name, ll in keys],
                                  [side_by_side(name) for name, _ in keys])

        def bwd_done(self, l, brought):
            for key, land in zip(self.bwd_keys(l), brought):
                self.landed[key] = land

    small_shapes = [conv_w.shape, meta.shape]
    gs = _all_gather(_pack([conv_w, meta]), "gather_small")
    parts = [_unpack(gs[d], small_shapes) for d in range(N_DEV)]
    conv_full = jnp.concatenate([p[0] for p in parts], axis=2)
    meta_full = jnp.concatenate([p[1] for p in parts], axis=1)

    W = {name: [None] * depth for name in _BIG}
    W["w_in"][0] = assemble("w_in", _all_gather(shards["w_in"][0], "gather_w_in0"))
    W["conv_w"] = [conv_full[l] for l in range(depth)]
    for name in _REPLICATED:
        W[name] = [w[name][l] for l in range(depth)]
    hooks = Hooks()
    loss, grad_x, d_meta, grads = _local_step(x[0], loss_target[0], meta_full, W, hooks)

    first = blocks_of("w_in", grads["w_in"][0])[:, None]
    pairs = _pair_sum(first, _exchange_cores(first, "exchange_cores_w_in0"), "pair_sum_w_in0")
    hooks.landed[("w_in", 0)] = _exchange_chips(pairs, "exchange_chips_w_in0")[:, 0]
    out = {}
    for name in _BIG:
        out[name] = _adamw_landed([hooks.landed[(name, l)] for l in range(depth)], w[name], m[name], v[name], f"adamw_{name}")

    Fs, Ms = conv_w.shape[2], meta.shape[1]
    d_conv = jnp.stack(grads["conv_w"], axis=0)
    blocks = jnp.stack([_pack([d_conv[:, :, d * Fs:(d + 1) * Fs], d_meta[:, d * Ms:(d + 1) * Ms]]) for d in range(N_DEV)])
    land = _exchange(blocks, "exchange_small")
    res = _adamw_landed([land], _pack([conv_w, meta])[None], _pack([m_conv_w, m_meta])[None],
                        _pack([v_conv_w, v_meta])[None], "adamw_small")
    for i, r in enumerate(res):
        cw, mt = _unpack(r[0], small_shapes)
        out.setdefault("conv_w", [None] * 4)[i] = cw
        out.setdefault("meta", [None] * 4)[i] = mt

    rep_shapes = [()] + [w[name].shape for name in _REPLICATED]
    mine = _pack([loss] + [jnp.stack(grads[name], axis=0) for name in _REPLICATED])
    land = _all_gather(mine, "gather_replicated")
    zero = jnp.zeros((), F32)
    res = _adamw_landed([land], _pack([zero] + [w[n] for n in _REPLICATED])[None],
                        _pack([zero] + [m[n] for n in _REPLICATED])[None],
                        _pack([zero + 1.0] + [v[n] for n in _REPLICATED])[None], "adamw_replicated")
    for i, r in enumerate(res):
        vals = _unpack(r[0], rep_shapes)
        if i == 0:
            loss_total = vals[0]
        for name, val in zip(_REPLICATED, vals[1:]):
            out.setdefault(name, [None] * 4)[i] = val

    return (loss_total, grad_x[None], *[out[n][0] for n in _ORDER], *[out[n][1] for n in _ORDER],
            *[out[n][2] for n in _ORDER], *[out[n][3] for n in _ORDER])
```

```python
import functools
import math

import jax
import jax.numpy as jnp
from jax import lax
from jax.experimental import pallas as pl
from jax.experimental.pallas import tpu as pltpu

F32, BF16 = jnp.float32, jnp.bfloat16
HEAD_DIM = 128
LANES = 128
EPS = 1e-6
NEG_INF = -1e30
ATT_BLOCK = 256
N_DEV = 8
V7X_VMEM_BUDGET = 56 * 1024 * 1024

ADAM_LR, ADAM_B1, ADAM_B2, ADAM_EPS, ADAM_WD, ADAM_STEP = 0.001, 0.9, 0.999, 1e-08, 0.01, 10


def _divisor(n, cands):
    for c in cands:
        if c <= n and n % c == 0:
            return c
    raise ValueError(f"no tile for {n} among {cands}")


def _params(sem, est_bytes):
    limit = int(min(V7X_VMEM_BUDGET, max(16 * 1024 * 1024, est_bytes * 5 // 4 + (2 << 20))))
    return pltpu.CompilerParams(dimension_semantics=sem, vmem_limit_bytes=limit)


def _nbytes(shape, dtype):
    return math.prod(shape) * jnp.dtype(dtype).itemsize


_DN = {"nn": (((1,), (0,)), ((), ())), "nt": (((1,), (1,)), ((), ())), "tn": (((0,), (0,)), ((), ()))}
_ROW_TILES = (1088, 544, 272, 512, 256, 128, 64, 32, 16, 8)
_COL_TILES = (1024, 512, 256, 128)
MM_VMEM_BLOCKS = 40 * 1024 * 1024


def _mm_tiles(M, N, K, rows, out_bytes):
    best = None
    for tm in rows:
        for tn in _COL_TILES:
            if tm > M or tn > N or M % tm or N % tn:
                continue
            if 4 * (tm + tn) * K + 2 * tm * tn * out_bytes > MM_VMEM_BLOCKS:
                continue
            if best is None or tm * tn > best[0] * best[1]:
                best = (tm, tn, K)
    if best is not None:
        return best
    return _divisor(M, rows), _divisor(N, _COL_TILES[1:]), _ktile(K, 3072)


def _ktile(k, cap):
    if k <= cap:
        return k
    for t in range(cap - cap % LANES, 0, -LANES):
        if k % t == 0:
            return t
    raise ValueError(k)


def _mm(a, b, mode, out_dtype, name, a2=None, b2=None):
    if mode == "nn":
        (M, K), (_, N) = a.shape, b.shape
    elif mode == "nt":
        (M, K), (N, _) = a.shape, b.shape
    else:
        (K, M), (_, N) = a.shape, b.shape
    tm, tn, tk = _mm_tiles(M, N, K, _COL_TILES if mode == "tn" else _ROW_TILES, jnp.dtype(out_dtype).itemsize)
    nk = K // tk
    a_spec = {"nn": pl.BlockSpec((tm, tk), lambda i, j, k: (i, k)),
              "nt": pl.BlockSpec((tm, tk), lambda i, j, k: (i, k)),
              "tn": pl.BlockSpec((tk, tm), lambda i, j, k: (k, i))}[mode]
    b_spec = {"nn": pl.BlockSpec((tk, tn), lambda i, j, k: (k, j)),
              "nt": pl.BlockSpec((tn, tk), lambda i, j, k: (j, k)),
              "tn": pl.BlockSpec((tk, tn), lambda i, j, k: (k, j))}[mode]
    dn = _DN[mode]
    extra = a2 is not None
    in_specs, args = [a_spec, b_spec], [a, b]
    if extra:
        k2 = a2.shape[1]
        in_specs += [pl.BlockSpec((tm, k2), lambda i, j, k: (i, 0)), pl.BlockSpec((tn, k2), lambda i, j, k: (j, 0))]
        args += [a2, b2]

    def body(*refs):
        if extra:
            a_ref, b_ref, a2_ref, b2_ref, o_ref, acc = refs
        else:
            a_ref, b_ref, o_ref, acc = refs
        part = lax.dot_general(a_ref[...], b_ref[...], dn, preferred_element_type=F32)
        if nk == 1:
            if extra:
                part = part + lax.dot_general(a2_ref[...], b2_ref[...], _DN["nt"], preferred_element_type=F32)
            o_ref[...] = part.astype(o_ref.dtype)
            return
        kk = pl.program_id(2)

        @pl.when(kk == 0)
        def _():
            if extra:
                acc[...] = part + lax.dot_general(a2_ref[...], b2_ref[...], _DN["nt"], preferred_element_type=F32)
            else:
                acc[...] = part

        @pl.when(kk > 0)
        def _():
            acc[...] += part

        @pl.when(kk == nk - 1)
        def _():
            o_ref[...] = acc[...].astype(o_ref.dtype)

    est = 2 * (tm * tk + tk * tn) * 2 + 2 * _nbytes((tm, tn), out_dtype) + (tm * tn * 4 if nk > 1 else 0)
    return pl.pallas_call(
        body, name=name, grid=(M // tm, N // tn, nk), in_specs=in_specs,
        out_specs=pl.BlockSpec((tm, tn), lambda i, j, k: (i, j)),
        out_shape=jax.ShapeDtypeStruct((M, N), out_dtype),
        scratch_shapes=[pltpu.VMEM((tm, tn) if nk > 1 else (8, LANES), F32)],
        compiler_params=_params(("parallel", "parallel", "arbitrary"), est),
    )(*args)


def _rms(x, g):
    r = lax.rsqrt(jnp.mean(x * x, axis=-1, keepdims=True) + EPS)
    return x * r * g


def _rms_bwd(x, g, dout):
    r = lax.rsqrt(jnp.mean(x * x, axis=-1, keepdims=True) + EPS)
    xhat = x * r
    dxh = dout * g
    dx = r * (dxh - xhat * jnp.mean(dxh * xhat, axis=-1, keepdims=True))
    return dx, dout * xhat


def _row_tile(L):
    return _divisor(L, (272, 256, 128, 64, 32, 16))


def _resnorm_fwd(h, y, g_post, g_pre, name):
    L, D = h.shape
    bt = _row_tile(L)
    has_y, has_pre = y is not None, g_pre is not None
    row = pl.BlockSpec((bt, D), lambda i: (i, 0))
    vec = pl.BlockSpec((1, D), lambda i: (0, 0))
    args, in_specs = [h], [row]
    if has_y:
        args += [y, g_post.reshape(1, D)]
        in_specs += [row, vec]
    if has_pre:
        args += [g_pre.reshape(1, D)]
        in_specs += [vec]
    out_shape, out_specs = [], []
    if has_y:
        out_shape.append(jax.ShapeDtypeStruct((L, D), F32))
        out_specs.append(row)
    if has_pre:
        out_shape.append(jax.ShapeDtypeStruct((L, D), BF16))
        out_specs.append(row)

    def body(*refs):
        refs = list(refs)
        h_ref = refs.pop(0)
        hn = h_ref[...]
        if has_y:
            y_ref, gp_ref = refs.pop(0), refs.pop(0)
            hn = hn + _rms(y_ref[...], gp_ref[...])
        if has_pre:
            g_ref = refs.pop(0)
        if has_y:
            refs.pop(0)[...] = hn
        if has_pre:
            refs.pop(0)[...] = _rms(hn, g_ref[...]).astype(BF16)

    outs = pl.pallas_call(
        body, name=name, grid=(L // bt,), in_specs=in_specs, out_specs=out_specs, out_shape=out_shape,
        compiler_params=_params(("parallel",), 10 * bt * D * 4),
    )(*args)
    outs = list(outs)
    h_new = outs.pop(0) if has_y else h
    u = outs.pop(0) if has_pre else None
    return h_new, u


def _resnorm_bwd(dh_direct, du, h_new, y, g_post, g_pre, name):
    L, D = dh_direct.shape
    bt = _row_tile(L)
    has_y, has_pre = y is not None, du is not None
    row = pl.BlockSpec((bt, D), lambda i: (i, 0))
    vec = pl.BlockSpec((1, D), lambda i: (0, 0))
    acc = pl.BlockSpec((8, D), lambda i: (0, 0))
    args, in_specs = [dh_direct], [row]
    if has_pre:
        args += [du, h_new, g_pre.reshape(1, D)]
        in_specs += [row, row, vec]
    if has_y:
        args += [y, g_post.reshape(1, D)]
        in_specs += [row, vec]
    out_shape, out_specs = [], []
    if has_pre:
        out_shape += [jax.ShapeDtypeStruct((L, D), F32), jax.ShapeDtypeStruct((8, D), F32)]
        out_specs += [row, acc]
    if has_y:
        out_shape += [jax.ShapeDtypeStruct((L, D), BF16), jax.ShapeDtypeStruct((8, D), F32)]
        out_specs += [row, acc]

    def colsum8(v):
        return jnp.sum(v.reshape(bt // 8, 8, D), axis=0)

    def body(*refs):
        refs = list(refs)
        first = pl.program_id(0) == 0
        dh = refs.pop(0)[...]
        if has_pre:
            du_ref, hn_ref, g_ref = refs.pop(0), refs.pop(0), refs.pop(0)
        if has_y:
            y_ref, gp_ref = refs.pop(0), refs.pop(0)
        if has_pre:
            dh_ref, dgpre_ref = refs.pop(0), refs.pop(0)
            dx, dgp = _rms_bwd(hn_ref[...], g_ref[...], du_ref[...].astype(F32))
            dh = dh + dx
            dh_ref[...] = dh

            @pl.when(first)
            def _():
                dgpre_ref[...] = jnp.zeros_like(dgpre_ref)
            dgpre_ref[...] += colsum8(dgp)
        if has_y:
            dy_ref, dgpost_ref = refs.pop(0), refs.pop(0)
            dy, dgq = _rms_bwd(y_ref[...], gp_ref[...], dh)
            dy_ref[...] = dy.astype(BF16)

            @pl.when(first)
            def _():
                dgpost_ref[...] = jnp.zeros_like(dgpost_ref)
            dgpost_ref[...] += colsum8(dgq)

    outs = list(pl.pallas_call(
        body, name=name, grid=(L // bt,), in_specs=in_specs, out_specs=out_specs, out_shape=out_shape,
        compiler_params=_params(("arbitrary",), 14 * bt * D * 4),
    )(*args))
    dh, dg_pre, dy, dg_post = dh_direct, None, None, None
    if has_pre:
        dh, dg_pre = outs.pop(0), outs.pop(0).sum(0)
    if has_y:
        dy, dg_post = outs.pop(0), outs.pop(0).sum(0)
    return dh, dy, dg_post, dg_pre


def _loss(h, tgt, n_meta, seq, name):
    L, D = h.shape
    bt = _row_tile(L)
    row = pl.BlockSpec((bt, D), lambda i: (i, 0))

    def body(h_ref, t_ref, dy_ref, loss_ref):
        i = pl.program_id(0)
        r = i * bt + lax.broadcasted_iota(jnp.int32, (bt, 1), 0)
        valid = (r >= n_meta) & (r < n_meta + seq)
        e = jnp.where(valid, h_ref[...] - t_ref[...], 0.0)
        dy_ref[...] = e * (1.0 / D)

        @pl.when(i == 0)
        def _():
            loss_ref[...] = jnp.zeros_like(loss_ref)
        loss_ref[...] += 0.5 * jnp.sum(jnp.sum(e * e, axis=-1, keepdims=True) * (1.0 / D))

    dy, loss = pl.pallas_call(
        body, name=name, grid=(L // bt,), in_specs=[row, row],
        out_specs=[row, pl.BlockSpec((8, LANES), lambda i: (0, 0))],
        out_shape=[jax.ShapeDtypeStruct((L, D), F32), jax.ShapeDtypeStruct((8, LANES), F32)],
        compiler_params=_params(("arbitrary",), 8 * bt * D * 4),
    )(h, tgt)
    return loss[0, 0], dy


def _split3(x):
    x1 = x.astype(BF16)
    r1 = x - x1.astype(F32)
    x2 = r1.astype(BF16)
    x3 = (r1 - x2.astype(F32)).astype(BF16)
    return x1, x2, x3


def _tri_dot3(tri, x):
    x1, x2, x3 = _split3(x)
    d = functools.partial(jnp.dot, preferred_element_type=F32)
    return d(tri, x1) + d(tri, x2) + d(tri, x3)


def _gate_fwd(fl, b_pad, name):
    L = fl.shape[0]
    bt = 128
    blk = pl.BlockSpec((bt, LANES), lambda i: (i, 0))

    def body(fl_ref, b_ref, c_ref, carry):
        @pl.when(pl.program_id(0) == 0)
        def _():
            carry[...] = jnp.zeros_like(carry)
        x = fl_ref[...] + b_ref[...]
        lf = jnp.minimum(x, 0.0) - jnp.log(1.0 + jnp.exp(-jnp.abs(x)))
        r = lax.broadcasted_iota(jnp.int32, (bt, bt), 0)
        s = lax.broadcasted_iota(jnp.int32, (bt, bt), 1)
        tri = (s <= r).astype(BF16)
        c = _tri_dot3(tri, lf) + carry[...]
        c_ref[...] = c
        carry[...] = c[bt - 1:bt, :]

    return pl.pallas_call(
        body, name=name, grid=(L // bt,), in_specs=[blk, pl.BlockSpec((1, LANES), lambda i: (0, 0))],
        out_specs=blk, out_shape=jax.ShapeDtypeStruct((L, LANES), F32),
        scratch_shapes=[pltpu.VMEM((1, LANES), F32)],
        compiler_params=_params(("arbitrary",), 1 << 20),
    )(fl, b_pad)


def _gate_bwd(dc, fl, b_pad, name):
    L = fl.shape[0]
    bt = 128
    n = L // bt
    blk = pl.BlockSpec((bt, LANES), lambda i: (n - 1 - i, 0))

    def body(dc_ref, fl_ref, b_ref, dfl_ref, db_ref, carry):
        @pl.when(pl.program_id(0) == 0)
        def _():
            carry[...] = jnp.zeros_like(carry)
            db_ref[...] = jnp.zeros_like(db_ref)
        r = lax.broadcasted_iota(jnp.int32, (bt, bt), 0)
        s = lax.broadcasted_iota(jnp.int32, (bt, bt), 1)
        tri = (s >= r).astype(BF16)
        dlf = _tri_dot3(tri, dc_ref[...]) + carry[...]
        carry[...] = dlf[0:1, :]
        x = fl_ref[...] + b_ref[...]
        dfl = dlf / (1.0 + jnp.exp(x))
        dfl_ref[...] = dfl.astype(BF16)
        db_ref[...] += jnp.sum(dfl.reshape(bt // 8, 8, LANES), axis=0)

    dfl, db = pl.pallas_call(
        body, name=name, grid=(n,), in_specs=[blk, blk, pl.BlockSpec((1, LANES), lambda i: (0, 0))],
        out_specs=[blk, pl.BlockSpec((8, LANES), lambda i: (0, 0))],
        out_shape=[jax.ShapeDtypeStruct((L, LANES), BF16), jax.ShapeDtypeStruct((8, LANES), F32)],
        scratch_shapes=[pltpu.VMEM((1, LANES), F32)],
        compiler_params=_params(("arbitrary",), 1 << 20),
    )(dc, fl, b_pad)
    return dfl, db.sum(0)


_MESH = pl.DeviceIdType.MESH
_ANY = pl.BlockSpec(memory_space=pl.ANY)


def _flip(v, bit):
    return 1 - v if bit else v


class _GatherRider:
    def __init__(self, blocks, side_by_side):
        self.blocks = list(blocks)
        self.side_by_side = list(side_by_side)
        self.n = len(self.blocks)
        self.in_specs = [_ANY] * self.n
        self.out_specs = [_ANY] * self.n
        self.out_shape = [jax.ShapeDtypeStruct((b.shape[0], N_DEV * b.shape[1]) if cols else (N_DEV, *b.shape), b.dtype)
                          for b, cols in zip(self.blocks, self.side_by_side)]
        self.scratch = [pltpu.SemaphoreType.DMA((7 * self.n,)), pltpu.SemaphoreType.DMA((7 * self.n,)),
                        pltpu.SemaphoreType.DMA((self.n,))]

    def _copies(self, a, x_ref, out_ref, send_sems, recv_sems):
        mx, my, mc = lax.axis_index("x"), lax.axis_index("y"), lax.axis_index("c")
        me, sibling = (mx, my, mc), (mx, my, 1 - mc)
        chips = [(1 - mx, my), (mx, 1 - my), (1 - mx, 1 - my)]
        width = self.blocks[a].shape[1]

        def slot(px, py, pc):
            d = 4 * px + 2 * py + pc
            if self.side_by_side[a]:
                return out_ref.at[:, pl.ds(pl.multiple_of(d * width, LANES), width)]
            return out_ref.at[d]

        def copy(k, block, to, src=None):
            return pltpu.make_async_remote_copy(
                src_ref=slot(*block) if src is None else src, dst_ref=slot(*block),
                send_sem=send_sems.at[7 * a + k], recv_sem=recv_sems.at[7 * a + k], device_id=to, device_id_type=_MESH)

        first = [copy(0, me, sibling, src=x_ref)] + [copy(1 + j, me, (*chip, mc), src=x_ref) for j, chip in enumerate(chips)]
        landed = [copy(1 + j, (*chip, mc), me) for j, chip in enumerate(chips)]
        passed = [copy(4 + j, (*chip, mc), sibling) for j, chip in enumerate(chips)]
        last = [copy(0, sibling, me)] + [copy(4 + j, (*chip, 1 - mc), me) for j, chip in enumerate(chips)]
        return slot(*me), first, landed, passed, last

    def start(self, ins, outs, send_sems, recv_sems, local_sems):
        for a in range(self.n):
            mine, first, _, _, _ = self._copies(a, ins[a], outs[a], send_sems, recv_sems)
            pltpu.make_async_copy(ins[a], mine, local_sems.at[a]).start()
            for cp in first:
                cp.start()

    def middle(self, ins, outs, send_sems, recv_sems, local_sems):
        for a in range(self.n):
            _, _, landed, passed, _ = self._copies(a, ins[a], outs[a], send_sems, recv_sems)
            for arrived, onward in zip(landed, passed):
                arrived.wait_recv()
                onward.start()

    def finish(self, ins, outs, send_sems, recv_sems, local_sems):
        for a in range(self.n):
            mine, first, _, passed, last = self._copies(a, ins[a], outs[a], send_sems, recv_sems)
            for cp in last:
                cp.wait_recv()
            for cp in first + passed:
                cp.wait_send()
            pltpu.make_async_copy(ins[a], mine, local_sems.at[a]).wait()


class _ExchangeRider:
    def __init__(self, arrays, side_by_side):
        self.blocks = list(arrays)
        self.side_by_side = list(side_by_side)
        self.n = len(self.blocks)
        self.in_specs = [_ANY] * self.n
        self.out_specs = [_ANY] * self.n
        self.out_shape = [jax.ShapeDtypeStruct((N_DEV, b.shape[0], b.shape[1] // N_DEV) if cols else b.shape, b.dtype)
                          for b, cols in zip(self.blocks, self.side_by_side)]
        self.scratch = [pltpu.SemaphoreType.DMA((7 * self.n,)), pltpu.SemaphoreType.DMA((7 * self.n,)),
                        pltpu.SemaphoreType.DMA((self.n,))]

    def _copies(self, a, g_ref, land_ref, send_sems, recv_sems):
        mx, my, mc = lax.axis_index("x"), lax.axis_index("y"), lax.axis_index("c")
        me = 4 * mx + 2 * my + mc
        width = self.out_shape[a].shape[2]

        def block(j):
            if self.side_by_side[a]:
                return g_ref.at[:, pl.ds(pl.multiple_of(j * width, LANES), width)]
            return g_ref.at[j]

        sends, recvs = [], []
        for k in range(1, N_DEV):
            peer = (_flip(mx, k & 4), _flip(my, k & 2), _flip(mc, k & 1))
            p = 4 * peer[0] + 2 * peer[1] + peer[2]
            sems = dict(send_sem=send_sems.at[7 * a + k - 1], recv_sem=recv_sems.at[7 * a + k - 1], device_id=peer, device_id_type=_MESH)
            sends.append(pltpu.make_async_remote_copy(src_ref=block(p), dst_ref=land_ref.at[me], **sems))
            recvs.append(pltpu.make_async_remote_copy(src_ref=block(p), dst_ref=land_ref.at[p], **sems))
        return block(me), land_ref.at[me], sends, recvs

    def start(self, ins, outs, send_sems, recv_sems, local_sems):
        for a in range(self.n):
            src, dst, sends, _ = self._copies(a, ins[a], outs[a], send_sems, recv_sems)
            pltpu.make_async_copy(src, dst, local_sems.at[a]).start()
            for cp in sends:
                cp.start()

    def middle(self, ins, outs, send_sems, recv_sems, local_sems):
        pass

    def finish(self, ins, outs, send_sems, recv_sems, local_sems):
        for a in range(self.n):
            src, dst, sends, recvs = self._copies(a, ins[a], outs[a], send_sems, recv_sems)
            for cp in recvs:
                cp.wait_recv()
            for cp in sends:
                cp.wait_send()
            pltpu.make_async_copy(src, dst, local_sems.at[a]).wait()


class _NoRider:
    blocks, in_specs, out_specs, out_shape, scratch = [], [], [], [], []


_NO_RIDER = _NoRider()


def _ride(rider, refs, n_in, n_out, first, middle, last, work):
    if rider is None:
        return work(*refs)
    r = rider.n
    own = refs[:n_in] + refs[n_in + r:n_in + r + n_out] + refs[n_in + r + n_out + r:len(refs) - 3]
    args = (refs[n_in:n_in + r], refs[n_in + r + n_out:n_in + r + n_out + r], *refs[len(refs) - 3:])
    pl.when(first)(lambda: rider.start(*args))
    pl.when(middle)(lambda: rider.middle(*args))
    work(*own)
    pl.when(last)(lambda: rider.finish(*args))


def _split2(x):
    hi = x.astype(BF16)
    return hi, (x - hi.astype(F32)).astype(BF16)


def _dot_hi_lo(x, w2):
    hi, lo = _split2(x)
    return jnp.dot(jnp.concatenate([hi, lo], axis=1), w2, preferred_element_type=F32)


def _suffix_sums(x, tri2, exact):
    if exact:
        return _dot_hi_lo(x, tri2)
    return jnp.dot(x.astype(BF16), tri2[:x.shape[1]], preferred_element_type=F32)


def _dot_nt(a, b):
    return lax.dot_general(a, b, _DN["nt"], preferred_element_type=F32)


def _suffix_matrix(bk):
    j = lax.broadcasted_iota(jnp.int32, (2 * bk, bk), 0) % bk
    s = lax.broadcasted_iota(jnp.int32, (2 * bk, bk), 1)
    return (j >= s).astype(BF16)


LOG2E = 1.4426950408889634


def _sb_window(q, k, t_pos, ks, bk, scale2, carry_c, tri, masked):
    width = k.shape[0]
    z = _dot_nt(q, k) * scale2
    sp = jnp.maximum(z, 0.0) + jnp.log2(1.0 + jnp.exp2(-jnp.abs(z)))
    if masked:
        mask = ks + lax.broadcasted_iota(jnp.int32, (1, width), 1) < t_pos
        lkm = jnp.where(mask, -sp, 0.0)
    else:
        mask, lkm = None, -sp
    sums, run = _window_suffix_sums(lkm, bk, tri, carry_c, False)
    a = jnp.exp2(z + sums)
    if masked:
        a = jnp.where(mask, a, 0.0)
    return z, sp, mask, a, run


def _window_suffix_sums(x, bk, tri, carry, exact):
    nb = x.shape[1] // bk
    parts, run = [None] * nb, carry
    for b in reversed(range(nb)):
        cs = _suffix_sums(x[:, b * bk:(b + 1) * bk], tri, exact)
        parts[b] = run + cs
        run = run + cs[:, 0:1]
    return (parts[0] if nb == 1 else jnp.concatenate(parts, axis=1)), run


KEY_WINDOW = 4
KEY_WINDOW_WIDE = 8


def _key_tiles(i, bk, step, carry):
    r = i % KEY_WINDOW

    def first(w):
        return lambda c: step(pl.multiple_of((i - w) * bk, bk), (w + 1) * bk, c, True)

    carry = lax.switch(r, [first(w) for w in range(KEY_WINDOW)], carry)
    top = i - r
    odd = (top // KEY_WINDOW) % (KEY_WINDOW_WIDE // KEY_WINDOW)
    carry = lax.fori_loop(
        0, odd, lambda p, c: step(pl.multiple_of((top - KEY_WINDOW) * bk, bk), KEY_WINDOW * bk, c, False), carry)
    top = top - odd * KEY_WINDOW
    return lax.fori_loop(
        0, top // KEY_WINDOW_WIDE,
        lambda p, c: step(pl.multiple_of((top - KEY_WINDOW_WIDE * (p + 1)) * bk, bk), KEY_WINDOW_WIDE * bk, c, False), carry)


def _attn_specs(H, L, bq):
    W3 = 3 * H

    def col(role):
        return lambda h, i: (h // H) * W3 + role * H + h % H

    q_spec = pl.BlockSpec((bq, HEAD_DIM), lambda h, i: (i, col(0)(h, i)))
    k_spec = pl.BlockSpec((L, HEAD_DIM), lambda h, i: (0, col(1)(h, i)))
    v_spec = pl.BlockSpec((L, HEAD_DIM), lambda h, i: (0, col(2)(h, i)))
    crow_spec = pl.BlockSpec((None, 1, L), lambda h, i: (jnp.maximum(h - H, 0), 0, 0))
    ccol_spec = pl.BlockSpec((None, bq, 1), lambda h, i: (jnp.maximum(h - H, 0), i, 0))
    g_spec = pl.BlockSpec((None, 1, HEAD_DIM), lambda h, i: (h, 0, 0))
    tile = pl.BlockSpec((bq, HEAD_DIM), lambda h, i: (i, h))
    stat = pl.BlockSpec((None, bq, 1), lambda h, i: (h, i, 0))
    return q_spec, k_spec, v_spec, crow_spec, ccol_spec, g_spec, tile, stat


def _attn_fwd(qkv, crow, ccol, g_heads, H, name, rider=None):
    L = qkv.shape[0]
    bq = bk = min(ATT_BLOCK, L)
    nq = L // bq
    scale = HEAD_DIM ** -0.5
    scale2 = scale * LOG2E
    q_spec, k_spec, v_spec, crow_spec, ccol_spec, g_spec, tile, stat = _attn_specs(H, L, bq)

    def work(q_ref, k_ref, v_ref, crow_ref, ccol_ref, g_ref, o_ref, on_ref, lse_ref):
        h, i = pl.program_id(0), pl.program_id(1)
        q = q_ref[...]
        t_pos = i * bq + lax.broadcasted_iota(jnp.int32, (bq, 1), 0)

        def finish(o):
            o_ref[...] = o
            on_ref[...] = _rms(o, g_ref[...]).astype(BF16)

        @pl.when(h < H)
        def _stick_breaking():
            tri = _suffix_matrix(bk)

            def step(ks, rows, carry, masked):
                c, acc = carry
                k = k_ref[pl.ds(ks, rows), :]
                v = v_ref[pl.ds(ks, rows), :]
                _, _, _, a, c = _sb_window(q, k, t_pos, ks, bk, scale2, c, tri, masked)
                acc = acc + jnp.dot(a.astype(BF16), v, preferred_element_type=F32)
                return c, acc

            _, acc = _key_tiles(i, bk, step, (jnp.zeros((bq, 1), F32), jnp.zeros((bq, HEAD_DIM), F32)))
            finish(acc)
            lse_ref[...] = jnp.zeros_like(lse_ref)

        @pl.when(h >= H)
        def _forgetting():
            cq = ccol_ref[...] * LOG2E

            def step(ks, rows, carry, masked):
                m, l, acc = carry
                k = k_ref[pl.ds(ks, rows), :]
                v = v_ref[pl.ds(ks, rows), :]
                s = _dot_nt(q, k) * scale2 + (cq - crow_ref[:, pl.ds(ks, rows)] * LOG2E)
                if masked:
                    s = jnp.where(ks + lax.broadcasted_iota(jnp.int32, (1, rows), 1) <= t_pos, s, NEG_INF)
                m_new = jnp.maximum(m, jnp.max(s, axis=-1, keepdims=True))
                alpha = jnp.exp2(m - m_new)
                p = jnp.exp2(s - m_new)
                l = alpha * l + jnp.sum(p, axis=-1, keepdims=True)
                acc = alpha * acc + _dot_hi_lo(p, jnp.concatenate([v, v], axis=0))
                return m_new, l, acc

            init = (jnp.full((bq, 1), NEG_INF, F32), jnp.zeros((bq, 1), F32), jnp.zeros((bq, HEAD_DIM), F32))
            m, l, acc = _key_tiles(i, bk, step, init)
            finish(acc / l)
            lse_ref[...] = m + jnp.log2(l)

    def body(*refs):
        h, i = pl.program_id(0), pl.program_id(1)
        _ride(rider, refs, 6, 3, (h == 0) & (i == 0), (h == 2 * H - 2) & (i == 0), (h == 2 * H - 1) & (i == nq - 1), work)

    W2 = 2 * H * HEAD_DIM
    est = 4 * L * HEAD_DIM * 2 + 14 * bq * KEY_WINDOW_WIDE * bk * 4 + (4 << 20)
    extra = rider or _NO_RIDER
    outs = pl.pallas_call(
        body, name=name, grid=(2 * H, nq),
        in_specs=[q_spec, k_spec, v_spec, crow_spec, ccol_spec, g_spec] + extra.in_specs,
        out_specs=[tile, tile, stat] + extra.out_specs,
        out_shape=[jax.ShapeDtypeStruct((L, W2), F32), jax.ShapeDtypeStruct((L, W2), BF16),
                   jax.ShapeDtypeStruct((2 * H, L, 1), F32)] + extra.out_shape,
        scratch_shapes=extra.scratch,
        compiler_params=_params(("arbitrary", "arbitrary"), est),
    )(qkv, qkv, qkv, crow, ccol, g_heads, *extra.blocks)
    return outs[0], outs[1], outs[2], list(outs[3:])


def _attn_bwd(qkv, crow, ccol, g_heads, o, lse, d_on, H, name, rider=None):
    L = qkv.shape[0]
    bq = bk = min(ATT_BLOCK, L)
    nq = L // bq
    scale = HEAD_DIM ** -0.5
    scale2 = scale * LOG2E
    q_spec, k_spec, v_spec, crow_spec, ccol_spec, g_spec, tile, stat = _attn_specs(H, L, bq)
    full = pl.BlockSpec((L, HEAD_DIM), lambda h, i: (0, h))
    dg_spec = pl.BlockSpec((None, 1, HEAD_DIM), lambda h, i: (h, 0, 0))
    dc_spec = pl.BlockSpec((None, 1, L), lambda h, i: (h, 0, 0))

    def work(q_ref, k_ref, v_ref, crow_ref, ccol_ref, g_ref, o_ref, lse_ref, don_ref,
             dq_ref, dk_ref, dv_ref, dg_ref, dc_ref, dkt_acc, dvt_acc):
        h, i = pl.program_id(0), pl.program_id(1)

        @pl.when(i == 0)
        def _():
            dkt_acc[...] = jnp.zeros_like(dkt_acc)
            dvt_acc[...] = jnp.zeros_like(dvt_acc)
            dg_ref[...] = jnp.zeros_like(dg_ref)
            dc_ref[...] = jnp.zeros_like(dc_ref)

        q = q_ref[...]
        t_pos = i * bq + lax.broadcasted_iota(jnp.int32, (bq, 1), 0)
        o_t = o_ref[...]
        d_o, dg = _rms_bwd(o_t, g_ref[...], don_ref[...])
        dg_ref[...] += jnp.sum(dg, axis=0, keepdims=True)
        d_ob = d_o.astype(BF16)
        dsum = jnp.sum(d_ob.astype(F32) * o_t, axis=-1, keepdims=True)
        q_t = q.astype(F32).T.astype(BF16)
        d_obt = d_o.T.astype(BF16)

        @pl.when(h < H)
        def _stick_breaking():
            tri = _suffix_matrix(bk)

            def step(ks, rows, carry, masked):
                c, gs, dq = carry
                k = k_ref[pl.ds(ks, rows), :]
                v = v_ref[pl.ds(ks, rows), :]
                z, sp, mask, a, c = _sb_window(q, k, t_pos, ks, bk, scale2, c, tri, masked)
                a_b = a.astype(BF16)
                g_w = a_b.astype(F32) * _dot_nt(d_ob, v)
                later, gs = _window_suffix_sums(g_w, bk, tri, gs, True)
                prefix = dsum - (later - g_w)
                dz = (g_w - jnp.exp2(z - sp) * prefix) * scale
                if masked:
                    dz = jnp.where(mask, dz, 0.0)
                dzb = dz.astype(BF16)
                dq = dq + jnp.dot(dzb, k, preferred_element_type=F32)
                dkt_acc[:, pl.ds(ks, rows)] += jnp.dot(q_t, dzb, preferred_element_type=F32)
                dvt_acc[:, pl.ds(ks, rows)] += jnp.dot(d_obt, a_b, preferred_element_type=F32)
                return c, gs, dq

            z1 = jnp.zeros((bq, 1), F32)
            _, _, dq = _key_tiles(i, bk, step, (z1, z1, jnp.zeros((bq, HEAD_DIM), F32)))
            dq_ref[...] = dq.astype(BF16)

        @pl.when(h >= H)
        def _forgetting():
            cq = ccol_ref[...] * LOG2E - lse_ref[...]

            def step(ks, rows, dq, masked):
                k = k_ref[pl.ds(ks, rows), :]
                v = v_ref[pl.ds(ks, rows), :]
                p = jnp.exp2(_dot_nt(q, k) * scale2 + (cq - crow_ref[:, pl.ds(ks, rows)] * LOG2E))
                if masked:
                    p = jnp.where(ks + lax.broadcasted_iota(jnp.int32, (1, rows), 1) <= t_pos, p, 0.0)
                ds = p * (_dot_nt(d_ob, v) - dsum)
                dsb = (ds * scale).astype(BF16)
                dq = dq + jnp.dot(dsb, k, preferred_element_type=F32)
                dkt_acc[:, pl.ds(ks, rows)] += jnp.dot(q_t, dsb, preferred_element_type=F32)
                dvt_acc[:, pl.ds(ks, rows)] += jnp.dot(d_obt, p.astype(BF16), preferred_element_type=F32)
                dc_ref[:, pl.ds(ks, rows)] += -jnp.sum(ds, axis=0, keepdims=True)
                return dq

            dq = _key_tiles(i, bk, step, jnp.zeros((bq, HEAD_DIM), F32))
            dq_ref[...] = dq.astype(BF16)

        @pl.when(i == nq - 1)
        def _():
            dk_ref[...] = dkt_acc[...].T.astype(BF16)
            dv_ref[...] = dvt_acc[...].T.astype(BF16)

    W2 = 2 * H * HEAD_DIM
    est = 4 * L * HEAD_DIM * 2 + 4 * L * HEAD_DIM * 2 + 2 * L * HEAD_DIM * 4 + 18 * bq * KEY_WINDOW_WIDE * bk * 4 + (4 << 20)
    def body(*refs):
        h, i = pl.program_id(0), pl.program_id(1)
        _ride(rider, refs, 9, 5, (h == 0) & (i == 0), (h == H) & (i == 0), (h == 2 * H - 1) & (i == nq - 1), work)

    extra = rider or _NO_RIDER
    dq, dk, dv, dg, dc, *brought = pl.pallas_call(
        body, name=name, grid=(2 * H, nq),
        in_specs=[q_spec, k_spec, v_spec, crow_spec, ccol_spec, g_spec, tile, stat, tile] + extra.in_specs,
        out_specs=[tile, full, full, dg_spec, dc_spec] + extra.out_specs,
        out_shape=[jax.ShapeDtypeStruct((L, W2), BF16)] * 3
        + [jax.ShapeDtypeStruct((2 * H, 1, HEAD_DIM), F32), jax.ShapeDtypeStruct((2 * H, 1, L), F32)] + extra.out_shape,
        scratch_shapes=[pltpu.VMEM((HEAD_DIM, L), F32), pltpu.VMEM((HEAD_DIM, L), F32)] + extra.scratch,
        compiler_params=_params(("arbitrary", "arbitrary"), est),
    )(qkv, qkv, qkv, crow, ccol, g_heads, o, lse, d_on, *extra.blocks)
    return dq, dk, dv, dg[:, 0, :], dc, brought


HALO = 16


def _conv_tiles(L, F):
    return _row_tile(L), _divisor(F, (512, 256, 128))


SUBLANES = 8


def _shift_down(x, before, bt):
    ext = jnp.concatenate([before, x], axis=0)
    return pltpu.roll(ext, 1, 0)[SUBLANES:], pltpu.roll(ext, 2, 0)[SUBLANES:]


def _shift_up(x, after, bt):
    ext = jnp.concatenate([x, after], axis=0)
    return pltpu.roll(ext, bt + SUBLANES - 1, 0)[:bt], pltpu.roll(ext, bt + SUBLANES - 2, 0)[:bt]


def _conv_rows(p_ref, halo_ref, w_ref, b_ref, first, bt):
    p = p_ref[...].astype(F32)
    before = jnp.where(first, 0.0, halo_ref[...].astype(F32)[HALO - SUBLANES:HALO, :])
    p1, p2 = _shift_down(p, before, bt)
    w = w_ref[...]
    a = w[0:1, :] * p2 + w[1:2, :] * p1 + w[2:3, :] * p + b_ref[...]
    return a, p, p1, p2


def _sigmoid(x):
    return 0.5 + 0.5 * jnp.tanh(0.5 * x)


def _conv_in_specs(L, F, bt, bc, order):
    nf = F // bc
    r = bt // HALO
    ix = (lambda a, b: (a, b)) if order == "ij" else (lambda a, b: (b, a))

    def mk(shape, fn):
        return pl.BlockSpec(shape, lambda a, b: fn(*ix(a, b)))

    return [
        mk((bt, bc), lambda i, j: (i, j)), mk((HALO, bc), lambda i, j: (jnp.maximum(i * r - 1, 0), j)),
        mk((bt, bc), lambda i, j: (i, nf + j)), mk((HALO, bc), lambda i, j: (jnp.maximum(i * r - 1, 0), nf + j)),
        mk((3, bc), lambda i, j: (0, j)), mk((3, bc), lambda i, j: (0, nf + j)),
        mk((1, bc), lambda i, j: (0, j)), mk((1, bc), lambda i, j: (0, nf + j)),
    ]


def _convgate_fwd(p, conv_w, conv_b, name):
    L, F2 = p.shape
    F = F2 // 2
    bt, bc = _conv_tiles(L, F)

    def body(pg, hg, pu, hu, wg, wu, bg, bu, act_ref):
        first = pl.program_id(0) == 0
        ag = _conv_rows(pg, hg, wg, bg, first, bt)[0]
        au = _conv_rows(pu, hu, wu, bu, first, bt)[0]
        act_ref[...] = (ag * _sigmoid(ag) * au).astype(BF16)

    return pl.pallas_call(
        body, name=name, grid=(L // bt, F // bc), in_specs=_conv_in_specs(L, F, bt, bc, "ij"),
        out_specs=pl.BlockSpec((bt, bc), lambda i, j: (i, j)), out_shape=jax.ShapeDtypeStruct((L, F), BF16),
        compiler_params=_params(("parallel", "parallel"), 24 * bt * bc * 4),
    )(p, p, p, p, conv_w, conv_w, conv_b.reshape(1, F2), conv_b.reshape(1, F2))


def _convgate_bwd(p, conv_w, conv_b, d_act, name):
    L, F2 = p.shape
    F = F2 // 2
    bt, bc = _conv_tiles(L, F)

    def body(pg, hg, pu, hu, wg, wu, bg, bu, dact_ref, da_ref, dwb_ref):
        first = pl.program_id(1) == 0
        ag, xg, xg1, xg2 = _conv_rows(pg, hg, wg, bg, first, bt)
        au, xu, xu1, xu2 = _conv_rows(pu, hu, wu, bu, first, bt)
        d_act = dact_ref[...].astype(F32)
        sg = _sigmoid(ag)
        dag = d_act * au * sg * (1.0 + ag * (1.0 - sg))
        dau = d_act * ag * sg
        da_ref[0] = dag.astype(BF16)
        da_ref[1] = dau.astype(BF16)

        @pl.when(first)
        def _():
            dwb_ref[...] = jnp.zeros_like(dwb_ref)
        cs = lambda v: jnp.sum(v, axis=0, keepdims=True)
        dwb_ref[0] += jnp.concatenate([cs(dag * xg2), cs(dag * xg1), cs(dag * xg), cs(dag)], axis=0)
        dwb_ref[1] += jnp.concatenate([cs(dau * xu2), cs(dau * xu1), cs(dau * xu), cs(dau)], axis=0)

    da, dwb = pl.pallas_call(
        body, name=name, grid=(F // bc, L // bt),
        in_specs=_conv_in_specs(L, F, bt, bc, "ji") + [pl.BlockSpec((bt, bc), lambda j, i: (i, j))],
        out_specs=[pl.BlockSpec((2, bt, bc), lambda j, i: (0, i, j)), pl.BlockSpec((2, 4, bc), lambda j, i: (0, 0, j))],
        out_shape=[jax.ShapeDtypeStruct((2, L, F), BF16), jax.ShapeDtypeStruct((2, 4, F), F32)],
        compiler_params=_params(("parallel", "arbitrary"), 40 * bt * bc * 4),
    )(p, p, p, p, conv_w, conv_w, conv_b.reshape(1, F2), conv_b.reshape(1, F2), d_act)
    d_w = jnp.concatenate([dwb[0, 0:3], dwb[1, 0:3]], axis=1)
    d_b = jnp.concatenate([dwb[0, 3], dwb[1, 3]], axis=0)
    return da, d_w, d_b


def _conv_bwd_data(da, conv_w, name):
    _, L, F = da.shape
    bt, bc = _conv_tiles(L, F)
    nf, r, nt = F // bc, bt // HALO, L // bt

    def body(da_ref, nxt_ref, w_ref, dp_ref):
        last = pl.program_id(0) == nt - 1
        x = da_ref[...].astype(F32)
        after = jnp.where(last, 0.0, nxt_ref[...].astype(F32)[0:SUBLANES, :])
        x1, x2 = _shift_up(x, after, bt)
        w = w_ref[...]
        dp_ref[...] = (w[2:3, :] * x + w[1:2, :] * x1 + w[0:1, :] * x2).astype(BF16)

    return pl.pallas_call(
        body, name=name, grid=(nt, 2 * nf),
        in_specs=[pl.BlockSpec((None, bt, bc), lambda i, j: (j // nf, i, j % nf)),
                  pl.BlockSpec((None, HALO, bc), lambda i, j: (j // nf, jnp.minimum((i + 1) * r, nt * r - 1), j % nf)),
                  pl.BlockSpec((3, bc), lambda i, j: (0, j))],
        out_specs=pl.BlockSpec((bt, bc), lambda i, j: (i, j)), out_shape=jax.ShapeDtypeStruct((L, 2 * F), BF16),
        compiler_params=_params(("parallel", "parallel"), 16 * bt * bc * 4),
    )(da, da, conv_w)


def _pad_cols(a, n):
    return jnp.pad(a, ((0, 0), (0, n - a.shape[1])))


def _local_step(x, tgt, meta, W, hooks=None):
    S, D = x.shape
    n_meta = meta.shape[0]
    depth = len(W["w_in"])
    H = D // (2 * HEAD_DIM)
    WQ = 6 * H * HEAD_DIM
    L = -(-(S + n_meta) // ATT_BLOCK) * ATT_BLOCK
    tail = L - S - n_meta
    zeros_tail = jnp.zeros((tail, D), F32)
    h = jnp.concatenate([meta, x, zeros_tail], axis=0)
    tgt_p = jnp.concatenate([jnp.zeros((n_meta, D), F32), tgt, zeros_tail], axis=0)

    saved = []
    _, u1 = _resnorm_fwd(h, None, None, W["g_mix_pre"][0], "prenorm0")
    for l in range(depth):
        w_in = W["w_in"][l]
        w_qkv, w_f = w_in[:, :WQ], _pad_cols(w_in[:, WQ:], LANES)
        b_pad = jnp.pad(W["b_f"][l], (0, LANES - H)).reshape(1, LANES)
        g_heads = jnp.concatenate([W["g_sb"][l], W["g_fox"][l]], axis=0).reshape(2 * H, 1, HEAD_DIM)
        qkv = _mm(u1, w_qkv, "nn", BF16, f"qkv{l}")
        fl = _mm(u1, w_f, "nn", F32, f"flogit{l}")
        c = _gate_fwd(fl, b_pad, f"gate_fwd{l}")
        c_heads = c[:, :H].T
        crow, ccol = c_heads[:, None, :], c_heads[:, :, None]
        o, on, lse, brought = _attn_fwd(qkv, crow, ccol, g_heads, H, f"attn_fwd{l}", hooks and hooks.fwd_rider(l))
        if hooks:
            hooks.fwd_done(l, brought, W)
        mix = _mm(on, W["w_out"][l], "nn", F32, f"mix{l}")
        h_mid, u2 = _resnorm_fwd(h, mix, W["g_mix_post"][l], W["g_ffn_pre"][l], f"resnorm_a{l}")
        p = _mm(u2, W["w_up"][l], "nn", BF16, f"up{l}")
        act = _convgate_fwd(p, W["conv_w"][l], W["conv_b"][l], f"convgate{l}")
        ff = _mm(act, W["w_down"][l], "nn", F32, f"down{l}")
        g_next = W["g_mix_pre"][l + 1] if l + 1 < depth else None
        h_out, u1_next = _resnorm_fwd(h_mid, ff, W["g_ffn_post"][l], g_next, f"resnorm_b{l}")
        saved.append(dict(h_in=h, u1=u1, w_qkv=w_qkv, w_f=w_f, b_pad=b_pad, g_heads=g_heads, qkv=qkv, fl=fl, crow=crow,
                          ccol=ccol, o=o, on=on, lse=lse, mix=mix, h_mid=h_mid, u2=u2, p=p, act=act, ff=ff, h_out=h_out))
        h, u1 = h_out, u1_next

    loss, dh = _loss(h, tgt_p, n_meta, S, "loss")

    grads = {k: [None] * depth for k in ("g_mix_pre", "w_in", "b_f", "g_sb", "g_fox", "w_out", "g_mix_post", "g_ffn_pre",
                                         "w_up", "conv_w", "conv_b", "w_down", "g_ffn_post")}
    du1_next = None
    for l in reversed(range(depth)):
        s = saved[l]
        g_next = W["g_mix_pre"][l + 1] if l + 1 < depth else None
        dh, d_ff, grads["g_ffn_post"][l], dg_pre_next = _resnorm_bwd(
            dh, du1_next, s["h_out"], s["ff"], W["g_ffn_post"][l], g_next, f"resnorm_b_bwd{l}")
        if l + 1 < depth:
            grads["g_mix_pre"][l + 1] = dg_pre_next
        d_act = _mm(d_ff, W["w_down"][l], "nt", BF16, f"d_act{l}")
        grads["w_down"][l] = _mm(s["act"], d_ff, "tn", BF16, f"dw_down{l}")
        da, grads["conv_w"][l], grads["conv_b"][l] = _convgate_bwd(s["p"], W["conv_w"][l], W["conv_b"][l], d_act, f"convgate_bwd{l}")
        dp = _conv_bwd_data(da, W["conv_w"][l], f"conv_bwd{l}")
        du2 = _mm(dp, W["w_up"][l], "nt", F32, f"d_u2{l}")
        grads["w_up"][l] = _mm(s["u2"], dp, "tn", BF16, f"dw_up{l}")
        dh, d_mix, grads["g_mix_post"][l], grads["g_ffn_pre"][l] = _resnorm_bwd(
            dh, du2, s["h_mid"], s["mix"], W["g_mix_post"][l], W["g_ffn_pre"][l], f"resnorm_a_bwd{l}")
        d_on = _mm(d_mix, W["w_out"][l], "nt", F32, f"d_on{l}")
        grads["w_out"][l] = _mm(s["on"], d_mix, "tn", BF16, f"dw_out{l}")
        dq, dk, dv, dg_heads, dcrow, brought = _attn_bwd(s["qkv"], s["crow"], s["ccol"], s["g_heads"], s["o"], s["lse"], d_on, H,
                                                         f"attn_bwd{l}", hooks and hooks.bwd_rider(l, grads))
        if hooks:
            hooks.bwd_done(l, brought)
        grads["g_sb"][l], grads["g_fox"][l] = dg_heads[:H], dg_heads[H:]
        dc = _pad_cols(dcrow[H:, 0, :].T, LANES)
        dfl, db = _gate_bwd(dc, s["fl"], s["b_pad"], f"gate_bwd{l}")
        grads["b_f"][l] = db[:H]
        Wh = H * HEAD_DIM
        d_qkv = jnp.concatenate([dq[:, :Wh], dk[:, :Wh], dv[:, :Wh], dq[:, Wh:], dk[:, Wh:], dv[:, Wh:]], axis=1)
        du1_next = _mm(d_qkv, s["w_qkv"], "nt", F32, f"d_u1{l}", a2=dfl, b2=s["w_f"])
        dw_qkv = _mm(s["u1"], d_qkv, "tn", BF16, f"dw_qkv{l}")
        dw_f = _mm(s["u1"], dfl, "tn", BF16, f"dw_f{l}")
        grads["w_in"][l] = jnp.concatenate([dw_qkv, dw_f[:, :H]], axis=1)
    dh0, _, _, grads["g_mix_pre"][0] = _resnorm_bwd(dh, du1_next, saved[0]["h_in"], None, None, W["g_mix_pre"][0], "prenorm0_bwd")
    return loss, dh0[n_meta:n_meta + S], dh0[:n_meta], grads


def _all_gather(x, name):
    def body(x_ref, out_ref, send_sems, recv_sems, local_sem):
        mx, my, mc = lax.axis_index("x"), lax.axis_index("y"), lax.axis_index("c")
        me, sibling = (mx, my, mc), (mx, my, 1 - mc)
        chips = [(1 - mx, my), (mx, 1 - my), (1 - mx, 1 - my)]

        def slot(px, py, pc):
            return out_ref.at[4 * px + 2 * py + pc]

        def copy(k, block, to, src=None):
            return pltpu.make_async_remote_copy(
                src_ref=slot(*block) if src is None else src, dst_ref=slot(*block),
                send_sem=send_sems.at[k], recv_sem=recv_sems.at[k], device_id=to, device_id_type=_MESH)

        mine = pltpu.make_async_copy(x_ref, slot(*me), local_sem)
        mine.start()
        first = [copy(0, me, sibling, src=x_ref)]
        first += [copy(1 + j, me, (*chip, mc), src=x_ref) for j, chip in enumerate(chips)]
        for cp in first:
            cp.start()
        passed = [copy(4 + j, (*chip, mc), sibling) for j, chip in enumerate(chips)]
        for j, chip in enumerate(chips):
            copy(1 + j, (*chip, mc), me).wait_recv()
            passed[j].start()
        copy(0, sibling, me).wait_recv()
        for j, chip in enumerate(chips):
            copy(4 + j, (*chip, 1 - mc), me).wait_recv()
        for cp in first + passed:
            cp.wait_send()
        mine.wait()

    return pl.pallas_call(
        body, name=name, out_shape=jax.ShapeDtypeStruct((N_DEV, *x.shape), x.dtype),
        in_specs=[_ANY], out_specs=_ANY,
        scratch_shapes=[pltpu.SemaphoreType.DMA((7,)), pltpu.SemaphoreType.DMA((7,)), pltpu.SemaphoreType.DMA],
    )(x)


def _exchange(g, name):
    def body(g_ref, land_ref, send_sems, recv_sems, local_sem):
        mx, my, mc = lax.axis_index("x"), lax.axis_index("y"), lax.axis_index("c")
        me = 4 * mx + 2 * my + mc
        local = pltpu.make_async_copy(g_ref.at[me], land_ref.at[me], local_sem)
        local.start()
        sends, recvs = [], []
        for k in range(1, N_DEV):
            peer = (_flip(mx, k & 4), _flip(my, k & 2), _flip(mc, k & 1))
            p = 4 * peer[0] + 2 * peer[1] + peer[2]
            sends.append(pltpu.make_async_remote_copy(
                src_ref=g_ref.at[p], dst_ref=land_ref.at[me], send_sem=send_sems.at[k - 1], recv_sem=recv_sems.at[k - 1],
                device_id=peer, device_id_type=_MESH))
            recvs.append(pltpu.make_async_remote_copy(
                src_ref=g_ref.at[p], dst_ref=land_ref.at[p], send_sem=send_sems.at[k - 1], recv_sem=recv_sems.at[k - 1],
                device_id=peer, device_id_type=_MESH))
        for cp in sends:
            cp.start()
        for cp in recvs:
            cp.wait_recv()
        for cp in sends:
            cp.wait_send()
        local.wait()

    return pl.pallas_call(
        body, name=name, out_shape=jax.ShapeDtypeStruct(g.shape, g.dtype), in_specs=[_ANY], out_specs=_ANY,
        scratch_shapes=[pltpu.SemaphoreType.DMA((7,)), pltpu.SemaphoreType.DMA((7,)), pltpu.SemaphoreType.DMA],
    )(g)


def _exchange_cores(g, name):
    def body(g_ref, land_ref, send_sems, recv_sems):
        mx, my, mc = lax.axis_index("x"), lax.axis_index("y"), lax.axis_index("c")
        copies = [pltpu.make_async_remote_copy(
            src_ref=g_ref.at[2 * q + (1 - mc)], dst_ref=land_ref.at[q], send_sem=send_sems.at[q], recv_sem=recv_sems.at[q],
            device_id=(mx, my, 1 - mc), device_id_type=_MESH) for q in range(4)]
        for cp in copies:
            cp.start()
        for cp in copies:
            cp.wait_recv()
        for cp in copies:
            cp.wait_send()

    return pl.pallas_call(
        body, name=name, out_shape=jax.ShapeDtypeStruct((4, *g.shape[1:]), g.dtype), in_specs=[_ANY], out_specs=_ANY,
        scratch_shapes=[pltpu.SemaphoreType.DMA((4,)), pltpu.SemaphoreType.DMA((4,))],
    )(g)


def _pair_sum(g, land, name):
    _, n, R, C = g.shape
    tr = _divisor(R, (128, 64, 32, 16, 8))
    core = lax.axis_index("c").astype(jnp.int32).reshape(1)

    def body(c_ref, g_ref, land_ref, o_ref):
        o_ref[...] = (g_ref[...].astype(F32) + land_ref[...].astype(F32)).astype(o_ref.dtype)

    blk = pl.BlockSpec((None, None, tr, C), lambda q, l, r, c_ref: (q, l, r, 0))
    return pl.pallas_call(
        body, name=name, out_shape=jax.ShapeDtypeStruct(land.shape, g.dtype),
        grid_spec=pltpu.PrefetchScalarGridSpec(
            num_scalar_prefetch=1, grid=(4, n, R // tr),
            in_specs=[pl.BlockSpec((None, None, tr, C), lambda q, l, r, c_ref: (2 * q + c_ref[0], l, r, 0)), blk],
            out_specs=blk),
        compiler_params=_params(("parallel", "parallel", "parallel"), 8 * tr * C * 4),
    )(core, g, land)


def _exchange_chips(part, name):
    def body(p_ref, land_ref, send_sems, recv_sems, local_sem):
        mx, my, mc = lax.axis_index("x"), lax.axis_index("y"), lax.axis_index("c")
        me = 2 * mx + my
        local = pltpu.make_async_copy(p_ref.at[me], land_ref.at[me], local_sem)
        local.start()
        sends, recvs = [], []
        for k in range(1, 4):
            px, py = _flip(mx, k & 2), _flip(my, k & 1)
            p = 2 * px + py
            sends.append(pltpu.make_async_remote_copy(
                src_ref=p_ref.at[p], dst_ref=land_ref.at[me], send_sem=send_sems.at[k - 1], recv_sem=recv_sems.at[k - 1],
                device_id=(px, py, mc), device_id_type=_MESH))
            recvs.append(pltpu.make_async_remote_copy(
                src_ref=p_ref.at[p], dst_ref=land_ref.at[p], send_sem=send_sems.at[k - 1], recv_sem=recv_sems.at[k - 1],
                device_id=(px, py, mc), device_id_type=_MESH))
        for cp in sends:
            cp.start()
        for cp in recvs:
            cp.wait_recv()
        for cp in sends:
            cp.wait_send()
        local.wait()

    return pl.pallas_call(
        body, name=name, out_shape=jax.ShapeDtypeStruct(part.shape, part.dtype), in_specs=[_ANY], out_specs=_ANY,
        scratch_shapes=[pltpu.SemaphoreType.DMA((3,)), pltpu.SemaphoreType.DMA((3,)), pltpu.SemaphoreType.DMA],
    )(part)


def _adamw_landed(lands, w, m, v, name):
    n, R, C = w.shape
    tr = _divisor(R, (64, 32, 16, 8))
    nr = R // tr
    blk = pl.BlockSpec((None, tr, C), lambda l, r: (l, r, 0))

    def land_spec(k, S):
        return pl.BlockSpec((S, tr, C), lambda l, r: (0, jnp.where(l == k, r, jnp.where(l < k, 0, nr - 1)), 0))

    def body(*refs):
        land_refs = refs[:n]
        w_ref, m_ref, v_ref, g_ref, d_ref, mo_ref, vo_ref = refs[n:]
        for k in range(n):
            @pl.when(pl.program_id(0) == k)
            def _(k=k):
                g = land_refs[k][0].astype(F32)
                for s in range(1, lands[k].shape[0]):
                    g = g + land_refs[k][s].astype(F32)
                m_new = ADAM_B1 * m_ref[...] + (1.0 - ADAM_B1) * g
                v_new = ADAM_B2 * v_ref[...] + (1.0 - ADAM_B2) * (g * g)
                m_hat = m_new / (1.0 - ADAM_B1 ** ADAM_STEP)
                v_hat = v_new / (1.0 - ADAM_B2 ** ADAM_STEP)
                g_ref[...] = g
                d_ref[...] = -ADAM_LR * (m_hat / (jnp.sqrt(v_hat) + ADAM_EPS) + ADAM_WD * w_ref[...])
                mo_ref[...] = m_new
                vo_ref[...] = v_new

    est = 2 * tr * C * (sum(x.shape[0] * jnp.dtype(x.dtype).itemsize for x in lands) + 7 * 4) * 9 // 8
    return pl.pallas_call(
        body, name=name, grid=(n, nr),
        in_specs=[land_spec(k, x.shape[0]) for k, x in enumerate(lands)] + [blk, blk, blk],
        out_specs=[blk] * 4, out_shape=[jax.ShapeDtypeStruct((n, R, C), F32)] * 4,
        compiler_params=_params(("arbitrary", "arbitrary"), est),
    )(*lands, w, m, v)


def _pack(arrs):
    flat = jnp.concatenate([a.reshape(-1).astype(F32) for a in arrs])
    rows = -(-flat.shape[0] // (8 * LANES)) * 8
    return jnp.pad(flat, (0, rows * LANES - flat.shape[0])).reshape(rows, LANES)


def _unpack(packed, shapes):
    flat, out, at = packed.reshape(-1), [], 0
    for s in shapes:
        n = math.prod(s)
        out.append(flat[at:at + n].reshape(s))
        at += n
    return out


_BIG = ("w_in", "w_out", "w_up", "w_down")
_REPLICATED = ("g_mix_pre", "b_f", "g_sb", "g_fox", "g_mix_post", "g_ffn_pre", "conv_b", "g_ffn_post")
_ORDER = ("meta", "g_mix_pre", "w_in", "b_f", "g_sb", "g_fox", "w_out", "g_mix_post", "g_ffn_pre", "w_up", "conv_w",
          "conv_b", "w_down", "g_ffn_post")
_COLUMN_SHARDED = ("w_in", "w_up")


def kernel(x, meta, g_mix_pre, w_in, b_f, g_sb, g_fox, w_out, g_mix_post, g_ffn_pre, w_up, conv_w, conv_b, w_down, g_ffn_post, loss_target, m_meta, m_g_mix_pre, m_w_in, m_b_f, m_g_sb, m_g_fox, m_w_out, m_g_mix_post, m_g_ffn_pre, m_w_up, m_conv_w, m_conv_b, m_w_down, m_g_ffn_post, v_meta, v_g_mix_pre, v_w_in, v_b_f, v_g_sb, v_g_fox, v_w_out, v_g_mix_post, v_g_ffn_pre, v_w_up, v_conv_w, v_conv_b, v_w_down, v_g_ffn_post):
    w = dict(meta=meta, g_mix_pre=g_mix_pre, w_in=w_in, b_f=b_f, g_sb=g_sb, g_fox=g_fox, w_out=w_out, g_mix_post=g_mix_post,
             g_ffn_pre=g_ffn_pre, w_up=w_up, conv_w=conv_w, conv_b=conv_b, w_down=w_down, g_ffn_post=g_ffn_post)
    m = dict(meta=m_meta, g_mix_pre=m_g_mix_pre, w_in=m_w_in, b_f=m_b_f, g_sb=m_g_sb, g_fox=m_g_fox, w_out=m_w_out,
             g_mix_post=m_g_mix_post, g_ffn_pre=m_g_ffn_pre, w_up=m_w_up, conv_w=m_conv_w, conv_b=m_conv_b, w_down=m_w_down,
             g_ffn_post=m_g_ffn_post)
    v = dict(meta=v_meta, g_mix_pre=v_g_mix_pre, w_in=v_w_in, b_f=v_b_f, g_sb=v_g_sb, g_fox=v_g_fox, w_out=v_w_out,
             g_mix_post=v_g_mix_post, g_ffn_pre=v_g_ffn_pre, w_up=v_w_up, conv_w=v_conv_w, conv_b=v_conv_b, w_down=v_w_down,
             g_ffn_post=v_g_ffn_post)
    depth = w_in.shape[0]
    shards = {name: [w[name][l].astype(BF16) for l in range(depth)] for name in _BIG}

    def assemble(name, g):
        if name in _COLUMN_SHARDED:
            return jnp.transpose(g, (1, 0, 2)).reshape(g.shape[1], -1)
        return g.reshape(-1, g.shape[2])

    def blocks_of(name, grad):
        R, C = w[name].shape[1:]
        if name in _COLUMN_SHARDED:
            return jnp.transpose(grad.reshape(R, N_DEV, C), (1, 0, 2)).astype(BF16)
        return grad.reshape(N_DEV, R, C).astype(BF16)

    def side_by_side(name):
        return name in _COLUMN_SHARDED and w[name].shape[2] % LANES == 0

    class Hooks:
        def __init__(self):
            self.landed = {}

        def fwd_keys(self, l):
            return ([("w_in", l + 1)] if l + 1 < depth else []) + [("w_out", l), ("w_up", l), ("w_down", l)]

        def bwd_keys(self, l):
            return ([("w_in", l + 1)] if l + 1 < depth else []) + [("w_down", l), ("w_up", l), ("w_out", l)]

        def fwd_rider(self, l):
            keys = self.fwd_keys(l)
            return _GatherRider([shards[name][ll] for name, ll in keys], [side_by_side(name) for name, _ in keys])

        def fwd_done(self, l, brought, W):
            for (name, ll), g in zip(self.fwd_keys(l), brought):
                W[name][ll] = g if side_by_side(name) else assemble(name, g)

        def bwd_rider(self, l, grads):
            keys = self.bwd_keys(l)
            return _ExchangeRider([grads[name][ll] if side_by_side(name) else blocks_of(name, grads[name][ll]) for name, ll in keys],
                                  [side_by_side(name) for name, _ in keys])

        def bwd_done(self, l, brought):
            for key, land in zip(self.bwd_keys(l), brought):
                self.landed[key] = land

    small_shapes = [conv_w.shape, meta.shape]
    gs = _all_gather(_pack([conv_w, meta]), "gather_small")
    parts = [_unpack(gs[d], small_shapes) for d in range(N_DEV)]
    conv_full = jnp.concatenate([p[0] for p in parts], axis=2)
    meta_full = jnp.concatenate([p[1] for p in parts], axis=1)

    W = {name: [None] * depth for name in _BIG}
    W["w_in"][0] = assemble("w_in", _all_gather(shards["w_in"][0], "gather_w_in0"))
    W["conv_w"] = [conv_full[l] for l in range(depth)]
    for name in _REPLICATED:
        W[name] = [w[name][l] for l in range(depth)]
    hooks = Hooks()
    loss, grad_x, d_meta, grads = _local_step(x[0], loss_target[0], meta_full, W, hooks)

    first = blocks_of("w_in", grads["w_in"][0])[:, None]
    pairs = _pair_sum(first, _exchange_cores(first, "exchange_cores_w_in0"), "pair_sum_w_in0")
    hooks.landed[("w_in", 0)] = _exchange_chips(pairs, "exchange_chips_w_in0")[:, 0]
    out = {}
    for name in _BIG:
        out[name] = _adamw_landed([hooks.landed[(name, l)] for l in range(depth)], w[name], m[name], v[name], f"adamw_{name}")

    Fs, Ms = conv_w.shape[2], meta.shape[1]
    d_conv = jnp.stack(grads["conv_w"], axis=0)
    blocks = jnp.stack([_pack([d_conv[:, :, d * Fs:(d + 1) * Fs], d_meta[:, d * Ms:(d + 1) * Ms]]) for d in range(N_DEV)])
    land = _exchange(blocks, "exchange_small")
    res = _adamw_landed([land], _pack([conv_w, meta])[None], _pack([m_conv_w, m_meta])[None],
                        _pack([v_conv_w, v_meta])[None], "adamw_small")
    for i, r in enumerate(res):
        cw, mt = _unpack(r[0], small_shapes)
        out.setdefault("conv_w", [None] * 4)[i] = cw
        out.setdefault("meta", [None] * 4)[i] = mt

    rep_shapes = [()] + [w[name].shape for name in _REPLICATED]
    mine = _pack([loss] + [jnp.stack(grads[name], axis=0) for name in _REPLICATED])
    land = _all_gather(mine, "gather_replicated")
    zero = jnp.zeros((), F32)
    res = _adamw_landed([land], _pack([zero] + [w[n] for n in _REPLICATED])[None],
                        _pack([zero] + [m[n] for n in _REPLICATED])[None],
                        _pack([zero + 1.0] + [v[n] for n in _REPLICATED])[None], "adamw_replicated")
    for i, r in enumerate(res):
        vals = _unpack(r[0], rep_shapes)
        if i == 0:
            loss_total = vals[0]
        for name, val in zip(_REPLICATED, vals[1:]):
            out.setdefault(name, [None] * 4)[i] = val

    return (loss_total, grad_x[None], *[out[n][0] for n in _ORDER], *[out[n][1] for n in _ORDER],
            *[out[n][2] for n in _ORDER], *[out[n][3] for n in _ORDER])
```

```python
import functools
import math

import jax
import jax.numpy as jnp
from jax import lax
from jax.experimental import pallas as pl
from jax.experimental.pallas import tpu as pltpu

F32, BF16 = jnp.float32, jnp.bfloat16
HEAD_DIM = 128
LANES = 128
EPS = 1e-6
NEG_INF = -1e30
ATT_BLOCK = 256
N_DEV = 8
V7X_VMEM_BUDGET = 56 * 1024 * 1024

ADAM_LR, ADAM_B1, ADAM_B2, ADAM_EPS, ADAM_WD, ADAM_STEP = 0.001, 0.9, 0.999, 1e-08, 0.01, 10


def _divisor(n, cands):
    for c in cands:
        if c <= n and n % c == 0:
            return c
    raise ValueError(f"no tile for {n} among {cands}")


def _params(sem, est_bytes):
    limit = int(min(V7X_VMEM_BUDGET, max(16 * 1024 * 1024, est_bytes * 5 // 4 + (2 << 20))))
    return pltpu.CompilerParams(dimension_semantics=sem, vmem_limit_bytes=limit)


def _nbytes(shape, dtype):
    return math.prod(shape) * jnp.dtype(dtype).itemsize


_DN = {"nn": (((1,), (0,)), ((), ())), "nt": (((1,), (1,)), ((), ())), "tn": (((0,), (0,)), ((), ()))}
_ROW_TILES = (1088, 544, 272, 512, 256, 128, 64, 32, 16, 8)
_COL_TILES = (1024, 512, 256, 128)
MM_VMEM_BLOCKS = 40 * 1024 * 1024


def _mm_tiles(M, N, K, rows, out_bytes):
    best = None
    for tm in rows:
        for tn in _COL_TILES:
            if tm > M or tn > N or M % tm or N % tn:
                continue
            if 4 * (tm + tn) * K + 2 * tm * tn * out_bytes > MM_VMEM_BLOCKS:
                continue
            if best is None or tm * tn > best[0] * best[1]:
                best = (tm, tn, K)
    if best is not None:
        return best
    return _divisor(M, rows), _divisor(N, _COL_TILES[1:]), _ktile(K, 3072)


def _ktile(k, cap):
    if k <= cap:
        return k
    for t in range(cap - cap % LANES, 0, -LANES):
        if k % t == 0:
            return t
    raise ValueError(k)


def _mm(a, b, mode, out_dtype, name, a2=None, b2=None):
    if mode == "nn":
        (M, K), (_, N) = a.shape, b.shape
    elif mode == "nt":
        (M, K), (N, _) = a.shape, b.shape
    else:
        (K, M), (_, N) = a.shape, b.shape
    tm, tn, tk = _mm_tiles(M, N, K, _COL_TILES if mode == "tn" else _ROW_TILES, jnp.dtype(out_dtype).itemsize)
    nk = K // tk
    a_spec = {"nn": pl.BlockSpec((tm, tk), lambda i, j, k: (i, k)),
              "nt": pl.BlockSpec((tm, tk), lambda i, j, k: (i, k)),
              "tn": pl.BlockSpec((tk, tm), lambda i, j, k: (k, i))}[mode]
    b_spec = {"nn": pl.BlockSpec((tk, tn), lambda i, j, k: (k, j)),
              "nt": pl.BlockSpec((tn, tk), lambda i, j, k: (j, k)),
              "tn": pl.BlockSpec((tk, tn), lambda i, j, k: (k, j))}[mode]
    dn = _DN[mode]
    extra = a2 is not None
    in_specs, args = [a_spec, b_spec], [a, b]
    if extra:
        k2 = a2.shape[1]
        in_specs += [pl.BlockSpec((tm, k2), lambda i, j, k: (i, 0)), pl.BlockSpec((tn, k2), lambda i, j, k: (j, 0))]
        args += [a2, b2]

    def body(*refs):
        if extra:
            a_ref, b_ref, a2_ref, b2_ref, o_ref, acc = refs
        else:
            a_ref, b_ref, o_ref, acc = refs
        part = lax.dot_general(a_ref[...], b_ref[...], dn, preferred_element_type=F32)
        if nk == 1:
            if extra:
                part = part + lax.dot_general(a2_ref[...], b2_ref[...], _DN["nt"], preferred_element_type=F32)
            o_ref[...] = part.astype(o_ref.dtype)
            return
        kk = pl.program_id(2)

        @pl.when(kk == 0)
        def _():
            if extra:
                acc[...] = part + lax.dot_general(a2_ref[...], b2_ref[...], _DN["nt"], preferred_element_type=F32)
            else:
                acc[...] = part

        @pl.when(kk > 0)
        def _():
            acc[...] += part

        @pl.when(kk == nk - 1)
        def _():
            o_ref[...] = acc[...].astype(o_ref.dtype)

    est = 2 * (tm * tk + tk * tn) * 2 + 2 * _nbytes((tm, tn), out_dtype) + (tm * tn * 4 if nk > 1 else 0)
    return pl.pallas_call(
        body, name=name, grid=(M // tm, N // tn, nk), in_specs=in_specs,
        out_specs=pl.BlockSpec((tm, tn), lambda i, j, k: (i, j)),
        out_shape=jax.ShapeDtypeStruct((M, N), out_dtype),
        scratch_shapes=[pltpu.VMEM((tm, tn) if nk > 1 else (8, LANES), F32)],
        compiler_params=_params(("parallel", "parallel", "arbitrary"), est),
    )(*args)


def _rms(x, g):
    r = lax.rsqrt(jnp.mean(x * x, axis=-1, keepdims=True) + EPS)
    return x * r * g


def _rms_bwd(x, g, dout):
    r = lax.rsqrt(jnp.mean(x * x, axis=-1, keepdims=True) + EPS)
    xhat = x * r
    dxh = dout * g
    dx = r * (dxh - xhat * jnp.mean(dxh * xhat, axis=-1, keepdims=True))
    return dx, dout * xhat


def _row_tile(L):
    return _divisor(L, (272, 256, 128, 64, 32, 16))


def _resnorm_fwd(h, y, g_post, g_pre, name):
    L, D = h.shape
    bt = _row_tile(L)
    has_y, has_pre = y is not None, g_pre is not None
    row = pl.BlockSpec((bt, D), lambda i: (i, 0))
    vec = pl.BlockSpec((1, D), lambda i: (0, 0))
    args, in_specs = [h], [row]
    if has_y:
        args += [y, g_post.reshape(1, D)]
        in_specs += [row, vec]
    if has_pre:
        args += [g_pre.reshape(1, D)]
        in_specs += [vec]
    out_shape, out_specs = [], []
    if has_y:
        out_shape.append(jax.ShapeDtypeStruct((L, D), F32))
        out_specs.append(row)
    if has_pre:
        out_shape.append(jax.ShapeDtypeStruct((L, D), BF16))
        out_specs.append(row)

    def body(*refs):
        refs = list(refs)
        h_ref = refs.pop(0)
        hn = h_ref[...]
        if has_y:
            y_ref, gp_ref = refs.pop(0), refs.pop(0)
            hn = hn + _rms(y_ref[...], gp_ref[...])
        if has_pre:
            g_ref = refs.pop(0)
        if has_y:
            refs.pop(0)[...] = hn
        if has_pre:
            refs.pop(0)[...] = _rms(hn, g_ref[...]).astype(BF16)

    outs = pl.pallas_call(
        body, name=name, grid=(L // bt,), in_specs=in_specs, out_specs=out_specs, out_shape=out_shape,
        compiler_params=_params(("parallel",), 10 * bt * D * 4),
    )(*args)
    outs = list(outs)
    h_new = outs.pop(0) if has_y else h
    u = outs.pop(0) if has_pre else None
    return h_new, u


def _resnorm_bwd(dh_direct, du, h_new, y, g_post, g_pre, name):
    L, D = dh_direct.shape
    bt = _row_tile(L)
    has_y, has_pre = y is not None, du is not None
    row = pl.BlockSpec((bt, D), lambda i: (i, 0))
    vec = pl.BlockSpec((1, D), lambda i: (0, 0))
    acc = pl.BlockSpec((8, D), lambda i: (0, 0))
    args, in_specs = [dh_direct], [row]
    if has_pre:
        args += [du, h_new, g_pre.reshape(1, D)]
        in_specs += [row, row, vec]
    if has_y:
        args += [y, g_post.reshape(1, D)]
        in_specs += [row, vec]
    out_shape, out_specs = [], []
    if has_pre:
        out_shape += [jax.ShapeDtypeStruct((L, D), F32), jax.ShapeDtypeStruct((8, D), F32)]
        out_specs += [row, acc]
    if has_y:
        out_shape += [jax.ShapeDtypeStruct((L, D), BF16), jax.ShapeDtypeStruct((8, D), F32)]
        out_specs += [row, acc]

    def colsum8(v):
        return jnp.sum(v.reshape(bt // 8, 8, D), axis=0)

    def body(*refs):
        refs = list(refs)
        first = pl.program_id(0) == 0
        dh = refs.pop(0)[...]
        if has_pre:
            du_ref, hn_ref, g_ref = refs.pop(0), refs.pop(0), refs.pop(0)
        if has_y:
            y_ref, gp_ref = refs.pop(0), refs.pop(0)
        if has_pre:
            dh_ref, dgpre_ref = refs.pop(0), refs.pop(0)
            dx, dgp = _rms_bwd(hn_ref[...], g_ref[...], du_ref[...].astype(F32))
            dh = dh + dx
            dh_ref[...] = dh

            @pl.when(first)
            def _():
                dgpre_ref[...] = jnp.zeros_like(dgpre_ref)
            dgpre_ref[...] += colsum8(dgp)
        if has_y:
            dy_ref, dgpost_ref = refs.pop(0), refs.pop(0)
            dy, dgq = _rms_bwd(y_ref[...], gp_ref[...], dh)
            dy_ref[...] = dy.astype(BF16)

            @pl.when(first)
            def _():
                dgpost_ref[...] = jnp.zeros_like(dgpost_ref)
            dgpost_ref[...] += colsum8(dgq)

    outs = list(pl.pallas_call(
        body, name=name, grid=(L // bt,), in_specs=in_specs, out_specs=out_specs, out_shape=out_shape,
        compiler_params=_params(("arbitrary",), 14 * bt * D * 4),
    )(*args))
    dh, dg_pre, dy, dg_post = dh_direct, None, None, None
    if has_pre:
        dh, dg_pre = outs.pop(0), outs.pop(0).sum(0)
    if has_y:
        dy, dg_post = outs.pop(0), outs.pop(0).sum(0)
    return dh, dy, dg_post, dg_pre


def _loss(h, tgt, n_meta, seq, name):
    L, D = h.shape
    bt = _row_tile(L)
    row = pl.BlockSpec((bt, D), lambda i: (i, 0))

    def body(h_ref, t_ref, dy_ref, loss_ref):
        i = pl.program_id(0)
        r = i * bt + lax.broadcasted_iota(jnp.int32, (bt, 1), 0)
        valid = (r >= n_meta) & (r < n_meta + seq)
        e = jnp.where(valid, h_ref[...] - t_ref[...], 0.0)
        dy_ref[...] = e * (1.0 / D)

        @pl.when(i == 0)
        def _():
            loss_ref[...] = jnp.zeros_like(loss_ref)
        loss_ref[...] += 0.5 * jnp.sum(jnp.sum(e * e, axis=-1, keepdims=True) * (1.0 / D))

    dy, loss = pl.pallas_call(
        body, name=name, grid=(L // bt,), in_specs=[row, row],
        out_specs=[row, pl.BlockSpec((8, LANES), lambda i: (0, 0))],
        out_shape=[jax.ShapeDtypeStruct((L, D), F32), jax.ShapeDtypeStruct((8, LANES), F32)],
        compiler_params=_params(("arbitrary",), 8 * bt * D * 4),
    )(h, tgt)
    return loss[0, 0], dy


def _split3(x):
    x1 = x.astype(BF16)
    r1 = x - x1.astype(F32)
    x2 = r1.astype(BF16)
    x3 = (r1 - x2.astype(F32)).astype(BF16)
    return x1, x2, x3


def _tri_dot3(tri, x):
    x1, x2, x3 = _split3(x)
    d = functools.partial(jnp.dot, preferred_element_type=F32)
    return d(tri, x1) + d(tri, x2) + d(tri, x3)


def _gate_fwd(fl, b_pad, name):
    L = fl.shape[0]
    bt = 128
    blk = pl.BlockSpec((bt, LANES), lambda i: (i, 0))

    def body(fl_ref, b_ref, c_ref, carry):
        @pl.when(pl.program_id(0) == 0)
        def _():
            carry[...] = jnp.zeros_like(carry)
        x = fl_ref[...] + b_ref[...]
        lf = jnp.minimum(x, 0.0) - jnp.log(1.0 + jnp.exp(-jnp.abs(x)))
        r = lax.broadcasted_iota(jnp.int32, (bt, bt), 0)
        s = lax.broadcasted_iota(jnp.int32, (bt, bt), 1)
        tri = (s <= r).astype(BF16)
        c = _tri_dot3(tri, lf) + carry[...]
        c_ref[...] = c
        carry[...] = c[bt - 1:bt, :]

    return pl.pallas_call(
        body, name=name, grid=(L // bt,), in_specs=[blk, pl.BlockSpec((1, LANES), lambda i: (0, 0))],
        out_specs=blk, out_shape=jax.ShapeDtypeStruct((L, LANES), F32),
        scratch_shapes=[pltpu.VMEM((1, LANES), F32)],
        compiler_params=_params(("arbitrary",), 1 << 20),
    )(fl, b_pad)


def _gate_bwd(dc, fl, b_pad, name):
    L = fl.shape[0]
    bt = 128
    n = L // bt
    blk = pl.BlockSpec((bt, LANES), lambda i: (n - 1 - i, 0))

    def body(dc_ref, fl_ref, b_ref, dfl_ref, db_ref, carry):
        @pl.when(pl.program_id(0) == 0)
        def _():
            carry[...] = jnp.zeros_like(carry)
            db_ref[...] = jnp.zeros_like(db_ref)
        r = lax.broadcasted_iota(jnp.int32, (bt, bt), 0)
        s = lax.broadcasted_iota(jnp.int32, (bt, bt), 1)
        tri = (s >= r).astype(BF16)
        dlf = _tri_dot3(tri, dc_ref[...]) + carry[...]
        carry[...] = dlf[0:1, :]
        x = fl_ref[...] + b_ref[...]
        dfl = dlf / (1.0 + jnp.exp(x))
        dfl_ref[...] = dfl.astype(BF16)
        db_ref[...] += jnp.sum(dfl.reshape(bt // 8, 8, LANES), axis=0)

    dfl, db = pl.pallas_call(
        body, name=name, grid=(n,), in_specs=[blk, blk, pl.BlockSpec((1, LANES), lambda i: (0, 0))],
        out_specs=[blk, pl.BlockSpec((8, LANES), lambda i: (0, 0))],
        out_shape=[jax.ShapeDtypeStruct((L, LANES), BF16), jax.ShapeDtypeStruct((8, LANES), F32)],
        scratch_shapes=[pltpu.VMEM((1, LANES), F32)],
        compiler_params=_params(("arbitrary",), 1 << 20),
    )(dc, fl, b_pad)
    return dfl, db.sum(0)


_MESH = pl.DeviceIdType.MESH
_ANY = pl.BlockSpec(memory_space=pl.ANY)


def _flip(v, bit):
    return 1 - v if bit else v


class _GatherRider:
    def __init__(self, blocks, side_by_side):
        self.blocks = list(blocks)
        self.side_by_side = list(side_by_side)
        self.n = len(self.blocks)
        self.in_specs = [_ANY] * self.n
        self.out_specs = [_ANY] * self.n
        self.out_shape = [jax.ShapeDtypeStruct((b.shape[0], N_DEV * b.shape[1]) if cols else (N_DEV, *b.shape), b.dtype)
                          for b, cols in zip(self.blocks, self.side_by_side)]
        self.scratch = [pltpu.SemaphoreType.DMA((7 * self.n,)), pltpu.SemaphoreType.DMA((7 * self.n,)),
                        pltpu.SemaphoreType.DMA((self.n,))]

    def _copies(self, a, x_ref, out_ref, send_sems, recv_sems):
        mx, my, mc = lax.axis_index("x"), lax.axis_index("y"), lax.axis_index("c")
        me, sibling = (mx, my, mc), (mx, my, 1 - mc)
        chips = [(1 - mx, my), (mx, 1 - my), (1 - mx, 1 - my)]
        width = self.blocks[a].shape[1]

        def slot(px, py, pc):
            d = 4 * px + 2 * py + pc
            if self.side_by_side[a]:
                return out_ref.at[:, pl.ds(pl.multiple_of(d * width, LANES), width)]
            return out_ref.at[d]

        def copy(k, block, to, src=None):
            return pltpu.make_async_remote_copy(
                src_ref=slot(*block) if src is None else src, dst_ref=slot(*block),
                send_sem=send_sems.at[7 * a + k], recv_sem=recv_sems.at[7 * a + k], device_id=to, device_id_type=_MESH)

        first = [copy(0, me, sibling, src=x_ref)] + [copy(1 + j, me, (*chip, mc), src=x_ref) for j, chip in enumerate(chips)]
        landed = [copy(1 + j, (*chip, mc), me) for j, chip in enumerate(chips)]
        passed = [copy(4 + j, (*chip, mc), sibling) for j, chip in enumerate(chips)]
        last = [copy(0, sibling, me)] + [copy(4 + j, (*chip, 1 - mc), me) for j, chip in enumerate(chips)]
        return slot(*me), first, landed, passed, last

    def start(self, ins, outs, send_sems, recv_sems, local_sems):
        for a in range(self.n):
            mine, first, _, _, _ = self._copies(a, ins[a], outs[a], send_sems, recv_sems)
            pltpu.make_async_copy(ins[a], mine, local_sems.at[a]).start()
            for cp in first:
                cp.start()

    def middle(self, ins, outs, send_sems, recv_sems, local_sems):
        for a in range(self.n):
            _, _, landed, passed, _ = self._copies(a, ins[a], outs[a], send_sems, recv_sems)
            for arrived, onward in zip(landed, passed):
                arrived.wait_recv()
                onward.start()

    def finish(self, ins, outs, send_sems, recv_sems, local_sems):
        for a in range(self.n):
            mine, first, _, passed, last = self._copies(a, ins[a], outs[a], send_sems, recv_sems)
            for cp in last:
                cp.wait_recv()
            for cp in first + passed:
                cp.wait_send()
            pltpu.make_async_copy(ins[a], mine, local_sems.at[a]).wait()


class _ExchangeRider:
    def __init__(self, arrays, side_by_side):
        self.blocks = list(arrays)
        self.side_by_side = list(side_by_side)
        self.n = len(self.blocks)
        self.in_specs = [_ANY] * self.n
        self.out_specs = [_ANY] * self.n
        self.out_shape = [jax.ShapeDtypeStruct((N_DEV, b.shape[0], b.shape[1] // N_DEV) if cols else b.shape, b.dtype)
                          for b, cols in zip(self.blocks, self.side_by_side)]
        self.scratch = [pltpu.SemaphoreType.DMA((7 * self.n,)), pltpu.SemaphoreType.DMA((7 * self.n,)),
                        pltpu.SemaphoreType.DMA((self.n,))]

    def _copies(self, a, g_ref, land_ref, send_sems, recv_sems):
        mx, my, mc = lax.axis_index("x"), lax.axis_index("y"), lax.axis_index("c")
        me = 4 * mx + 2 * my + mc
        width = self.out_shape[a].shape[2]

        def block(j):
            if self.side_by_side[a]:
                return g_ref.at[:, pl.ds(pl.multiple_of(j * width, LANES), width)]
            return g_ref.at[j]

        sends, recvs = [], []
        for k in range(1, N_DEV):
            peer = (_flip(mx, k & 4), _flip(my, k & 2), _flip(mc, k & 1))
            p = 4 * peer[0] + 2 * peer[1] + peer[2]
            sems = dict(send_sem=send_sems.at[7 * a + k - 1], recv_sem=recv_sems.at[7 * a + k - 1], device_id=peer, device_id_type=_MESH)
            sends.append(pltpu.make_async_remote_copy(src_ref=block(p), dst_ref=land_ref.at[me], **sems))
            recvs.append(pltpu.make_async_remote_copy(src_ref=block(p), dst_ref=land_ref.at[p], **sems))
        return block(me), land_ref.at[me], sends, recvs

    def start(self, ins, outs, send_sems, recv_sems, local_sems):
        for a in range(self.n):
            src, dst, sends, _ = self._copies(a, ins[a], outs[a], send_sems, recv_sems)
            pltpu.make_async_copy(src, dst, local_sems.at[a]).start()
            for cp in sends:
                cp.start()

    def middle(self, ins, outs, send_sems, recv_sems, local_sems):
        pass

    def finish(self, ins, outs, send_sems, recv_sems, local_sems):
        for a in range(self.n):
            src, dst, sends, recvs = self._copies(a, ins[a], outs[a], send_sems, recv_sems)
            for cp in recvs:
                cp.wait_recv()
            for cp in sends:
                cp.wait_send()
            pltpu.make_async_copy(src, dst, local_sems.at[a]).wait()


class _NoRider:
    blocks, in_specs, out_specs, out_shape, scratch = [], [], [], [], []


_NO_RIDER = _NoRider()


def _ride(rider, refs, n_in, n_out, first, middle, last, work):
    if rider is None:
        return work(*refs)
    r = rider.n
    own = refs[:n_in] + refs[n_in + r:n_in + r + n_out] + refs[n_in + r + n_out + r:len(refs) - 3]
    args = (refs[n_in:n_in + r], refs[n_in + r + n_out:n_in + r + n_out + r], *refs[len(refs) - 3:])
    pl.when(first)(lambda: rider.start(*args))
    pl.when(middle)(lambda: rider.middle(*args))
    work(*own)
    pl.when(last)(lambda: rider.finish(*args))


def _split2(x):
    hi = x.astype(BF16)
    return hi, (x - hi.astype(F32)).astype(BF16)


def _dot_hi_lo(x, w2):
    hi, lo = _split2(x)
    return jnp.dot(jnp.concatenate([hi, lo], axis=1), w2, preferred_element_type=F32)


def _suffix_sums(x, tri2, exact):
    if exact:
        return _dot_hi_lo(x, tri2)
    return jnp.dot(x.astype(BF16), tri2[:x.shape[1]], preferred_element_type=F32)


def _dot_nt(a, b):
    return lax.dot_general(a, b, _DN["nt"], preferred_element_type=F32)


def _suffix_matrix(bk):
    j = lax.broadcasted_iota(jnp.int32, (2 * bk, bk), 0) % bk
    s = lax.broadcasted_iota(jnp.int32, (2 * bk, bk), 1)
    return (j >= s).astype(BF16)


LOG2E = 1.4426950408889634


def _sb_window(q, k, t_pos, ks, bk, scale2, carry_c, tri, masked):
    width = k.shape[0]
    z = _dot_nt(q, k) * scale2
    sp = jnp.maximum(z, 0.0) + jnp.log2(1.0 + jnp.exp2(-jnp.abs(z)))
    if masked:
        mask = ks + lax.broadcasted_iota(jnp.int32, (1, width), 1) < t_pos
        lkm = jnp.where(mask, -sp, 0.0)
    else:
        mask, lkm = None, -sp
    sums, run = _window_suffix_sums(lkm, bk, tri, carry_c, False)
    a = jnp.exp2(z + sums)
    if masked:
        a = jnp.where(mask, a, 0.0)
    return z, sp, mask, a, run


def _window_suffix_sums(x, bk, tri, carry, exact):
    nb = x.shape[1] // bk
    parts, run = [None] * nb, carry
    for b in reversed(range(nb)):
        cs = _suffix_sums(x[:, b * bk:(b + 1) * bk], tri, exact)
        parts[b] = run + cs
        run = run + cs[:, 0:1]
    return (parts[0] if nb == 1 else jnp.concatenate(parts, axis=1)), run


KEY_WINDOW = 4
KEY_WINDOW_WIDE = 8


def _key_tiles(i, bk, step, carry):
    r = i % KEY_WINDOW

    def first(w):
        return lambda c: step(pl.multiple_of((i - w) * bk, bk), (w + 1) * bk, c, True)

    carry = lax.switch(r, [first(w) for w in range(KEY_WINDOW)], carry)
    top = i - r
    odd = (top // KEY_WINDOW) % (KEY_WINDOW_WIDE // KEY_WINDOW)
    carry = lax.fori_loop(
        0, odd, lambda p, c: step(pl.multiple_of((top - KEY_WINDOW) * bk, bk), KEY_WINDOW * bk, c, False), carry)
    top = top - odd * KEY_WINDOW
    return lax.fori_loop(
        0, top // KEY_WINDOW_WIDE,
        lambda p, c: step(pl.multiple_of((top - KEY_WINDOW_WIDE * (p + 1)) * bk, bk), KEY_WINDOW_WIDE * bk, c, False), carry)


def _attn_specs(H, L, bq):
    W3 = 3 * H

    def col(role):
        return lambda h, i: (h // H) * W3 + role * H + h % H

    q_spec = pl.BlockSpec((bq, HEAD_DIM), lambda h, i: (i, col(0)(h, i)))
    k_spec = pl.BlockSpec((L, HEAD_DIM), lambda h, i: (0, col(1)(h, i)))
    v_spec = pl.BlockSpec((L, HEAD_DIM), lambda h, i: (0, col(2)(h, i)))
    crow_spec = pl.BlockSpec((None, 1, L), lambda h, i: (jnp.maximum(h - H, 0), 0, 0))
    ccol_spec = pl.BlockSpec((None, bq, 1), lambda h, i: (jnp.maximum(h - H, 0), i, 0))
    g_spec = pl.BlockSpec((None, 1, HEAD_DIM), lambda h, i: (h, 0, 0))
    tile = pl.BlockSpec((bq, HEAD_DIM), lambda h, i: (i, h))
    stat = pl.BlockSpec((None, bq, 1), lambda h, i: (h, i, 0))
    return q_spec, k_spec, v_spec, crow_spec, ccol_spec, g_spec, tile, stat


def _attn_fwd(qkv, crow, ccol, g_heads, H, name, rider=None):
    L = qkv.shape[0]
    bq = bk = min(ATT_BLOCK, L)
    nq = L // bq
    scale = HEAD_DIM ** -0.5
    scale2 = scale * LOG2E
    q_spec, k_spec, v_spec, crow_spec, ccol_spec, g_spec, tile, stat = _attn_specs(H, L, bq)

    def work(q_ref, k_ref, v_ref, crow_ref, ccol_ref, g_ref, o_ref, on_ref, lse_ref):
        h, i = pl.program_id(0), pl.program_id(1)
        q = q_ref[...]
        t_pos = i * bq + lax.broadcasted_iota(jnp.int32, (bq, 1), 0)

        def finish(o):
            o_ref[...] = o
            on_ref[...] = _rms(o, g_ref[...]).astype(BF16)

        @pl.when(h < H)
        def _stick_breaking():
            tri = _suffix_matrix(bk)

            def step(ks, rows, carry, masked):
                c, acc = carry
                k = k_ref[pl.ds(ks, rows), :]
                v = v_ref[pl.ds(ks, rows), :]
                _, _, _, a, c = _sb_window(q, k, t_pos, ks, bk, scale2, c, tri, masked)
                acc = acc + jnp.dot(a.astype(BF16), v, preferred_element_type=F32)
                return c, acc

            _, acc = _key_tiles(i, bk, step, (jnp.zeros((bq, 1), F32), jnp.zeros((bq, HEAD_DIM), F32)))
            finish(acc)
            lse_ref[...] = jnp.zeros_like(lse_ref)

        @pl.when(h >= H)
        def _forgetting():
            cq = ccol_ref[...] * LOG2E

            def step(ks, rows, carry, masked):
                m, l, acc = carry
                k = k_ref[pl.ds(ks, rows), :]
                v = v_ref[pl.ds(ks, rows), :]
                s = _dot_nt(q, k) * scale2 + (cq - crow_ref[:, pl.ds(ks, rows)] * LOG2E)
                if masked:
                    s = jnp.where(ks + lax.broadcasted_iota(jnp.int32, (1, rows), 1) <= t_pos, s, NEG_INF)
                m_new = jnp.maximum(m, jnp.max(s, axis=-1, keepdims=True))
                alpha = jnp.exp2(m - m_new)
                p = jnp.exp2(s - m_new)
                l = alpha * l + jnp.sum(p, axis=-1, keepdims=True)
                acc = alpha * acc + _dot_hi_lo(p, jnp.concatenate([v, v], axis=0))
                return m_new, l, acc

            init = (jnp.full((bq, 1), NEG_INF, F32), jnp.zeros((bq, 1), F32), jnp.zeros((bq, HEAD_DIM), F32))
            m, l, acc = _key_tiles(i, bk, step, init)
            finish(acc / l)
            lse_ref[...] = m + jnp.log2(l)

    def body(*refs):
        h, i = pl.program_id(0), pl.program_id(1)
        _ride(rider, refs, 6, 3, (h == 0) & (i == 0), (h == 2 * H - 2) & (i == 0), (h == 2 * H - 1) & (i == nq - 1), work)

    W2 = 2 * H * HEAD_DIM
    est = 4 * L * HEAD_DIM * 2 + 14 * bq * KEY_WINDOW_WIDE * bk * 4 + (4 << 20)
    extra = rider or _NO_RIDER
    outs = pl.pallas_call(
        body, name=name, grid=(2 * H, nq),
        in_specs=[q_spec, k_spec, v_spec, crow_spec, ccol_spec, g_spec] + extra.in_specs,
        out_specs=[tile, tile, stat] + extra.out_specs,
        out_shape=[jax.ShapeDtypeStruct((L, W2), F32), jax.ShapeDtypeStruct((L, W2), BF16),
                   jax.ShapeDtypeStruct((2 * H, L, 1), F32)] + extra.out_shape,
        scratch_shapes=extra.scratch,
        compiler_params=_params(("arbitrary", "arbitrary"), est),
    )(qkv, qkv, qkv, crow, ccol, g_heads, *extra.blocks)
    return outs[0], outs[1], outs[2], list(outs[3:])


def _attn_bwd(qkv, crow, ccol, g_heads, o, lse, d_on, H, name, rider=None):
    L = qkv.shape[0]
    bq = bk = min(ATT_BLOCK, L)
    nq = L // bq
    scale = HEAD_DIM ** -0.5
    scale2 = scale * LOG2E
    q_spec, k_spec, v_spec, crow_spec, ccol_spec, g_spec, tile, stat = _attn_specs(H, L, bq)
    full = pl.BlockSpec((L, HEAD_DIM), lambda h, i: (0, h))
    dg_spec = pl.BlockSpec((None, 1, HEAD_DIM), lambda h, i: (h, 0, 0))
    dc_spec = pl.BlockSpec((None, 1, L), lambda h, i: (h, 0, 0))

    def work(q_ref, k_ref, v_ref, crow_ref, ccol_ref, g_ref, o_ref, lse_ref, don_ref,
             dq_ref, dk_ref, dv_ref, dg_ref, dc_ref, dkt_acc, dvt_acc):
        h, i = pl.program_id(0), pl.program_id(1)

        @pl.when(i == 0)
        def _():
            dkt_acc[...] = jnp.zeros_like(dkt_acc)
            dvt_acc[...] = jnp.zeros_like(dvt_acc)
            dg_ref[...] = jnp.zeros_like(dg_ref)
            dc_ref[...] = jnp.zeros_like(dc_ref)

        q = q_ref[...]
        t_pos = i * bq + lax.broadcasted_iota(jnp.int32, (bq, 1), 0)
        o_t = o_ref[...]
        d_o, dg = _rms_bwd(o_t, g_ref[...], don_ref[...])
        dg_ref[...] += jnp.sum(dg, axis=0, keepdims=True)
        d_ob = d_o.astype(BF16)
        dsum = jnp.sum(d_ob.astype(F32) * o_t, axis=-1, keepdims=True)
        q_t = q.astype(F32).T.astype(BF16)
        d_obt = d_o.T.astype(BF16)

        @pl.when(h < H)
        def _stick_breaking():
            tri = _suffix_matrix(bk)

            def step(ks, rows, carry, masked):
                c, gs, dq = carry
                k = k_ref[pl.ds(ks, rows), :]
                v = v_ref[pl.ds(ks, rows), :]
                z, sp, mask, a, c = _sb_window(q, k, t_pos, ks, bk, scale2, c, tri, masked)
                a_b = a.astype(BF16)
                g_w = a_b.astype(F32) * _dot_nt(d_ob, v)
                later, gs = _window_suffix_sums(g_w, bk, tri, gs, True)
                prefix = dsum - (later - g_w)
                dz = (g_w - jnp.exp2(z - sp) * prefix) * scale
                if masked:
                    dz = jnp.where(mask, dz, 0.0)
                dzb = dz.astype(BF16)
                dq = dq + jnp.dot(dzb, k, preferred_element_type=F32)
                dkt_acc[:, pl.ds(ks, rows)] += jnp.dot(q_t, dzb, preferred_element_type=F32)
                dvt_acc[:, pl.ds(ks, rows)] += jnp.dot(d_obt, a_b, preferred_element_type=F32)
                return c, gs, dq

            z1 = jnp.zeros((bq, 1), F32)
            _, _, dq = _key_tiles(i, bk, step, (z1, z1, jnp.zeros((bq, HEAD_DIM), F32)))
            dq_ref[...] = dq.astype(BF16)

        @pl.when(h >= H)
        def _forgetting():
            cq = ccol_ref[...] * LOG2E - lse_ref[...]

            def step(ks, rows, dq, masked):
                k = k_ref[pl.ds(ks, rows), :]
                v = v_ref[pl.ds(ks, rows), :]
                p = jnp.exp2(_dot_nt(q, k) * scale2 + (cq - crow_ref[:, pl.ds(ks, rows)] * LOG2E))
                if masked:
                    p = jnp.where(ks + lax.broadcasted_iota(jnp.int32, (1, rows), 1) <= t_pos, p, 0.0)
                ds = p * (_dot_nt(d_ob, v) - dsum)
                dsb = (ds * scale).astype(BF16)
                dq = dq + jnp.dot(dsb, k, preferred_element_type=F32)
                dkt_acc[:, pl.ds(ks, rows)] += jnp.dot(q_t, dsb, preferred_element_type=F32)
                dvt_acc[:, pl.ds(ks, rows)] += jnp.dot(d_obt, p.astype(BF16), preferred_element_type=F32)
                dc_ref[:, pl.ds(ks, rows)] += -jnp.sum(ds, axis=0, keepdims=True)
                return dq

            dq = _key_tiles(i, bk, step, jnp.zeros((bq, HEAD_DIM), F32))
            dq_ref[...] = dq.astype(BF16)

        @pl.when(i == nq - 1)
        def _():
            dk_ref[...] = dkt_acc[...].T.astype(BF16)
            dv_ref[...] = dvt_acc[...].T.astype(BF16)

    W2 = 2 * H * HEAD_DIM
    est = 4 * L * HEAD_DIM * 2 + 4 * L * HEAD_DIM * 2 + 2 * L * HEAD_DIM * 4 + 18 * bq * KEY_WINDOW_WIDE * bk * 4 + (4 << 20)
    def body(*refs):
        h, i = pl.program_id(0), pl.program_id(1)
        _ride(rider, refs, 9, 5, (h == 0) & (i == 0), (h == H) & (i == 0), (h == 2 * H - 1) & (i == nq - 1), work)

    extra = rider or _NO_RIDER
    dq, dk, dv, dg, dc, *brought = pl.pallas_call(
        body, name=name, grid=(2 * H, nq),
        in_specs=[q_spec, k_spec, v_spec, crow_spec, ccol_spec, g_spec, tile, stat, tile] + extra.in_specs,
        out_specs=[tile, full, full, dg_spec, dc_spec] + extra.out_specs,
        out_shape=[jax.ShapeDtypeStruct((L, W2), BF16)] * 3
        + [jax.ShapeDtypeStruct((2 * H, 1, HEAD_DIM), F32), jax.ShapeDtypeStruct((2 * H, 1, L), F32)] + extra.out_shape,
        scratch_shapes=[pltpu.VMEM((HEAD_DIM, L), F32), pltpu.VMEM((HEAD_DIM, L), F32)] + extra.scratch,
        compiler_params=_params(("arbitrary", "arbitrary"), est),
    )(qkv, qkv, qkv, crow, ccol, g_heads, o, lse, d_on, *extra.blocks)
    return dq, dk, dv, dg[:, 0, :], dc, brought


HALO = 16


def _conv_tiles(L, F, rows):
    return _divisor(L, tuple(t for t in _ROW_TILES if t <= rows and t % HALO == 0)), _divisor(F, (512, 256, 128))


SUBLANES = 8


def _shift_down(x, before, bt):
    ext = jnp.concatenate([before, x], axis=0)
    return pltpu.roll(ext, 1, 0)[SUBLANES:], pltpu.roll(ext, 2, 0)[SUBLANES:]


def _shift_up(x, after, bt):
    ext = jnp.concatenate([x, after], axis=0)
    return pltpu.roll(ext, bt + SUBLANES - 1, 0)[:bt], pltpu.roll(ext, bt + SUBLANES - 2, 0)[:bt]


def _conv_rows(p_ref, halo_ref, w_ref, b_ref, first, bt):
    p = p_ref[...].astype(F32)
    before = jnp.where(first, 0.0, halo_ref[...].astype(F32)[HALO - SUBLANES:HALO, :])
    p1, p2 = _shift_down(p, before, bt)
    w = w_ref[...]
    a = w[0:1, :] * p2 + w[1:2, :] * p1 + w[2:3, :] * p + b_ref[...]
    return a, p, p1, p2


def _sigmoid(x):
    return 0.5 + 0.5 * jnp.tanh(0.5 * x)


def _conv_in_specs(L, F, bt, bc, order):
    nf = F // bc
    r = bt // HALO
    ix = (lambda a, b: (a, b)) if order == "ij" else (lambda a, b: (b, a))

    def mk(shape, fn):
        return pl.BlockSpec(shape, lambda a, b: fn(*ix(a, b)))

    return [
        mk((bt, bc), lambda i, j: (i, j)), mk((HALO, bc), lambda i, j: (jnp.maximum(i * r - 1, 0), j)),
        mk((bt, bc), lambda i, j: (i, nf + j)), mk((HALO, bc), lambda i, j: (jnp.maximum(i * r - 1, 0), nf + j)),
        mk((3, bc), lambda i, j: (0, j)), mk((3, bc), lambda i, j: (0, nf + j)),
        mk((1, bc), lambda i, j: (0, j)), mk((1, bc), lambda i, j: (0, nf + j)),
    ]


def _convgate_fwd(p, conv_w, conv_b, name):
    L, F2 = p.shape
    F = F2 // 2
    bt, bc = _conv_tiles(L, F, 544)

    def body(pg, hg, pu, hu, wg, wu, bg, bu, act_ref):
        first = pl.program_id(0) == 0
        ag = _conv_rows(pg, hg, wg, bg, first, bt)[0]
        au = _conv_rows(pu, hu, wu, bu, first, bt)[0]
        act_ref[...] = (ag * _sigmoid(ag) * au).astype(BF16)

    return pl.pallas_call(
        body, name=name, grid=(L // bt, F // bc), in_specs=_conv_in_specs(L, F, bt, bc, "ij"),
        out_specs=pl.BlockSpec((bt, bc), lambda i, j: (i, j)), out_shape=jax.ShapeDtypeStruct((L, F), BF16),
        compiler_params=_params(("parallel", "parallel"), 24 * bt * bc * 4),
    )(p, p, p, p, conv_w, conv_w, conv_b.reshape(1, F2), conv_b.reshape(1, F2))


def _convgate_bwd(p, conv_w, conv_b, d_act, name):
    L, F2 = p.shape
    F = F2 // 2
    bt, bc = _conv_tiles(L, F, 544)

    def body(pg, hg, pu, hu, wg, wu, bg, bu, dact_ref, da_ref, dwb_ref):
        first = pl.program_id(1) == 0
        ag, xg, xg1, xg2 = _conv_rows(pg, hg, wg, bg, first, bt)
        au, xu, xu1, xu2 = _conv_rows(pu, hu, wu, bu, first, bt)
        d_act = dact_ref[...].astype(F32)
        sg = _sigmoid(ag)
        dag = d_act * au * sg * (1.0 + ag * (1.0 - sg))
        dau = d_act * ag * sg
        da_ref[0] = dag.astype(BF16)
        da_ref[1] = dau.astype(BF16)

        @pl.when(first)
        def _():
            dwb_ref[...] = jnp.zeros_like(dwb_ref)
        cs = lambda v: jnp.sum(v, axis=0, keepdims=True)
        dwb_ref[0] += jnp.concatenate([cs(dag * xg2), cs(dag * xg1), cs(dag * xg), cs(dag)], axis=0)
        dwb_ref[1] += jnp.concatenate([cs(dau * xu2), cs(dau * xu1), cs(dau * xu), cs(dau)], axis=0)

    da, dwb = pl.pallas_call(
        body, name=name, grid=(F // bc, L // bt),
        in_specs=_conv_in_specs(L, F, bt, bc, "ji") + [pl.BlockSpec((bt, bc), lambda j, i: (i, j))],
        out_specs=[pl.BlockSpec((2, bt, bc), lambda j, i: (0, i, j)), pl.BlockSpec((2, 4, bc), lambda j, i: (0, 0, j))],
        out_shape=[jax.ShapeDtypeStruct((2, L, F), BF16), jax.ShapeDtypeStruct((2, 4, F), F32)],
        compiler_params=_params(("parallel", "arbitrary"), 40 * bt * bc * 4),
    )(p, p, p, p, conv_w, conv_w, conv_b.reshape(1, F2), conv_b.reshape(1, F2), d_act)
    d_w = jnp.concatenate([dwb[0, 0:3], dwb[1, 0:3]], axis=1)
    d_b = jnp.concatenate([dwb[0, 3], dwb[1, 3]], axis=0)
    return da, d_w, d_b


def _conv_bwd_data(da, conv_w, name):
    _, L, F = da.shape
    bt, bc = _conv_tiles(L, F, 1088)
    nf, r, nt = F // bc, bt // HALO, L // bt

    def body(da_ref, nxt_ref, w_ref, dp_ref):
        last = pl.program_id(0) == nt - 1
        x = da_ref[...].astype(F32)
        after = jnp.where(last, 0.0, nxt_ref[...].astype(F32)[0:SUBLANES, :])
        x1, x2 = _shift_up(x, after, bt)
        w = w_ref[...]
        dp_ref[...] = (w[2:3, :] * x + w[1:2, :] * x1 + w[0:1, :] * x2).astype(BF16)

    return pl.pallas_call(
        body, name=name, grid=(nt, 2 * nf),
        in_specs=[pl.BlockSpec((None, bt, bc), lambda i, j: (j // nf, i, j % nf)),
                  pl.BlockSpec((None, HALO, bc), lambda i, j: (j // nf, jnp.minimum((i + 1) * r, nt * r - 1), j % nf)),
                  pl.BlockSpec((3, bc), lambda i, j: (0, j))],
        out_specs=pl.BlockSpec((bt, bc), lambda i, j: (i, j)), out_shape=jax.ShapeDtypeStruct((L, 2 * F), BF16),
        compiler_params=_params(("parallel", "parallel"), 16 * bt * bc * 4),
    )(da, da, conv_w)


def _pad_cols(a, n):
    return jnp.pad(a, ((0, 0), (0, n - a.shape[1])))


def _local_step(x, tgt, meta, W, hooks=None):
    S, D = x.shape
    n_meta = meta.shape[0]
    depth = len(W["w_in"])
    H = D // (2 * HEAD_DIM)
    WQ = 6 * H * HEAD_DIM
    L = -(-(S + n_meta) // ATT_BLOCK) * ATT_BLOCK
    tail = L - S - n_meta
    zeros_tail = jnp.zeros((tail, D), F32)
    h = jnp.concatenate([meta, x, zeros_tail], axis=0)
    tgt_p = jnp.concatenate([jnp.zeros((n_meta, D), F32), tgt, zeros_tail], axis=0)

    saved = []
    _, u1 = _resnorm_fwd(h, None, None, W["g_mix_pre"][0], "prenorm0")
    for l in range(depth):
        w_in = W["w_in"][l]
        w_qkv, w_f = w_in[:, :WQ], _pad_cols(w_in[:, WQ:], LANES)
        b_pad = jnp.pad(W["b_f"][l], (0, LANES - H)).reshape(1, LANES)
        g_heads = jnp.concatenate([W["g_sb"][l], W["g_fox"][l]], axis=0).reshape(2 * H, 1, HEAD_DIM)
        qkv = _mm(u1, w_qkv, "nn", BF16, f"qkv{l}")
        fl = _mm(u1, w_f, "nn", F32, f"flogit{l}")
        c = _gate_fwd(fl, b_pad, f"gate_fwd{l}")
        c_heads = c[:, :H].T
        crow, ccol = c_heads[:, None, :], c_heads[:, :, None]
        o, on, lse, brought = _attn_fwd(qkv, crow, ccol, g_heads, H, f"attn_fwd{l}", hooks and hooks.fwd_rider(l))
        if hooks:
            hooks.fwd_done(l, brought, W)
        mix = _mm(on, W["w_out"][l], "nn", F32, f"mix{l}")
        h_mid, u2 = _resnorm_fwd(h, mix, W["g_mix_post"][l], W["g_ffn_pre"][l], f"resnorm_a{l}")
        p = _mm(u2, W["w_up"][l], "nn", BF16, f"up{l}")
        act = _convgate_fwd(p, W["conv_w"][l], W["conv_b"][l], f"convgate{l}")
        ff = _mm(act, W["w_down"][l], "nn", F32, f"down{l}")
        g_next = W["g_mix_pre"][l + 1] if l + 1 < depth else None
        h_out, u1_next = _resnorm_fwd(h_mid, ff, W["g_ffn_post"][l], g_next, f"resnorm_b{l}")
        saved.append(dict(h_in=h, u1=u1, w_qkv=w_qkv, w_f=w_f, b_pad=b_pad, g_heads=g_heads, qkv=qkv, fl=fl, crow=crow,
                          ccol=ccol, o=o, on=on, lse=lse, mix=mix, h_mid=h_mid, u2=u2, p=p, act=act, ff=ff, h_out=h_out))
        h, u1 = h_out, u1_next

    loss, dh = _loss(h, tgt_p, n_meta, S, "loss")

    grads = {k: [None] * depth for k in ("g_mix_pre", "w_in", "b_f", "g_sb", "g_fox", "w_out", "g_mix_post", "g_ffn_pre",
                                         "w_up", "conv_w", "conv_b", "w_down", "g_ffn_post")}
    du1_next = None
    for l in reversed(range(depth)):
        s = saved[l]
        g_next = W["g_mix_pre"][l + 1] if l + 1 < depth else None
        dh, d_ff, grads["g_ffn_post"][l], dg_pre_next = _resnorm_bwd(
            dh, du1_next, s["h_out"], s["ff"], W["g_ffn_post"][l], g_next, f"resnorm_b_bwd{l}")
        if l + 1 < depth:
            grads["g_mix_pre"][l + 1] = dg_pre_next
        d_act = _mm(d_ff, W["w_down"][l], "nt", BF16, f"d_act{l}")
        grads["w_down"][l] = _mm(s["act"], d_ff, "tn", BF16, f"dw_down{l}")
        da, grads["conv_w"][l], grads["conv_b"][l] = _convgate_bwd(s["p"], W["conv_w"][l], W["conv_b"][l], d_act, f"convgate_bwd{l}")
        dp = _conv_bwd_data(da, W["conv_w"][l], f"conv_bwd{l}")
        du2 = _mm(dp, W["w_up"][l], "nt", F32, f"d_u2{l}")
        grads["w_up"][l] = _mm(s["u2"], dp, "tn", BF16, f"dw_up{l}")
        dh, d_mix, grads["g_mix_post"][l], grads["g_ffn_pre"][l] = _resnorm_bwd(
            dh, du2, s["h_mid"], s["mix"], W["g_mix_post"][l], W["g_ffn_pre"][l], f"resnorm_a_bwd{l}")
        d_on = _mm(d_mix, W["w_out"][l], "nt", F32, f"d_on{l}")
        grads["w_out"][l] = _mm(s["on"], d_mix, "tn", BF16, f"dw_out{l}")
        dq, dk, dv, dg_heads, dcrow, brought = _attn_bwd(s["qkv"], s["crow"], s["ccol"], s["g_heads"], s["o"], s["lse"], d_on, H,
                                                         f"attn_bwd{l}", hooks and hooks.bwd_rider(l, grads))
        if hooks:
            hooks.bwd_done(l, brought)
        grads["g_sb"][l], grads["g_fox"][l] = dg_heads[:H], dg_heads[H:]
        dc = _pad_cols(dcrow[H:, 0, :].T, LANES)
        dfl, db = _gate_bwd(dc, s["fl"], s["b_pad"], f"gate_bwd{l}")
        grads["b_f"][l] = db[:H]
        Wh = H * HEAD_DIM
        d_qkv = jnp.concatenate([dq[:, :Wh], dk[:, :Wh], dv[:, :Wh], dq[:, Wh:], dk[:, Wh:], dv[:, Wh:]], axis=1)
        du1_next = _mm(d_qkv, s["w_qkv"], "nt", F32, f"d_u1{l}", a2=dfl, b2=s["w_f"])
        dw_qkv = _mm(s["u1"], d_qkv, "tn", BF16, f"dw_qkv{l}")
        dw_f = _mm(s["u1"], dfl, "tn", BF16, f"dw_f{l}")
        grads["w_in"][l] = jnp.concatenate([dw_qkv, dw_f[:, :H]], axis=1)
    dh0, _, _, grads["g_mix_pre"][0] = _resnorm_bwd(dh, du1_next, saved[0]["h_in"], None, None, W["g_mix_pre"][0], "prenorm0_bwd")
    return loss, dh0[n_meta:n_meta + S], dh0[:n_meta], grads


def _all_gather(x, name):
    def body(x_ref, out_ref, send_sems, recv_sems, local_sem):
        mx, my, mc = lax.axis_index("x"), lax.axis_index("y"), lax.axis_index("c")
        me, sibling = (mx, my, mc), (mx, my, 1 - mc)
        chips = [(1 - mx, my), (mx, 1 - my), (1 - mx, 1 - my)]

        def slot(px, py, pc):
            return out_ref.at[4 * px + 2 * py + pc]

        def copy(k, block, to, src=None):
            return pltpu.make_async_remote_copy(
                src_ref=slot(*block) if src is None else src, dst_ref=slot(*block),
                send_sem=send_sems.at[k], recv_sem=recv_sems.at[k], device_id=to, device_id_type=_MESH)

        mine = pltpu.make_async_copy(x_ref, slot(*me), local_sem)
        mine.start()
        first = [copy(0, me, sibling, src=x_ref)]
        first += [copy(1 + j, me, (*chip, mc), src=x_ref) for j, chip in enumerate(chips)]
        for cp in first:
            cp.start()
        passed = [copy(4 + j, (*chip, mc), sibling) for j, chip in enumerate(chips)]
        for j, chip in enumerate(chips):
            copy(1 + j, (*chip, mc), me).wait_recv()
            passed[j].start()
        copy(0, sibling, me).wait_recv()
        for j, chip in enumerate(chips):
            copy(4 + j, (*chip, 1 - mc), me).wait_recv()
        for cp in first + passed:
            cp.wait_send()
        mine.wait()

    return pl.pallas_call(
        body, name=name, out_shape=jax.ShapeDtypeStruct((N_DEV, *x.shape), x.dtype),
        in_specs=[_ANY], out_specs=_ANY,
        scratch_shapes=[pltpu.SemaphoreType.DMA((7,)), pltpu.SemaphoreType.DMA((7,)), pltpu.SemaphoreType.DMA],
    )(x)


def _exchange(g, name):
    def body(g_ref, land_ref, send_sems, recv_sems, local_sem):
        mx, my, mc = lax.axis_index("x"), lax.axis_index("y"), lax.axis_index("c")
        me = 4 * mx + 2 * my + mc
        local = pltpu.make_async_copy(g_ref.at[me], land_ref.at[me], local_sem)
        local.start()
        sends, recvs = [], []
        for k in range(1, N_DEV):
            peer = (_flip(mx, k & 4), _flip(my, k & 2), _flip(mc, k & 1))
            p = 4 * peer[0] + 2 * peer[1] + peer[2]
            sends.append(pltpu.make_async_remote_copy(
                src_ref=g_ref.at[p], dst_ref=land_ref.at[me], send_sem=send_sems.at[k - 1], recv_sem=recv_sems.at[k - 1],
                device_id=peer, device_id_type=_MESH))
            recvs.append(pltpu.make_async_remote_copy(
                src_ref=g_ref.at[p], dst_ref=land_ref.at[p], send_sem=send_sems.at[k - 1], recv_sem=recv_sems.at[k - 1],
                device_id=peer, device_id_type=_MESH))
        for cp in sends:
            cp.start()
        for cp in recvs:
            cp.wait_recv()
        for cp in sends:
            cp.wait_send()
        local.wait()

    return pl.pallas_call(
        body, name=name, out_shape=jax.ShapeDtypeStruct(g.shape, g.dtype), in_specs=[_ANY], out_specs=_ANY,
        scratch_shapes=[pltpu.SemaphoreType.DMA((7,)), pltpu.SemaphoreType.DMA((7,)), pltpu.SemaphoreType.DMA],
    )(g)


def _exchange_cores(g, name):
    def body(g_ref, land_ref, send_sems, recv_sems):
        mx, my, mc = lax.axis_index("x"), lax.axis_index("y"), lax.axis_index("c")
        copies = [pltpu.make_async_remote_copy(
            src_ref=g_ref.at[2 * q + (1 - mc)], dst_ref=land_ref.at[q], send_sem=send_sems.at[q], recv_sem=recv_sems.at[q],
            device_id=(mx, my, 1 - mc), device_id_type=_MESH) for q in range(4)]
        for cp in copies:
            cp.start()
        for cp in copies:
            cp.wait_recv()
        for cp in copies:
            cp.wait_send()

    return pl.pallas_call(
        body, name=name, out_shape=jax.ShapeDtypeStruct((4, *g.shape[1:]), g.dtype), in_specs=[_ANY], out_specs=_ANY,
        scratch_shapes=[pltpu.SemaphoreType.DMA((4,)), pltpu.SemaphoreType.DMA((4,))],
    )(g)


def _pair_sum(g, land, name):
    _, n, R, C = g.shape
    tr = _divisor(R, (128, 64, 32, 16, 8))
    core = lax.axis_index("c").astype(jnp.int32).reshape(1)

    def body(c_ref, g_ref, land_ref, o_ref):
        o_ref[...] = (g_ref[...].astype(F32) + land_ref[...].astype(F32)).astype(o_ref.dtype)

    blk = pl.BlockSpec((None, None, tr, C), lambda q, l, r, c_ref: (q, l, r, 0))
    return pl.pallas_call(
        body, name=name, out_shape=jax.ShapeDtypeStruct(land.shape, g.dtype),
        grid_spec=pltpu.PrefetchScalarGridSpec(
            num_scalar_prefetch=1, grid=(4, n, R // tr),
            in_specs=[pl.BlockSpec((None, None, tr, C), lambda q, l, r, c_ref: (2 * q + c_ref[0], l, r, 0)), blk],
            out_specs=blk),
        compiler_params=_params(("parallel", "parallel", "parallel"), 8 * tr * C * 4),
    )(core, g, land)


def _exchange_chips(part, name):
    def body(p_ref, land_ref, send_sems, recv_sems, local_sem):
        mx, my, mc = lax.axis_index("x"), lax.axis_index("y"), lax.axis_index("c")
        me = 2 * mx + my
        local = pltpu.make_async_copy(p_ref.at[me], land_ref.at[me], local_sem)
        local.start()
        sends, recvs = [], []
        for k in range(1, 4):
            px, py = _flip(mx, k & 2), _flip(my, k & 1)
            p = 2 * px + py
            sends.append(pltpu.make_async_remote_copy(
                src_ref=p_ref.at[p], dst_ref=land_ref.at[me], send_sem=send_sems.at[k - 1], recv_sem=recv_sems.at[k - 1],
                device_id=(px, py, mc), device_id_type=_MESH))
            recvs.append(pltpu.make_async_remote_copy(
                src_ref=p_ref.at[p], dst_ref=land_ref.at[p], send_sem=send_sems.at[k - 1], recv_sem=recv_sems.at[k - 1],
                device_id=(px, py, mc), device_id_type=_MESH))
        for cp in sends:
            cp.start()
        for cp in recvs:
            cp.wait_recv()
        for cp in sends:
            cp.wait_send()
        local.wait()

    return pl.pallas_call(
        body, name=name, out_shape=jax.ShapeDtypeStruct(part.shape, part.dtype), in_specs=[_ANY], out_specs=_ANY,
        scratch_shapes=[pltpu.SemaphoreType.DMA((3,)), pltpu.SemaphoreType.DMA((3,)), pltpu.SemaphoreType.DMA],
    )(part)


def _adamw_landed(lands, w, m, v, name):
    n, R, C = w.shape
    tr = _divisor(R, (64, 32, 16, 8))
    nr = R // tr
    blk = pl.BlockSpec((None, tr, C), lambda l, r: (l, r, 0))

    def land_spec(k, S):
        return pl.BlockSpec((S, tr, C), lambda l, r: (0, jnp.where(l == k, r, jnp.where(l < k, 0, nr - 1)), 0))

    def body(*refs):
        land_refs = refs[:n]
        w_ref, m_ref, v_ref, g_ref, d_ref, mo_ref, vo_ref = refs[n:]
        for k in range(n):
            @pl.when(pl.program_id(0) == k)
            def _(k=k):
                g = land_refs[k][0].astype(F32)
                for s in range(1, lands[k].shape[0]):
                    g = g + land_refs[k][s].astype(F32)
                m_new = ADAM_B1 * m_ref[...] + (1.0 - ADAM_B1) * g
                v_new = ADAM_B2 * v_ref[...] + (1.0 - ADAM_B2) * (g * g)
                m_hat = m_new / (1.0 - ADAM_B1 ** ADAM_STEP)
                v_hat = v_new / (1.0 - ADAM_B2 ** ADAM_STEP)
                g_ref[...] = g
                d_ref[...] = -ADAM_LR * (m_hat / (jnp.sqrt(v_hat) + ADAM_EPS) + ADAM_WD * w_ref[...])
                mo_ref[...] = m_new
                vo_ref[...] = v_new

    est = 2 * tr * C * (sum(x.shape[0] * jnp.dtype(x.dtype).itemsize for x in lands) + 7 * 4) * 9 // 8
    return pl.pallas_call(
        body, name=name, grid=(n, nr),
        in_specs=[land_spec(k, x.shape[0]) for k, x in enumerate(lands)] + [blk, blk, blk],
        out_specs=[blk] * 4, out_shape=[jax.ShapeDtypeStruct((n, R, C), F32)] * 4,
        compiler_params=_params(("arbitrary", "arbitrary"), est),
    )(*lands, w, m, v)


def _pack(arrs):
    flat = jnp.concatenate([a.reshape(-1).astype(F32) for a in arrs])
    rows = -(-flat.shape[0] // (8 * LANES)) * 8
    return jnp.pad(flat, (0, rows * LANES - flat.shape[0])).reshape(rows, LANES)


def _unpack(packed, shapes):
    flat, out, at = packed.reshape(-1), [], 0
    for s in shapes:
        n = math.prod(s)
        out.append(flat[at:at + n].reshape(s))
        at += n
    return out


_BIG = ("w_in", "w_out", "w_up", "w_down")
_REPLICATED = ("g_mix_pre", "b_f", "g_sb", "g_fox", "g_mix_post", "g_ffn_pre", "conv_b", "g_ffn_post")
_ORDER = ("meta", "g_mix_pre", "w_in", "b_f", "g_sb", "g_fox", "w_out", "g_mix_post", "g_ffn_pre", "w_up", "conv_w",
          "conv_b", "w_down", "g_ffn_post")
_COLUMN_SHARDED = ("w_in", "w_up")


def kernel(x, meta, g_mix_pre, w_in, b_f, g_sb, g_fox, w_out, g_mix_post, g_ffn_pre, w_up, conv_w, conv_b, w_down, g_ffn_post, loss_target, m_meta, m_g_mix_pre, m_w_in, m_b_f, m_g_sb, m_g_fox, m_w_out, m_g_mix_post, m_g_ffn_pre, m_w_up, m_conv_w, m_conv_b, m_w_down, m_g_ffn_post, v_meta, v_g_mix_pre, v_w_in, v_b_f, v_g_sb, v_g_fox, v_w_out, v_g_mix_post, v_g_ffn_pre, v_w_up, v_conv_w, v_conv_b, v_w_down, v_g_ffn_post):
    w = dict(meta=meta, g_mix_pre=g_mix_pre, w_in=w_in, b_f=b_f, g_sb=g_sb, g_fox=g_fox, w_out=w_out, g_mix_post=g_mix_post,
             g_ffn_pre=g_ffn_pre, w_up=w_up, conv_w=conv_w, conv_b=conv_b, w_down=w_down, g_ffn_post=g_ffn_post)
    m = dict(meta=m_meta, g_mix_pre=m_g_mix_pre, w_in=m_w_in, b_f=m_b_f, g_sb=m_g_sb, g_fox=m_g_fox, w_out=m_w_out,
             g_mix_post=m_g_mix_post, g_ffn_pre=m_g_ffn_pre, w_up=m_w_up, conv_w=m_conv_w, conv_b=m_conv_b, w_down=m_w_down,
             g_ffn_post=m_g_ffn_post)
    v = dict(meta=v_meta, g_mix_pre=v_g_mix_pre, w_in=v_w_in, b_f=v_b_f, g_sb=v_g_sb, g_fox=v_g_fox, w_out=v_w_out,
             g_mix_post=v_g_mix_post, g_ffn_pre=v_g_ffn_pre, w_up=v_w_up, conv_w=v_conv_w, conv_b=v_conv_b, w_down=v_w_down,
             g_ffn_post=v_g_ffn_post)
    depth = w_in.shape[0]
    shards = {name: [w[name][l].astype(BF16) for l in range(depth)] for name in _BIG}

    def assemble(name, g):
        if name in _COLUMN_SHARDED:
            return jnp.transpose(g, (1, 0, 2)).reshape(g.shape[1], -1)
        return g.reshape(-1, g.shape[2])

    def blocks_of(name, grad):
        R, C = w[name].shape[1:]
        if name in _COLUMN_SHARDED:
            return jnp.transpose(grad.reshape(R, N_DEV, C), (1, 0, 2)).astype(BF16)
        return grad.reshape(N_DEV, R, C).astype(BF16)

    def side_by_side(name):
        return name in _COLUMN_SHARDED and w[name].shape[2] % LANES == 0

    class Hooks:
        def __init__(self):
            self.landed = {}

        def fwd_keys(self, l):
            return ([("w_in", l + 1)] if l + 1 < depth else []) + [("w_out", l), ("w_up", l), ("w_down", l)]

        def bwd_keys(self, l):
            return ([("w_in", l + 1)] if l + 1 < depth else []) + [("w_down", l), ("w_up", l), ("w_out", l)]

        def fwd_rider(self, l):
            keys = self.fwd_keys(l)
            return _GatherRider([shards[name][ll] for name, ll in keys], [side_by_side(name) for name, _ in keys])

        def fwd_done(self, l, brought, W):
            for (name, ll), g in zip(self.fwd_keys(l), brought):
                W[name][ll] = g if side_by_side(name) else assemble(name, g)

        def bwd_rider(self, l, grads):
            keys = self.bwd_keys(l)
            return _ExchangeRider([grads[name][ll] if side_by_side(name) else blocks_of(name, grads[name][ll]) for name, ll in keys],
                                  [side_by_side(name) for name, _ in keys])

        def bwd_done(self, l, brought):
            for key, land in zip(self.bwd_keys(l), brought):
                self.landed[key] = land

    small_shapes = [conv_w.shape, meta.shape]
    gs = _all_gather(_pack([conv_w, meta]), "gather_small")
    parts = [_unpack(gs[d], small_shapes) for d in range(N_DEV)]
    conv_full = jnp.concatenate([p[0] for p in parts], axis=2)
    meta_full = jnp.concatenate([p[1] for p in parts], axis=1)

    W = {name: [None] * depth for name in _BIG}
    W["w_in"][0] = assemble("w_in", _all_gather(shards["w_in"][0], "gather_w_in0"))
    W["conv_w"] = [conv_full[l] for l in range(depth)]
    for name in _REPLICATED:
        W[name] = [w[name][l] for l in range(depth)]
    hooks = Hooks()
    loss, grad_x, d_meta, grads = _local_step(x[0], loss_target[0], meta_full, W, hooks)

    first = blocks_of("w_in", grads["w_in"][0])[:, None]
    pairs = _pair_sum(first, _exchange_cores(first, "exchange_cores_w_in0"), "pair_sum_w_in0")
    hooks.landed[("w_in", 0)] = _exchange_chips(pairs, "exchange_chips_w_in0")[:, 0]
    out = {}
    for name in _BIG:
        out[name] = _adamw_landed([hooks.landed[(name, l)] for l in range(depth)], w[name], m[name], v[name], f"adamw_{name}")

    Fs, Ms = conv_w.shape[2], meta.shape[1]
    d_conv = jnp.stack(grads["conv_w"], axis=0)
    blocks = jnp.stack([_pack([d_conv[:, :, d * Fs:(d + 1) * Fs], d_meta[:, d * Ms:(d + 1) * Ms]]) for d in range(N_DEV)])
    land = _exchange(blocks, "exchange_small")
    res = _adamw_landed([land], _pack([conv_w, meta])[None], _pack([m_conv_w, m_meta])[None],
                        _pack([v_conv_w, v_meta])[None], "adamw_small")
    for i, r in enumerate(res):
        cw, mt = _unpack(r[0], small_shapes)
        out.setdefault("conv_w", [None] * 4)[i] = cw
        out.setdefault("meta", [None] * 4)[i] = mt

    rep_shapes = [()] + [w[name].shape for name in _REPLICATED]
    mine = _pack([loss] + [jnp.stack(grads[name], axis=0) for name in _REPLICATED])
    land = _all_gather(mine, "gather_replicated")
    zero = jnp.zeros((), F32)
    res = _adamw_landed([land], _pack([zero] + [w[n] for n in _REPLICATED])[None],
                        _pack([zero] + [m[n] for n in _REPLICATED])[None],
                        _pack([zero + 1.0] + [v[n] for n in _REPLICATED])[None], "adamw_replicated")
    for i, r in enumerate(res):
        vals = _unpack(r[0], rep_shapes)
        if i == 0:
            loss_total = vals[0]
        for name, val in zip(_REPLICATED, vals[1:]):
            out.setdefault(name, [None] * 4)[i] = val

    return (loss_total, grad_x[None], *[out[n][0] for n in _ORDER], *[out[n][1] for n in _ORDER],
            *[out[n][2] for n in _ORDER], *[out[n][3] for n in _ORDER])
```

```python
import functools
import math

import jax
import jax.numpy as jnp
from jax import lax
from jax.experimental import pallas as pl
from jax.experimental.pallas import tpu as pltpu

F32, BF16 = jnp.float32, jnp.bfloat16
HEAD_DIM = 128
LANES = 128
EPS = 1e-6
NEG_INF = -1e30
ATT_BLOCK = 256
N_DEV = 8
V7X_VMEM_BUDGET = 56 * 1024 * 1024

ADAM_LR, ADAM_B1, ADAM_B2, ADAM_EPS, ADAM_WD, ADAM_STEP = 0.001, 0.9, 0.999, 1e-08, 0.01, 10


def _divisor(n, cands):
    for c in cands:
        if c <= n and n % c == 0:
            return c
    raise ValueError(f"no tile for {n} among {cands}")


def _params(sem, est_bytes):
    limit = int(min(V7X_VMEM_BUDGET, max(16 * 1024 * 1024, est_bytes * 5 // 4 + (2 << 20))))
    return pltpu.CompilerParams(dimension_semantics=sem, vmem_limit_bytes=limit)


def _nbytes(shape, dtype):
    return math.prod(shape) * jnp.dtype(dtype).itemsize


_DN = {"nn": (((1,), (0,)), ((), ())), "nt": (((1,), (1,)), ((), ())), "tn": (((0,), (0,)), ((), ()))}
_ROW_TILES = (1088, 544, 272, 512, 256, 128, 64, 32, 16, 8)
_COL_TILES = (1024, 512, 256, 128)
MM_VMEM_BLOCKS = 40 * 1024 * 1024


def _mm_tiles(M, N, K, rows, out_bytes):
    best = None
    for tm in rows:
        for tn in _COL_TILES:
            if tm > M or tn > N or M % tm or N % tn:
                continue
            if 4 * (tm + tn) * K + 2 * tm * tn * out_bytes > MM_VMEM_BLOCKS:
                continue
            if best is None or tm * tn > best[0] * best[1]:
                best = (tm, tn, K)
    if best is not None:
        return best
    return _divisor(M, rows), _divisor(N, _COL_TILES[1:]), _ktile(K, 3072)


def _ktile(k, cap):
    if k <= cap:
        return k
    for t in range(cap - cap % LANES, 0, -LANES):
        if k % t == 0:
            return t
    raise ValueError(k)


def _mm(a, b, mode, out_dtype, name, a2=None, b2=None):
    if mode == "nn":
        (M, K), (_, N) = a.shape, b.shape
    elif mode == "nt":
        (M, K), (N, _) = a.shape, b.shape
    else:
        (K, M), (_, N) = a.shape, b.shape
    tm, tn, tk = _mm_tiles(M, N, K, _COL_TILES if mode == "tn" else _ROW_TILES, jnp.dtype(out_dtype).itemsize)
    nk = K // tk
    a_spec = {"nn": pl.BlockSpec((tm, tk), lambda i, j, k: (i, k)),
              "nt": pl.BlockSpec((tm, tk), lambda i, j, k: (i, k)),
              "tn": pl.BlockSpec((tk, tm), lambda i, j, k: (k, i))}[mode]
    b_spec = {"nn": pl.BlockSpec((tk, tn), lambda i, j, k: (k, j)),
              "nt": pl.BlockSpec((tn, tk), lambda i, j, k: (j, k)),
              "tn": pl.BlockSpec((tk, tn), lambda i, j, k: (k, j))}[mode]
    dn = _DN[mode]
    extra = a2 is not None
    in_specs, args = [a_spec, b_spec], [a, b]
    if extra:
        k2 = a2.shape[1]
        in_specs += [pl.BlockSpec((tm, k2), lambda i, j, k: (i, 0)), pl.BlockSpec((tn, k2), lambda i, j, k: (j, 0))]
        args += [a2, b2]

    def body(*refs):
        if extra:
            a_ref, b_ref, a2_ref, b2_ref, o_ref, acc = refs
        else:
            a_ref, b_ref, o_ref, acc = refs
        part = lax.dot_general(a_ref[...], b_ref[...], dn, preferred_element_type=F32)
        if nk == 1:
            if extra:
                part = part + lax.dot_general(a2_ref[...], b2_ref[...], _DN["nt"], preferred_element_type=F32)
            o_ref[...] = part.astype(o_ref.dtype)
            return
        kk = pl.program_id(2)

        @pl.when(kk == 0)
        def _():
            if extra:
                acc[...] = part + lax.dot_general(a2_ref[...], b2_ref[...], _DN["nt"], preferred_element_type=F32)
            else:
                acc[...] = part

        @pl.when(kk > 0)
        def _():
            acc[...] += part

        @pl.when(kk == nk - 1)
        def _():
            o_ref[...] = acc[...].astype(o_ref.dtype)

    est = 2 * (tm * tk + tk * tn) * 2 + 2 * _nbytes((tm, tn), out_dtype) + (tm * tn * 4 if nk > 1 else 0)
    return pl.pallas_call(
        body, name=name, grid=(M // tm, N // tn, nk), in_specs=in_specs,
        out_specs=pl.BlockSpec((tm, tn), lambda i, j, k: (i, j)),
        out_shape=jax.ShapeDtypeStruct((M, N), out_dtype),
        scratch_shapes=[pltpu.VMEM((tm, tn) if nk > 1 else (8, LANES), F32)],
        compiler_params=_params(("parallel", "parallel", "arbitrary"), est),
    )(*args)


def _rms(x, g):
    r = lax.rsqrt(jnp.mean(x * x, axis=-1, keepdims=True) + EPS)
    return x * r * g


def _rms_bwd(x, g, dout):
    r = lax.rsqrt(jnp.mean(x * x, axis=-1, keepdims=True) + EPS)
    xhat = x * r
    dxh = dout * g
    dx = r * (dxh - xhat * jnp.mean(dxh * xhat, axis=-1, keepdims=True))
    return dx, dout * xhat


def _row_tile(L):
    return _divisor(L, (272, 256, 128, 64, 32, 16))


def _resnorm_fwd(h, y, g_post, g_pre, name):
    L, D = h.shape
    bt = _row_tile(L)
    has_y, has_pre = y is not None, g_pre is not None
    row = pl.BlockSpec((bt, D), lambda i: (i, 0))
    vec = pl.BlockSpec((1, D), lambda i: (0, 0))
    args, in_specs = [h], [row]
    if has_y:
        args += [y, g_post.reshape(1, D)]
        in_specs += [row, vec]
    if has_pre:
        args += [g_pre.reshape(1, D)]
        in_specs += [vec]
    out_shape, out_specs = [], []
    if has_y:
        out_shape.append(jax.ShapeDtypeStruct((L, D), F32))
        out_specs.append(row)
    if has_pre:
        out_shape.append(jax.ShapeDtypeStruct((L, D), BF16))
        out_specs.append(row)

    def body(*refs):
        refs = list(refs)
        h_ref = refs.pop(0)
        hn = h_ref[...]
        if has_y:
            y_ref, gp_ref = refs.pop(0), refs.pop(0)
            hn = hn + _rms(y_ref[...], gp_ref[...])
        if has_pre:
            g_ref = refs.pop(0)
        if has_y:
            refs.pop(0)[...] = hn
        if has_pre:
            refs.pop(0)[...] = _rms(hn, g_ref[...]).astype(BF16)

    outs = pl.pallas_call(
        body, name=name, grid=(L // bt,), in_specs=in_specs, out_specs=out_specs, out_shape=out_shape,
        compiler_params=_params(("parallel",), 10 * bt * D * 4),
    )(*args)
    outs = list(outs)
    h_new = outs.pop(0) if has_y else h
    u = outs.pop(0) if has_pre else None
    return h_new, u


def _resnorm_bwd(dh_direct, du, h_new, y, g_post, g_pre, name):
    L, D = dh_direct.shape
    bt = _row_tile(L)
    has_y, has_pre = y is not None, du is not None
    row = pl.BlockSpec((bt, D), lambda i: (i, 0))
    vec = pl.BlockSpec((1, D), lambda i: (0, 0))
    acc = pl.BlockSpec((8, D), lambda i: (0, 0))
    args, in_specs = [dh_direct], [row]
    if has_pre:
        args += [du, h_new, g_pre.reshape(1, D)]
        in_specs += [row, row, vec]
    if has_y:
        args += [y, g_post.reshape(1, D)]
        in_specs += [row, vec]
    out_shape, out_specs = [], []
    if has_pre:
        out_shape += [jax.ShapeDtypeStruct((L, D), F32), jax.ShapeDtypeStruct((8, D), F32)]
        out_specs += [row, acc]
    if has_y:
        out_shape += [jax.ShapeDtypeStruct((L, D), BF16), jax.ShapeDtypeStruct((8, D), F32)]
        out_specs += [row, acc]

    def colsum8(v):
        return jnp.sum(v.reshape(bt // 8, 8, D), axis=0)

    def body(*refs):
        refs = list(refs)
        first = pl.program_id(0) == 0
        dh = refs.pop(0)[...]
        if has_pre:
            du_ref, hn_ref, g_ref = refs.pop(0), refs.pop(0), refs.pop(0)
        if has_y:
            y_ref, gp_ref = refs.pop(0), refs.pop(0)
        if has_pre:
            dh_ref, dgpre_ref = refs.pop(0), refs.pop(0)
            dx, dgp = _rms_bwd(hn_ref[...], g_ref[...], du_ref[...].astype(F32))
            dh = dh + dx
            dh_ref[...] = dh

            @pl.when(first)
            def _():
                dgpre_ref[...] = jnp.zeros_like(dgpre_ref)
            dgpre_ref[...] += colsum8(dgp)
        if has_y:
            dy_ref, dgpost_ref = refs.pop(0), refs.pop(0)
            dy, dgq = _rms_bwd(y_ref[...], gp_ref[...], dh)
            dy_ref[...] = dy.astype(BF16)

            @pl.when(first)
            def _():
                dgpost_ref[...] = jnp.zeros_like(dgpost_ref)
            dgpost_ref[...] += colsum8(dgq)

    outs = list(pl.pallas_call(
        body, name=name, grid=(L // bt,), in_specs=in_specs, out_specs=out_specs, out_shape=out_shape,
        compiler_params=_params(("arbitrary",), 14 * bt * D * 4),
    )(*args))
    dh, dg_pre, dy, dg_post = dh_direct, None, None, None
    if has_pre:
        dh, dg_pre = outs.pop(0), outs.pop(0).sum(0)
    if has_y:
        dy, dg_post = outs.pop(0), outs.pop(0).sum(0)
    return dh, dy, dg_post, dg_pre


def _loss(h, tgt, n_meta, seq, name):
    L, D = h.shape
    bt = _row_tile(L)
    row = pl.BlockSpec((bt, D), lambda i: (i, 0))

    def body(h_ref, t_ref, dy_ref, loss_ref):
        i = pl.program_id(0)
        r = i * bt + lax.broadcasted_iota(jnp.int32, (bt, 1), 0)
        valid = (r >= n_meta) & (r < n_meta + seq)
        e = jnp.where(valid, h_ref[...] - t_ref[...], 0.0)
        dy_ref[...] = e * (1.0 / D)

        @pl.when(i == 0)
        def _():
            loss_ref[...] = jnp.zeros_like(loss_ref)
        loss_ref[...] += 0.5 * jnp.sum(jnp.sum(e * e, axis=-1, keepdims=True) * (1.0 / D))

    dy, loss = pl.pallas_call(
        body, name=name, grid=(L // bt,), in_specs=[row, row],
        out_specs=[row, pl.BlockSpec((8, LANES), lambda i: (0, 0))],
        out_shape=[jax.ShapeDtypeStruct((L, D), F32), jax.ShapeDtypeStruct((8, LANES), F32)],
        compiler_params=_params(("arbitrary",), 8 * bt * D * 4),
    )(h, tgt)
    return loss[0, 0], dy


def _split3(x):
    x1 = x.astype(BF16)
    r1 = x - x1.astype(F32)
    x2 = r1.astype(BF16)
    x3 = (r1 - x2.astype(F32)).astype(BF16)
    return x1, x2, x3


def _tri_dot3(tri, x):
    x1, x2, x3 = _split3(x)
    d = functools.partial(jnp.dot, preferred_element_type=F32)
    return d(tri, x1) + d(tri, x2) + d(tri, x3)


def _gate_fwd(fl, b_pad, name):
    L = fl.shape[0]
    bt = 128
    blk = pl.BlockSpec((bt, LANES), lambda i: (i, 0))

    def body(fl_ref, b_ref, c_ref, carry):
        @pl.when(pl.program_id(0) == 0)
        def _():
            carry[...] = jnp.zeros_like(carry)
        x = fl_ref[...] + b_ref[...]
        lf = jnp.minimum(x, 0.0) - jnp.log(1.0 + jnp.exp(-jnp.abs(x)))
        r = lax.broadcasted_iota(jnp.int32, (bt, bt), 0)
        s = lax.broadcasted_iota(jnp.int32, (bt, bt), 1)
        tri = (s <= r).astype(BF16)
        c = _tri_dot3(tri, lf) + carry[...]
        c_ref[...] = c
        carry[...] = c[bt - 1:bt, :]

    return pl.pallas_call(
        body, name=name, grid=(L // bt,), in_specs=[blk, pl.BlockSpec((1, LANES), lambda i: (0, 0))],
        out_specs=blk, out_shape=jax.ShapeDtypeStruct((L, LANES), F32),
        scratch_shapes=[pltpu.VMEM((1, LANES), F32)],
        compiler_params=_params(("arbitrary",), 1 << 20),
    )(fl, b_pad)


def _gate_bwd(dc, fl, b_pad, name):
    L = fl.shape[0]
    bt = 128
    n = L // bt
    blk = pl.BlockSpec((bt, LANES), lambda i: (n - 1 - i, 0))

    def body(dc_ref, fl_ref, b_ref, dfl_ref, db_ref, carry):
        @pl.when(pl.program_id(0) == 0)
        def _():
            carry[...] = jnp.zeros_like(carry)
            db_ref[...] = jnp.zeros_like(db_ref)
        r = lax.broadcasted_iota(jnp.int32, (bt, bt), 0)
        s = lax.broadcasted_iota(jnp.int32, (bt, bt), 1)
        tri = (s >= r).astype(BF16)
        dlf = _tri_dot3(tri, dc_ref[...]) + carry[...]
        carry[...] = dlf[0:1, :]
        x = fl_ref[...] + b_ref[...]
        dfl = dlf / (1.0 + jnp.exp(x))
        dfl_ref[...] = dfl.astype(BF16)
        db_ref[...] += jnp.sum(dfl.reshape(bt // 8, 8, LANES), axis=0)

    dfl, db = pl.pallas_call(
        body, name=name, grid=(n,), in_specs=[blk, blk, pl.BlockSpec((1, LANES), lambda i: (0, 0))],
        out_specs=[blk, pl.BlockSpec((8, LANES), lambda i: (0, 0))],
        out_shape=[jax.ShapeDtypeStruct((L, LANES), BF16), jax.ShapeDtypeStruct((8, LANES), F32)],
        scratch_shapes=[pltpu.VMEM((1, LANES), F32)],
        compiler_params=_params(("arbitrary",), 1 << 20),
    )(dc, fl, b_pad)
    return dfl, db.sum(0)


_MESH = pl.DeviceIdType.MESH
_ANY = pl.BlockSpec(memory_space=pl.ANY)


def _flip(v, bit):
    return 1 - v if bit else v


class _GatherRider:
    def __init__(self, blocks, side_by_side):
        self.blocks = list(blocks)
        self.side_by_side = list(side_by_side)
        self.n = len(self.blocks)
        self.in_specs = [_ANY] * self.n
        self.out_specs = [_ANY] * self.n
        self.out_shape = [jax.ShapeDtypeStruct((b.shape[0], N_DEV * b.shape[1]) if cols else (N_DEV, *b.shape), b.dtype)
                          for b, cols in zip(self.blocks, self.side_by_side)]
        self.scratch = [pltpu.SemaphoreType.DMA((7 * self.n,)), pltpu.SemaphoreType.DMA((7 * self.n,)),
                        pltpu.SemaphoreType.DMA((self.n,))]

    def _copies(self, a, x_ref, out_ref, send_sems, recv_sems):
        mx, my, mc = lax.axis_index("x"), lax.axis_index("y"), lax.axis_index("c")
        me, sibling = (mx, my, mc), (mx, my, 1 - mc)
        chips = [(1 - mx, my), (mx, 1 - my), (1 - mx, 1 - my)]
        width = self.blocks[a].shape[1]

        def slot(px, py, pc):
            d = 4 * px + 2 * py + pc
            if self.side_by_side[a]:
                return out_ref.at[:, pl.ds(pl.multiple_of(d * width, LANES), width)]
            return out_ref.at[d]

        def copy(k, block, to, src=None):
            return pltpu.make_async_remote_copy(
                src_ref=slot(*block) if src is None else src, dst_ref=slot(*block),
                send_sem=send_sems.at[7 * a + k], recv_sem=recv_sems.at[7 * a + k], device_id=to, device_id_type=_MESH)

        first = [copy(0, me, sibling, src=x_ref)] + [copy(1 + j, me, (*chip, mc), src=x_ref) for j, chip in enumerate(chips)]
        landed = [copy(1 + j, (*chip, mc), me) for j, chip in enumerate(chips)]
        passed = [copy(4 + j, (*chip, mc), sibling) for j, chip in enumerate(chips)]
        last = [copy(0, sibling, me)] + [copy(4 + j, (*chip, 1 - mc), me) for j, chip in enumerate(chips)]
        return slot(*me), first, landed, passed, last

    def start(self, ins, outs, send_sems, recv_sems, local_sems):
        for a in range(self.n):
            mine, first, _, _, _ = self._copies(a, ins[a], outs[a], send_sems, recv_sems)
            pltpu.make_async_copy(ins[a], mine, local_sems.at[a]).start()
            for cp in first:
                cp.start()

    def middle(self, ins, outs, send_sems, recv_sems, local_sems):
        for a in range(self.n):
            _, _, landed, passed, _ = self._copies(a, ins[a], outs[a], send_sems, recv_sems)
            for arrived, onward in zip(landed, passed):
                arrived.wait_recv()
                onward.start()

    def finish(self, ins, outs, send_sems, recv_sems, local_sems):
        for a in range(self.n):
            mine, first, _, passed, last = self._copies(a, ins[a], outs[a], send_sems, recv_sems)
            for cp in last:
                cp.wait_recv()
            for cp in first + passed:
                cp.wait_send()
            pltpu.make_async_copy(ins[a], mine, local_sems.at[a]).wait()


class _ExchangeRider:
    def __init__(self, arrays, side_by_side):
        self.blocks = list(arrays)
        self.side_by_side = list(side_by_side)
        self.n = len(self.blocks)
        self.in_specs = [_ANY] * self.n
        self.out_specs = [_ANY] * self.n
        self.out_shape = [jax.ShapeDtypeStruct((N_DEV, b.shape[0], b.shape[1] // N_DEV) if cols else b.shape, b.dtype)
                          for b, cols in zip(self.blocks, self.side_by_side)]
        self.scratch = [pltpu.SemaphoreType.DMA((7 * self.n,)), pltpu.SemaphoreType.DMA((7 * self.n,)),
                        pltpu.SemaphoreType.DMA((self.n,))]

    def _copies(self, a, g_ref, land_ref, send_sems, recv_sems):
        mx, my, mc = lax.axis_index("x"), lax.axis_index("y"), lax.axis_index("c")
        me = 4 * mx + 2 * my + mc
        width = self.out_shape[a].shape[2]

        def block(j):
            if self.side_by_side[a]:
                return g_ref.at[:, pl.ds(pl.multiple_of(j * width, LANES), width)]
            return g_ref.at[j]

        sends, recvs = [], []
        for k in range(1, N_DEV):
            peer = (_flip(mx, k & 4), _flip(my, k & 2), _flip(mc, k & 1))
            p = 4 * peer[0] + 2 * peer[1] + peer[2]
            sems = dict(send_sem=send_sems.at[7 * a + k - 1], recv_sem=recv_sems.at[7 * a + k - 1], device_id=peer, device_id_type=_MESH)
            sends.append(pltpu.make_async_remote_copy(src_ref=block(p), dst_ref=land_ref.at[me], **sems))
            recvs.append(pltpu.make_async_remote_copy(src_ref=block(p), dst_ref=land_ref.at[p], **sems))
        return block(me), land_ref.at[me], sends, recvs

    def start(self, ins, outs, send_sems, recv_sems, local_sems):
        for a in range(self.n):
            src, dst, sends, _ = self._copies(a, ins[a], outs[a], send_sems, recv_sems)
            pltpu.make_async_copy(src, dst, local_sems.at[a]).start()
            for cp in sends:
                cp.start()

    def middle(self, ins, outs, send_sems, recv_sems, local_sems):
        pass

    def finish(self, ins, outs, send_sems, recv_sems, local_sems):
        for a in range(self.n):
            src, dst, sends, recvs = self._copies(a, ins[a], outs[a], send_sems, recv_sems)
            for cp in recvs:
                cp.wait_recv()
            for cp in sends:
                cp.wait_send()
            pltpu.make_async_copy(src, dst, local_sems.at[a]).wait()


class _NoRider:
    blocks, in_specs, out_specs, out_shape, scratch = [], [], [], [], []


_NO_RIDER = _NoRider()


def _ride(rider, refs, n_in, n_out, first, middle, last, work):
    if rider is None:
        return work(*refs)
    r = rider.n
    own = refs[:n_in] + refs[n_in + r:n_in + r + n_out] + refs[n_in + r + n_out + r:len(refs) - 3]
    args = (refs[n_in:n_in + r], refs[n_in + r + n_out:n_in + r + n_out + r], *refs[len(refs) - 3:])
    pl.when(first)(lambda: rider.start(*args))
    pl.when(middle)(lambda: rider.middle(*args))
    work(*own)
    pl.when(last)(lambda: rider.finish(*args))


def _split2(x):
    hi = x.astype(BF16)
    return hi, (x - hi.astype(F32)).astype(BF16)


def _dot_hi_lo(x, w2):
    hi, lo = _split2(x)
    return jnp.dot(jnp.concatenate([hi, lo], axis=1), w2, preferred_element_type=F32)


def _suffix_sums(x, tri2, exact):
    if exact:
        return _dot_hi_lo(x, tri2)
    return jnp.dot(x.astype(BF16), tri2[:x.shape[1]], preferred_element_type=F32)


def _dot_nt(a, b):
    return lax.dot_general(a, b, _DN["nt"], preferred_element_type=F32)


def _suffix_matrix(bk):
    j = lax.broadcasted_iota(jnp.int32, (2 * bk, bk), 0) % bk
    s = lax.broadcasted_iota(jnp.int32, (2 * bk, bk), 1)
    return (j >= s).astype(BF16)


LOG2E = 1.4426950408889634


def _sb_window(q, k, t_pos, ks, bk, scale2, carry_c, tri, masked):
    width = k.shape[0]
    z = _dot_nt(q, k) * scale2
    sp = jnp.maximum(z, 0.0) + jnp.log2(1.0 + jnp.exp2(-jnp.abs(z)))
    if masked:
        mask = ks + lax.broadcasted_iota(jnp.int32, (1, width), 1) < t_pos
        lkm = jnp.where(mask, -sp, 0.0)
    else:
        mask, lkm = None, -sp
    sums, run = _window_suffix_sums(lkm, bk, tri, carry_c, False)
    a = jnp.exp2(z + sums)
    if masked:
        a = jnp.where(mask, a, 0.0)
    return z, sp, mask, a, run


def _window_suffix_sums(x, bk, tri, carry, exact):
    nb = x.shape[1] // bk
    parts, run = [None] * nb, carry
    for b in reversed(range(nb)):
        cs = _suffix_sums(x[:, b * bk:(b + 1) * bk], tri, exact)
        parts[b] = run + cs
        run = run + cs[:, 0:1]
    return (parts[0] if nb == 1 else jnp.concatenate(parts, axis=1)), run


KEY_WINDOW = 4
KEY_WINDOW_WIDE = 8


def _key_tiles(i, bk, step, carry):
    r = i % KEY_WINDOW

    def first(w):
        return lambda c: step(pl.multiple_of((i - w) * bk, bk), (w + 1) * bk, c, True)

    carry = lax.switch(r, [first(w) for w in range(KEY_WINDOW)], carry)
    top = i - r
    odd = (top // KEY_WINDOW) % (KEY_WINDOW_WIDE // KEY_WINDOW)
    carry = lax.fori_loop(
        0, odd, lambda p, c: step(pl.multiple_of((top - KEY_WINDOW) * bk, bk), KEY_WINDOW * bk, c, False), carry)
    top = top - odd * KEY_WINDOW
    return lax.fori_loop(
        0, top // KEY_WINDOW_WIDE,
        lambda p, c: step(pl.multiple_of((top - KEY_WINDOW_WIDE * (p + 1)) * bk, bk), KEY_WINDOW_WIDE * bk, c, False), carry)


def _attn_specs(H, L, bq):
    W3 = 3 * H

    def col(role):
        return lambda h, i: (h // H) * W3 + role * H + h % H

    q_spec = pl.BlockSpec((bq, HEAD_DIM), lambda h, i: (i, col(0)(h, i)))
    k_spec = pl.BlockSpec((L, HEAD_DIM), lambda h, i: (0, col(1)(h, i)))
    v_spec = pl.BlockSpec((L, HEAD_DIM), lambda h, i: (0, col(2)(h, i)))
    crow_spec = pl.BlockSpec((None, 1, L), lambda h, i: (jnp.maximum(h - H, 0), 0, 0))
    ccol_spec = pl.BlockSpec((None, bq, 1), lambda h, i: (jnp.maximum(h - H, 0), i, 0))
    g_spec = pl.BlockSpec((None, 1, HEAD_DIM), lambda h, i: (h, 0, 0))
    tile = pl.BlockSpec((bq, HEAD_DIM), lambda h, i: (i, h))
    stat = pl.BlockSpec((None, bq, 1), lambda h, i: (h, i, 0))
    return q_spec, k_spec, v_spec, crow_spec, ccol_spec, g_spec, tile, stat


def _attn_fwd(qkv, crow, ccol, g_heads, H, name, rider=None):
    L = qkv.shape[0]
    bq = bk = min(ATT_BLOCK, L)
    nq = L // bq
    scale = HEAD_DIM ** -0.5
    scale2 = scale * LOG2E
    q_spec, k_spec, v_spec, crow_spec, ccol_spec, g_spec, tile, stat = _attn_specs(H, L, bq)

    def work(q_ref, k_ref, v_ref, crow_ref, ccol_ref, g_ref, o_ref, on_ref, lse_ref):
        h, i = pl.program_id(0), pl.program_id(1)
        q = q_ref[...]
        t_pos = i * bq + lax.broadcasted_iota(jnp.int32, (bq, 1), 0)

        def finish(o):
            o_ref[...] = o
            on_ref[...] = _rms(o, g_ref[...]).astype(BF16)

        @pl.when(h < H)
        def _stick_breaking():
            tri = _suffix_matrix(bk)

            def step(ks, rows, carry, masked):
                c, acc = carry
                k = k_ref[pl.ds(ks, rows), :]
                v = v_ref[pl.ds(ks, rows), :]
                _, _, _, a, c = _sb_window(q, k, t_pos, ks, bk, scale2, c, tri, masked)
                acc = acc + jnp.dot(a.astype(BF16), v, preferred_element_type=F32)
                return c, acc

            _, acc = _key_tiles(i, bk, step, (jnp.zeros((bq, 1), F32), jnp.zeros((bq, HEAD_DIM), F32)))
            finish(acc)
            lse_ref[...] = jnp.zeros_like(lse_ref)

        @pl.when(h >= H)
        def _forgetting():
            cq = ccol_ref[...] * LOG2E

            def step(ks, rows, carry, masked):
                m, l, acc = carry
                k = k_ref[pl.ds(ks, rows), :]
                v = v_ref[pl.ds(ks, rows), :]
                s = _dot_nt(q, k) * scale2 + (cq - crow_ref[:, pl.ds(ks, rows)] * LOG2E)
                if masked:
                    s = jnp.where(ks + lax.broadcasted_iota(jnp.int32, (1, rows), 1) <= t_pos, s, NEG_INF)
                m_new = jnp.maximum(m, jnp.max(s, axis=-1, keepdims=True))
                alpha = jnp.exp2(m - m_new)
                p = jnp.exp2(s - m_new)
                l = alpha * l + jnp.sum(p, axis=-1, keepdims=True)
                acc = alpha * acc + _dot_hi_lo(p, jnp.concatenate([v, v], axis=0))
                return m_new, l, acc

            init = (jnp.full((bq, 1), NEG_INF, F32), jnp.zeros((bq, 1), F32), jnp.zeros((bq, HEAD_DIM), F32))
            m, l, acc = _key_tiles(i, bk, step, init)
            finish(acc / l)
            lse_ref[...] = m + jnp.log2(l)

    def body(*refs):
        h, i = pl.program_id(0), pl.program_id(1)
        _ride(rider, refs, 6, 3, (h == 0) & (i == 0), (h == 2 * H - 2) & (i == 0), (h == 2 * H - 1) & (i == nq - 1), work)

    W2 = 2 * H * HEAD_DIM
    est = 4 * L * HEAD_DIM * 2 + 14 * bq * KEY_WINDOW_WIDE * bk * 4 + (4 << 20)
    extra = rider or _NO_RIDER
    outs = pl.pallas_call(
        body, name=name, grid=(2 * H, nq),
        in_specs=[q_spec, k_spec, v_spec, crow_spec, ccol_spec, g_spec] + extra.in_specs,
        out_specs=[tile, tile, stat] + extra.out_specs,
        out_shape=[jax.ShapeDtypeStruct((L, W2), F32), jax.ShapeDtypeStruct((L, W2), BF16),
                   jax.ShapeDtypeStruct((2 * H, L, 1), F32)] + extra.out_shape,
        scratch_shapes=extra.scratch,
        compiler_params=_params(("arbitrary", "arbitrary"), est),
    )(qkv, qkv, qkv, crow, ccol, g_heads, *extra.blocks)
    return outs[0], outs[1], outs[2], list(outs[3:])


def _attn_bwd(qkv, crow, ccol, g_heads, o, lse, d_on, H, name, rider=None):
    L = qkv.shape[0]
    bq = bk = min(ATT_BLOCK, L)
    nq = L // bq
    scale = HEAD_DIM ** -0.5
    scale2 = scale * LOG2E
    q_spec, k_spec, v_spec, crow_spec, ccol_spec, g_spec, tile, stat = _attn_specs(H, L, bq)
    full = pl.BlockSpec((L, HEAD_DIM), lambda h, i: (0, h))
    dg_spec = pl.BlockSpec((None, 1, HEAD_DIM), lambda h, i: (h, 0, 0))
    dc_spec = pl.BlockSpec((None, 1, L), lambda h, i: (h, 0, 0))

    def work(q_ref, k_ref, v_ref, crow_ref, ccol_ref, g_ref, o_ref, lse_ref, don_ref,
             dq_ref, dk_ref, dv_ref, dg_ref, dc_ref, dkt_acc, dvt_acc):
        h, i = pl.program_id(0), pl.program_id(1)

        @pl.when(i == 0)
        def _():
            dkt_acc[...] = jnp.zeros_like(dkt_acc)
            dvt_acc[...] = jnp.zeros_like(dvt_acc)
            dg_ref[...] = jnp.zeros_like(dg_ref)
            dc_ref[...] = jnp.zeros_like(dc_ref)

        q = q_ref[...]
        t_pos = i * bq + lax.broadcasted_iota(jnp.int32, (bq, 1), 0)
        o_t = o_ref[...]
        d_o, dg = _rms_bwd(o_t, g_ref[...], don_ref[...])
        dg_ref[...] += jnp.sum(dg, axis=0, keepdims=True)
        d_ob = d_o.astype(BF16)
        dsum = jnp.sum(d_ob.astype(F32) * o_t, axis=-1, keepdims=True)
        q_t = q.astype(F32).T.astype(BF16)
        d_obt = d_o.T.astype(BF16)

        @pl.when(h < H)
        def _stick_breaking():
            tri = _suffix_matrix(bk)

            def step(ks, rows, carry, masked):
                c, gs, dq = carry
                k = k_ref[pl.ds(ks, rows), :]
                v = v_ref[pl.ds(ks, rows), :]
                z, sp, mask, a, c = _sb_window(q, k, t_pos, ks, bk, scale2, c, tri, masked)
                a_b = a.astype(BF16)
                g_w = a_b.astype(F32) * _dot_nt(d_ob, v)
                later, gs = _window_suffix_sums(g_w, bk, tri, gs, True)
                prefix = dsum - (later - g_w)
                dz = (g_w - jnp.exp2(z - sp) * prefix) * scale
                if masked:
                    dz = jnp.where(mask, dz, 0.0)
                dzb = dz.astype(BF16)
                dq = dq + jnp.dot(dzb, k, preferred_element_type=F32)
                dkt_acc[:, pl.ds(ks, rows)] += jnp.dot(q_t, dzb, preferred_element_type=F32)
                dvt_acc[:, pl.ds(ks, rows)] += jnp.dot(d_obt, a_b, preferred_element_type=F32)
                return c, gs, dq

            z1 = jnp.zeros((bq, 1), F32)
            _, _, dq = _key_tiles(i, bk, step, (z1, z1, jnp.zeros((bq, HEAD_DIM), F32)))
            dq_ref[...] = dq.astype(BF16)

        @pl.when(h >= H)
        def _forgetting():
            cq = ccol_ref[...] * LOG2E - lse_ref[...]

            def step(ks, rows, dq, masked):
                k = k_ref[pl.ds(ks, rows), :]
                v = v_ref[pl.ds(ks, rows), :]
                p = jnp.exp2(_dot_nt(q, k) * scale2 + (cq - crow_ref[:, pl.ds(ks, rows)] * LOG2E))
                if masked:
                    p = jnp.where(ks + lax.broadcasted_iota(jnp.int32, (1, rows), 1) <= t_pos, p, 0.0)
                ds = p * (_dot_nt(d_ob, v) - dsum)
                dsb = (ds * scale).astype(BF16)
                dq = dq + jnp.dot(dsb, k, preferred_element_type=F32)
                dkt_acc[:, pl.ds(ks, rows)] += jnp.dot(q_t, dsb, preferred_element_type=F32)
                dvt_acc[:, pl.ds(ks, rows)] += jnp.dot(d_obt, p.astype(BF16), preferred_element_type=F32)
                dc_ref[:, pl.ds(ks, rows)] += -jnp.sum(ds, axis=0, keepdims=True)
                return dq

            dq = _key_tiles(i, bk, step, jnp.zeros((bq, HEAD_DIM), F32))
            dq_ref[...] = dq.astype(BF16)

        @pl.when(i == nq - 1)
        def _():
            dk_ref[...] = dkt_acc[...].T.astype(BF16)
            dv_ref[...] = dvt_acc[...].T.astype(BF16)

    W2 = 2 * H * HEAD_DIM
    est = 4 * L * HEAD_DIM * 2 + 4 * L * HEAD_DIM * 2 + 2 * L * HEAD_DIM * 4 + 18 * bq * KEY_WINDOW_WIDE * bk * 4 + (4 << 20)
    def body(*refs):
        h, i = pl.program_id(0), pl.program_id(1)
        _ride(rider, refs, 9, 5, (h == 0) & (i == 0), (h == H) & (i == 0), (h == 2 * H - 1) & (i == nq - 1), work)

    extra = rider or _NO_RIDER
    dq, dk, dv, dg, dc, *brought = pl.pallas_call(
        body, name=name, grid=(2 * H, nq),
        in_specs=[q_spec, k_spec, v_spec, crow_spec, ccol_spec, g_spec, tile, stat, tile] + extra.in_specs,
        out_specs=[tile, full, full, dg_spec, dc_spec] + extra.out_specs,
        out_shape=[jax.ShapeDtypeStruct((L, W2), BF16)] * 3
        + [jax.ShapeDtypeStruct((2 * H, 1, HEAD_DIM), F32), jax.ShapeDtypeStruct((2 * H, 1, L), F32)] + extra.out_shape,
        scratch_shapes=[pltpu.VMEM((HEAD_DIM, L), F32), pltpu.VMEM((HEAD_DIM, L), F32)] + extra.scratch,
        compiler_params=_params(("arbitrary", "arbitrary"), est),
    )(qkv, qkv, qkv, crow, ccol, g_heads, o, lse, d_on, *extra.blocks)
    return dq, dk, dv, dg[:, 0, :], dc, brought


HALO = 16


def _conv_tiles(L, F, rows):
    return _divisor(L, tuple(t for t in _ROW_TILES if t <= rows and t % HALO == 0)), _divisor(F, (512, 256, 128))


SUBLANES = 8


def _shift_down(x, before, bt):
    ext = jnp.concatenate([before, x], axis=0)
    return pltpu.roll(ext, 1, 0)[SUBLANES:], pltpu.roll(ext, 2, 0)[SUBLANES:]


def _shift_up(x, after, bt):
    ext = jnp.concatenate([x, after], axis=0)
    return pltpu.roll(ext, bt + SUBLANES - 1, 0)[:bt], pltpu.roll(ext, bt + SUBLANES - 2, 0)[:bt]


def _conv_rows(p_ref, halo_ref, w_ref, b_ref, first, bt):
    p = p_ref[...].astype(F32)
    before = jnp.where(first, 0.0, halo_ref[...].astype(F32)[HALO - SUBLANES:HALO, :])
    p1, p2 = _shift_down(p, before, bt)
    w = w_ref[...]
    a = w[0:1, :] * p2 + w[1:2, :] * p1 + w[2:3, :] * p + b_ref[...]
    return a, p, p1, p2


def _sigmoid(x):
    return 0.5 + 0.5 * jnp.tanh(0.5 * x)


def _conv_in_specs(L, F, bt, bc, order):
    nf = F // bc
    r = bt // HALO
    ix = (lambda a, b: (a, b)) if order == "ij" else (lambda a, b: (b, a))

    def mk(shape, fn):
        return pl.BlockSpec(shape, lambda a, b: fn(*ix(a, b)))

    return [
        mk((bt, bc), lambda i, j: (i, j)), mk((HALO, bc), lambda i, j: (jnp.maximum(i * r - 1, 0), j)),
        mk((bt, bc), lambda i, j: (i, nf + j)), mk((HALO, bc), lambda i, j: (jnp.maximum(i * r - 1, 0), nf + j)),
        mk((3, bc), lambda i, j: (0, j)), mk((3, bc), lambda i, j: (0, nf + j)),
        mk((1, bc), lambda i, j: (0, j)), mk((1, bc), lambda i, j: (0, nf + j)),
    ]


def _convgate_fwd(p, conv_w, conv_b, name):
    L, F2 = p.shape
    F = F2 // 2
    bt, bc = _conv_tiles(L, F, 544)

    def body(pg, hg, pu, hu, wg, wu, bg, bu, act_ref):
        first = pl.program_id(0) == 0
        ag = _conv_rows(pg, hg, wg, bg, first, bt)[0]
        au = _conv_rows(pu, hu, wu, bu, first, bt)[0]
        act_ref[...] = (ag * _sigmoid(ag) * au).astype(BF16)

    return pl.pallas_call(
        body, name=name, grid=(L // bt, F // bc), in_specs=_conv_in_specs(L, F, bt, bc, "ij"),
        out_specs=pl.BlockSpec((bt, bc), lambda i, j: (i, j)), out_shape=jax.ShapeDtypeStruct((L, F), BF16),
        compiler_params=_params(("parallel", "parallel"), 24 * bt * bc * 4),
    )(p, p, p, p, conv_w, conv_w, conv_b.reshape(1, F2), conv_b.reshape(1, F2))


def _convgate_bwd(p, conv_w, conv_b, d_act, name):
    L, F2 = p.shape
    F = F2 // 2
    bt, bc = _conv_tiles(L, F, 544)

    def body(pg, hg, pu, hu, wg, wu, bg, bu, dact_ref, da_ref, dwb_ref):
        first = pl.program_id(1) == 0
        ag, xg, xg1, xg2 = _conv_rows(pg, hg, wg, bg, first, bt)
        au, xu, xu1, xu2 = _conv_rows(pu, hu, wu, bu, first, bt)
        d_act = dact_ref[...].astype(F32)
        sg = _sigmoid(ag)
        dag = d_act * au * sg * (1.0 + ag * (1.0 - sg))
        dau = d_act * ag * sg
        da_ref[0] = dag.astype(BF16)
        da_ref[1] = dau.astype(BF16)

        @pl.when(first)
        def _():
            dwb_ref[...] = jnp.zeros_like(dwb_ref)
        cs = lambda v: jnp.sum(v, axis=0, keepdims=True)
        dwb_ref[0] += jnp.concatenate([cs(dag * xg2), cs(dag * xg1), cs(dag * xg), cs(dag)], axis=0)
        dwb_ref[1] += jnp.concatenate([cs(dau * xu2), cs(dau * xu1), cs(dau * xu), cs(dau)], axis=0)

    da, dwb = pl.pallas_call(
        body, name=name, grid=(F // bc, L // bt),
        in_specs=_conv_in_specs(L, F, bt, bc, "ji") + [pl.BlockSpec((bt, bc), lambda j, i: (i, j))],
        out_specs=[pl.BlockSpec((2, bt, bc), lambda j, i: (0, i, j)), pl.BlockSpec((2, 4, bc), lambda j, i: (0, 0, j))],
        out_shape=[jax.ShapeDtypeStruct((2, L, F), BF16), jax.ShapeDtypeStruct((2, 4, F), F32)],
        compiler_params=_params(("parallel", "arbitrary"), 40 * bt * bc * 4),
    )(p, p, p, p, conv_w, conv_w, conv_b.reshape(1, F2), conv_b.reshape(1, F2), d_act)
    d_w = jnp.concatenate([dwb[0, 0:3], dwb[1, 0:3]], axis=1)
    d_b = jnp.concatenate([dwb[0, 3], dwb[1, 3]], axis=0)
    return da, d_w, d_b


def _conv_bwd_data(da, conv_w, name):
    _, L, F = da.shape
    bt, bc = _conv_tiles(L, F, 1088)
    nf, r, nt = F // bc, bt // HALO, L // bt

    def body(da_ref, nxt_ref, w_ref, dp_ref):
        last = pl.program_id(0) == nt - 1
        x = da_ref[...].astype(F32)
        after = jnp.where(last, 0.0, nxt_ref[...].astype(F32)[0:SUBLANES, :])
        x1, x2 = _shift_up(x, after, bt)
        w = w_ref[...]
        dp_ref[...] = (w[2:3, :] * x + w[1:2, :] * x1 + w[0:1, :] * x2).astype(BF16)

    return pl.pallas_call(
        body, name=name, grid=(nt, 2 * nf),
        in_specs=[pl.BlockSpec((None, bt, bc), lambda i, j: (j // nf, i, j % nf)),
                  pl.BlockSpec((None, HALO, bc), lambda i, j: (j // nf, jnp.minimum((i + 1) * r, nt * r - 1), j % nf)),
                  pl.BlockSpec((3, bc), lambda i, j: (0, j))],
        out_specs=pl.BlockSpec((bt, bc), lambda i, j: (i, j)), out_shape=jax.ShapeDtypeStruct((L, 2 * F), BF16),
        compiler_params=_params(("parallel", "parallel"), 16 * bt * bc * 4),
    )(da, da, conv_w)


def _pad_cols(a, n):
    return jnp.pad(a, ((0, 0), (0, n - a.shape[1])))


def _local_step(x, tgt, meta, W, hooks=None):
    S, D = x.shape
    n_meta = meta.shape[0]
    depth = len(W["w_in"])
    H = D // (2 * HEAD_DIM)
    WQ = 6 * H * HEAD_DIM
    L = -(-(S + n_meta) // ATT_BLOCK) * ATT_BLOCK
    tail = L - S - n_meta
    zeros_tail = jnp.zeros((tail, D), F32)
    h = jnp.concatenate([meta, x, zeros_tail], axis=0)
    tgt_p = jnp.concatenate([jnp.zeros((n_meta, D), F32), tgt, zeros_tail], axis=0)

    saved = []
    _, u1 = _resnorm_fwd(h, None, None, W["g_mix_pre"][0], "prenorm0")
    for l in range(depth):
        w_in = W["w_in"][l]
        w_qkv, w_f = w_in[:, :WQ], _pad_cols(w_in[:, WQ:], LANES)
        b_pad = jnp.pad(W["b_f"][l], (0, LANES - H)).reshape(1, LANES)
        g_heads = jnp.concatenate([W["g_sb"][l], W["g_fox"][l]], axis=0).reshape(2 * H, 1, HEAD_DIM)
        qkv = _mm(u1, w_qkv, "nn", BF16, f"qkv{l}")
        fl = _mm(u1, w_f, "nn", F32, f"flogit{l}")
        c = _gate_fwd(fl, b_pad, f"gate_fwd{l}")
        c_heads = c[:, :H].T
        crow, ccol = c_heads[:, None, :], c_heads[:, :, None]
        o, on, lse, brought = _attn_fwd(qkv, crow, ccol, g_heads, H, f"attn_fwd{l}", hooks and hooks.fwd_rider(l))
        if hooks:
            hooks.fwd_done(l, brought, W)
        mix = _mm(on, W["w_out"][l], "nn", F32, f"mix{l}")
        h_mid, u2 = _resnorm_fwd(h, mix, W["g_mix_post"][l], W["g_ffn_pre"][l], f"resnorm_a{l}")
        p = _mm(u2, W["w_up"][l], "nn", BF16, f"up{l}")
        act = _convgate_fwd(p, W["conv_w"][l], W["conv_b"][l], f"convgate{l}")
        ff = _mm(act, W["w_down"][l], "nn", F32, f"down{l}")
        g_next = W["g_mix_pre"][l + 1] if l + 1 < depth else None
        h_out, u1_next = _resnorm_fwd(h_mid, ff, W["g_ffn_post"][l], g_next, f"resnorm_b{l}")
        saved.append(dict(h_in=h, u1=u1, w_qkv=w_qkv, w_f=w_f, b_pad=b_pad, g_heads=g_heads, qkv=qkv, fl=fl, crow=crow,
                          ccol=ccol, o=o, on=on, lse=lse, mix=mix, h_mid=h_mid, u2=u2, p=p, act=act, ff=ff, h_out=h_out))
        h, u1 = h_out, u1_next

    loss, dh = _loss(h, tgt_p, n_meta, S, "loss")

    grads = {k: [None] * depth for k in ("g_mix_pre", "w_in", "b_f", "g_sb", "g_fox", "w_out", "g_mix_post", "g_ffn_pre",
                                         "w_up", "conv_w", "conv_b", "w_down", "g_ffn_post")}
    du1_next = None
    for l in reversed(range(depth)):
        s = saved[l]
        g_next = W["g_mix_pre"][l + 1] if l + 1 < depth else None
        dh, d_ff, grads["g_ffn_post"][l], dg_pre_next = _resnorm_bwd(
            dh, du1_next, s["h_out"], s["ff"], W["g_ffn_post"][l], g_next, f"resnorm_b_bwd{l}")
        if l + 1 < depth:
            grads["g_mix_pre"][l + 1] = dg_pre_next
        d_act = _mm(d_ff, W["w_down"][l], "nt", BF16, f"d_act{l}")
        grads["w_down"][l] = _mm(s["act"], d_ff, "tn", BF16, f"dw_down{l}")
        da, grads["conv_w"][l], grads["conv_b"][l] = _convgate_bwd(s["p"], W["conv_w"][l], W["conv_b"][l], d_act, f"convgate_bwd{l}")
        dp = _conv_bwd_data(da, W["conv_w"][l], f"conv_bwd{l}")
        du2 = _mm(dp, W["w_up"][l], "nt", F32, f"d_u2{l}")
        grads["w_up"][l] = _mm(s["u2"], dp, "tn", BF16, f"dw_up{l}")
        dh, d_mix, grads["g_mix_post"][l], grads["g_ffn_pre"][l] = _resnorm_bwd(
            dh, du2, s["h_mid"], s["mix"], W["g_mix_post"][l], W["g_ffn_pre"][l], f"resnorm_a_bwd{l}")
        d_on = _mm(d_mix, W["w_out"][l], "nt", F32, f"d_on{l}")
        grads["w_out"][l] = _mm(s["on"], d_mix, "tn", BF16, f"dw_out{l}")
        dq, dk, dv, dg_heads, dcrow, brought = _attn_bwd(s["qkv"], s["crow"], s["ccol"], s["g_heads"], s["o"], s["lse"], d_on, H,
                                                         f"attn_bwd{l}", hooks and hooks.bwd_rider(l, grads))
        if hooks:
            hooks.bwd_done(l, brought)
        grads["g_sb"][l], grads["g_fox"][l] = dg_heads[:H], dg_heads[H:]
        dc = _pad_cols(dcrow[H:, 0, :].T, LANES)
        dfl, db = _gate_bwd(dc, s["fl"], s["b_pad"], f"gate_bwd{l}")
        grads["b_f"][l] = db[:H]
        Wh = H * HEAD_DIM
        d_qkv = jnp.concatenate([dq[:, :Wh], dk[:, :Wh], dv[:, :Wh], dq[:, Wh:], dk[:, Wh:], dv[:, Wh:]], axis=1)
        du1_next = _mm(d_qkv, s["w_qkv"], "nt", F32, f"d_u1{l}", a2=dfl, b2=s["w_f"])
        dw_qkv = _mm(s["u1"], d_qkv, "tn", BF16, f"dw_qkv{l}")
        dw_f = _mm(s["u1"], dfl, "tn", BF16, f"dw_f{l}")
        grads["w_in"][l] = jnp.concatenate([dw_qkv, dw_f[:, :H]], axis=1)
    dh0, _, _, grads["g_mix_pre"][0] = _resnorm_bwd(dh, du1_next, saved[0]["h_in"], None, None, W["g_mix_pre"][0], "prenorm0_bwd")
    return loss, dh0[n_meta:n_meta + S], dh0[:n_meta], grads


def _all_gather(x, name):
    def body(x_ref, out_ref, send_sems, recv_sems, local_sem):
        mx, my, mc = lax.axis_index("x"), lax.axis_index("y"), lax.axis_index("c")
        me, sibling = (mx, my, mc), (mx, my, 1 - mc)
        chips = [(1 - mx, my), (mx, 1 - my), (1 - mx, 1 - my)]

        def slot(px, py, pc):
            return out_ref.at[4 * px + 2 * py + pc]

        def copy(k, block, to, src=None):
            return pltpu.make_async_remote_copy(
                src_ref=slot(*block) if src is None else src, dst_ref=slot(*block),
                send_sem=send_sems.at[k], recv_sem=recv_sems.at[k], device_id=to, device_id_type=_MESH)

        mine = pltpu.make_async_copy(x_ref, slot(*me), local_sem)
        mine.start()
        first = [copy(0, me, sibling, src=x_ref)]
        first += [copy(1 + j, me, (*chip, mc), src=x_ref) for j, chip in enumerate(chips)]
        for cp in first:
            cp.start()
        passed = [copy(4 + j, (*chip, mc), sibling) for j, chip in enumerate(chips)]
        for j, chip in enumerate(chips):
            copy(1 + j, (*chip, mc), me).wait_recv()
            passed[j].start()
        copy(0, sibling, me).wait_recv()
        for j, chip in enumerate(chips):
            copy(4 + j, (*chip, 1 - mc), me).wait_recv()
        for cp in first + passed:
            cp.wait_send()
        mine.wait()

    return pl.pallas_call(
        body, name=name, out_shape=jax.ShapeDtypeStruct((N_DEV, *x.shape), x.dtype),
        in_specs=[_ANY], out_specs=_ANY,
        scratch_shapes=[pltpu.SemaphoreType.DMA((7,)), pltpu.SemaphoreType.DMA((7,)), pltpu.SemaphoreType.DMA],
    )(x)


def _exchange(g, name):
    def body(g_ref, land_ref, send_sems, recv_sems, local_sem):
        mx, my, mc = lax.axis_index("x"), lax.axis_index("y"), lax.axis_index("c")
        me = 4 * mx + 2 * my + mc
        local = pltpu.make_async_copy(g_ref.at[me], land_ref.at[me], local_sem)
        local.start()
        sends, recvs = [], []
        for k in range(1, N_DEV):
            peer = (_flip(mx, k & 4), _flip(my, k & 2), _flip(mc, k & 1))
            p = 4 * peer[0] + 2 * peer[1] + peer[2]
            sends.append(pltpu.make_async_remote_copy(
                src_ref=g_ref.at[p], dst_ref=land_ref.at[me], send_sem=send_sems.at[k - 1], recv_sem=recv_sems.at[k - 1],
                device_id=peer, device_id_type=_MESH))
            recvs.append(pltpu.make_async_remote_copy(
                src_ref=g_ref.at[p], dst_ref=land_ref.at[p], send_sem=send_sems.at[k - 1], recv_sem=recv_sems.at[k - 1],
                device_id=peer, device_id_type=_MESH))
        for cp in sends:
            cp.start()
        for cp in recvs:
            cp.wait_recv()
        for cp in sends:
            cp.wait_send()
        local.wait()

    return pl.pallas_call(
        body, name=name, out_shape=jax.ShapeDtypeStruct(g.shape, g.dtype), in_specs=[_ANY], out_specs=_ANY,
        scratch_shapes=[pltpu.SemaphoreType.DMA((7,)), pltpu.SemaphoreType.DMA((7,)), pltpu.SemaphoreType.DMA],
    )(g)


def _exchange_cores(g, name):
    def body(g_ref, land_ref, send_sems, recv_sems):
        mx, my, mc = lax.axis_index("x"), lax.axis_index("y"), lax.axis_index("c")
        copies = [pltpu.make_async_remote_copy(
            src_ref=g_ref.at[2 * q + (1 - mc)], dst_ref=land_ref.at[q], send_sem=send_sems.at[q], recv_sem=recv_sems.at[q],
            device_id=(mx, my, 1 - mc), device_id_type=_MESH) for q in range(4)]
        for cp in copies:
            cp.start()
        for cp in copies:
            cp.wait_recv()
        for cp in copies:
            cp.wait_send()

    return pl.pallas_call(
        body, name=name, out_shape=jax.ShapeDtypeStruct((4, *g.shape[1:]), g.dtype), in_specs=[_ANY], out_specs=_ANY,
        scratch_shapes=[pltpu.SemaphoreType.DMA((4,)), pltpu.SemaphoreType.DMA((4,))],
    )(g)


def _pair_sum(g, land, name):
    _, n, R, C = g.shape
    tr = _divisor(R, (128, 64, 32, 16, 8))
    core = lax.axis_index("c").astype(jnp.int32).reshape(1)

    def body(c_ref, g_ref, land_ref, o_ref):
        o_ref[...] = (g_ref[...].astype(F32) + land_ref[...].astype(F32)).astype(o_ref.dtype)

    blk = pl.BlockSpec((None, None, tr, C), lambda q, l, r, c_ref: (q, l, r, 0))
    return pl.pallas_call(
        body, name=name, out_shape=jax.ShapeDtypeStruct(land.shape, g.dtype),
        grid_spec=pltpu.PrefetchScalarGridSpec(
            num_scalar_prefetch=1, grid=(4, n, R // tr),
            in_specs=[pl.BlockSpec((None, None, tr, C), lambda q, l, r, c_ref: (2 * q + c_ref[0], l, r, 0)), blk],
            out_specs=blk),
        compiler_params=_params(("parallel", "parallel", "parallel"), 8 * tr * C * 4),
    )(core, g, land)


def _exchange_chips(part, name):
    def body(p_ref, land_ref, send_sems, recv_sems, local_sem):
        mx, my, mc = lax.axis_index("x"), lax.axis_index("y"), lax.axis_index("c")
        me = 2 * mx + my
        local = pltpu.make_async_copy(p_ref.at[me], land_ref.at[me], local_sem)
        local.start()
        sends, recvs = [], []
        for k in range(1, 4):
            px, py = _flip(mx, k & 2), _flip(my, k & 1)
            p = 2 * px + py
            sends.append(pltpu.make_async_remote_copy(
                src_ref=p_ref.at[p], dst_ref=land_ref.at[me], send_sem=send_sems.at[k - 1], recv_sem=recv_sems.at[k - 1],
                device_id=(px, py, mc), device_id_type=_MESH))
            recvs.append(pltpu.make_async_remote_copy(
                src_ref=p_ref.at[p], dst_ref=land_ref.at[p], send_sem=send_sems.at[k - 1], recv_sem=recv_sems.at[k - 1],
                device_id=(px, py, mc), device_id_type=_MESH))
        for cp in sends:
            cp.start()
        for cp in recvs:
            cp.wait_recv()
        for cp in sends:
            cp.wait_send()
        local.wait()

    return pl.pallas_call(
        body, name=name, out_shape=jax.ShapeDtypeStruct(part.shape, part.dtype), in_specs=[_ANY], out_specs=_ANY,
        scratch_shapes=[pltpu.SemaphoreType.DMA((3,)), pltpu.SemaphoreType.DMA((3,)), pltpu.SemaphoreType.DMA],
    )(part)


def _adamw_landed(lands, w, m, v, name):
    n, R, C = w.shape
    tr = _divisor(R, (128, 64, 32, 16, 8))
    nr = R // tr
    blk = pl.BlockSpec((None, tr, C), lambda l, r: (l, r, 0))

    def land_spec(k, S):
        return pl.BlockSpec((S, tr, C), lambda l, r: (0, jnp.where(l == k, r, jnp.where(l < k, 0, nr - 1)), 0))

    def body(*refs):
        land_refs = refs[:n]
        w_ref, m_ref, v_ref, g_ref, d_ref, mo_ref, vo_ref = refs[n:]
        for k in range(n):
            @pl.when(pl.program_id(0) == k)
            def _(k=k):
                g = land_refs[k][0].astype(F32)
                for s in range(1, lands[k].shape[0]):
                    g = g + land_refs[k][s].astype(F32)
                m_new = ADAM_B1 * m_ref[...] + (1.0 - ADAM_B1) * g
                v_new = ADAM_B2 * v_ref[...] + (1.0 - ADAM_B2) * (g * g)
                m_hat = m_new / (1.0 - ADAM_B1 ** ADAM_STEP)
                v_hat = v_new / (1.0 - ADAM_B2 ** ADAM_STEP)
                g_ref[...] = g
                d_ref[...] = -ADAM_LR * (m_hat / (jnp.sqrt(v_hat) + ADAM_EPS) + ADAM_WD * w_ref[...])
                mo_ref[...] = m_new
                vo_ref[...] = v_new

    est = 2 * tr * C * (sum(x.shape[0] * jnp.dtype(x.dtype).itemsize for x in lands) + 7 * 4) * 9 // 8
    return pl.pallas_call(
        body, name=name, grid=(n, nr),
        in_specs=[land_spec(k, x.shape[0]) for k, x in enumerate(lands)] + [blk, blk, blk],
        out_specs=[blk] * 4, out_shape=[jax.ShapeDtypeStruct((n, R, C), F32)] * 4,
        compiler_params=_params(("arbitrary", "arbitrary"), est),
    )(*lands, w, m, v)


def _pack(arrs):
    flat = jnp.concatenate([a.reshape(-1).astype(F32) for a in arrs])
    rows = -(-flat.shape[0] // (8 * LANES)) * 8
    return jnp.pad(flat, (0, rows * LANES - flat.shape[0])).reshape(rows, LANES)


def _unpack(packed, shapes):
    flat, out, at = packed.reshape(-1), [], 0
    for s in shapes:
        n = math.prod(s)
        out.append(flat[at:at + n].reshape(s))
        at += n
    return out


_BIG = ("w_in", "w_out", "w_up", "w_down")
_REPLICATED = ("g_mix_pre", "b_f", "g_sb", "g_fox", "g_mix_post", "g_ffn_pre", "conv_b", "g_ffn_post")
_ORDER = ("meta", "g_mix_pre", "w_in", "b_f", "g_sb", "g_fox", "w_out", "g_mix_post", "g_ffn_pre", "w_up", "conv_w",
          "conv_b", "w_down", "g_ffn_post")
_COLUMN_SHARDED = ("w_in", "w_up")


def kernel(x, meta, g_mix_pre, w_in, b_f, g_sb, g_fox, w_out, g_mix_post, g_ffn_pre, w_up, conv_w, conv_b, w_down, g_ffn_post, loss_target, m_meta, m_g_mix_pre, m_w_in, m_b_f, m_g_sb, m_g_fox, m_w_out, m_g_mix_post, m_g_ffn_pre, m_w_up, m_conv_w, m_conv_b, m_w_down, m_g_ffn_post, v_meta, v_g_mix_pre, v_w_in, v_b_f, v_g_sb, v_g_fox, v_w_out, v_g_mix_post, v_g_ffn_pre, v_w_up, v_conv_w, v_conv_b, v_w_down, v_g_ffn_post):
    w = dict(meta=meta, g_mix_pre=g_mix_pre, w_in=w_in, b_f=b_f, g_sb=g_sb, g_fox=g_fox, w_out=w_out, g_mix_post=g_mix_post,
             g_ffn_pre=g_ffn_pre, w_up=w_up, conv_w=conv_w, conv_b=conv_b, w_down=w_down, g_ffn_post=g_ffn_post)
    m = dict(meta=m_meta, g_mix_pre=m_g_mix_pre, w_in=m_w_in, b_f=m_b_f, g_sb=m_g_sb, g_fox=m_g_fox, w_out=m_w_out,
             g_mix_post=m_g_mix_post, g_ffn_pre=m_g_ffn_pre, w_up=m_w_up, conv_w=m_conv_w, conv_b=m_conv_b, w_down=m_w_down,
             g_ffn_post=m_g_ffn_post)
    v = dict(meta=v_meta, g_mix_pre=v_g_mix_pre, w_in=v_w_in, b_f=v_b_f, g_sb=v_g_sb, g_fox=v_g_fox, w_out=v_w_out,
             g_mix_post=v_g_mix_post, g_ffn_pre=v_g_ffn_pre, w_up=v_w_up, conv_w=v_conv_w, conv_b=v_conv_b, w_down=v_w_down,
             g_ffn_post=v_g_ffn_post)
    depth = w_in.shape[0]
    shards = {name: [w[name][l].astype(BF16) for l in range(depth)] for name in _BIG}

    def assemble(name, g):
        if name in _COLUMN_SHARDED:
            return jnp.transpose(g, (1, 0, 2)).reshape(g.shape[1], -1)
        return g.reshape(-1, g.shape[2])

    def blocks_of(name, grad):
        R, C = w[name].shape[1:]
        if name in _COLUMN_SHARDED:
            return jnp.transpose(grad.reshape(R, N_DEV, C), (1, 0, 2)).astype(BF16)
        return grad.reshape(N_DEV, R, C).astype(BF16)

    def side_by_side(name):
        return name in _COLUMN_SHARDED and w[name].shape[2] % LANES == 0

    class Hooks:
        def __init__(self):
            self.landed = {}

        def fwd_keys(self, l):
            return ([("w_in", l + 1)] if l + 1 < depth else []) + [("w_out", l), ("w_up", l), ("w_down", l)]

        def bwd_keys(self, l):
            return ([("w_in", l + 1)] if l + 1 < depth else []) + [("w_down", l), ("w_up", l), ("w_out", l)]

        def fwd_rider(self, l):
            keys = self.fwd_keys(l)
            return _GatherRider([shards[name][ll] for name, ll in keys], [side_by_side(name) for name, _ in keys])

        def fwd_done(self, l, brought, W):
            for (name, ll), g in zip(self.fwd_keys(l), brought):
                W[name][ll] = g if side_by_side(name) else assemble(name, g)

        def bwd_rider(self, l, grads):
            keys = self.bwd_keys(l)
            return _ExchangeRider([grads[name][ll] if side_by_side(name) else blocks_of(name, grads[name][ll]) for name, ll in keys],
                                  [side_by_side(name) for name, _ in keys])

        def bwd_done(self, l, brought):
            for key, land in zip(self.bwd_keys(l), brought):
                self.landed[key] = land

    small_shapes = [conv_w.shape, meta.shape]
    gs = _all_gather(_pack([conv_w, meta]), "gather_small")
    parts = [_unpack(gs[d], small_shapes) for d in range(N_DEV)]
    conv_full = jnp.concatenate([p[0] for p in parts], axis=2)
    meta_full = jnp.concatenate([p[1] for p in parts], axis=1)

    W = {name: [None] * depth for name in _BIG}
    W["w_in"][0] = assemble("w_in", _all_gather(shards["w_in"][0], "gather_w_in0"))
    W["conv_w"] = [conv_full[l] for l in range(depth)]
    for name in _REPLICATED:
        W[name] = [w[name][l] for l in range(depth)]
    hooks = Hooks()
    loss, grad_x, d_meta, grads = _local_step(x[0], loss_target[0], meta_full, W, hooks)

    first = blocks_of("w_in", grads["w_in"][0])[:, None]
    pairs = _pair_sum(first, _exchange_cores(first, "exchange_cores_w_in0"), "pair_sum_w_in0")
    hooks.landed[("w_in", 0)] = _exchange_chips(pairs, "exchange_chips_w_in0")[:, 0]
    out = {}
    for name in _BIG:
        out[name] = _adamw_landed([hooks.landed[(name, l)] for l in range(depth)], w[name], m[name], v[name], f"adamw_{name}")

    Fs, Ms = conv_w.shape[2], meta.shape[1]
    d_conv = jnp.stack(grads["conv_w"], axis=0)
    blocks = jnp.stack([_pack([d_conv[:, :, d * Fs:(d + 1) * Fs], d_meta[:, d * Ms:(d + 1) * Ms]]) for d in range(N_DEV)])
    land = _exchange(blocks, "exchange_small")
    res = _adamw_landed([land], _pack([conv_w, meta])[None], _pack([m_conv_w, m_meta])[None],
                        _pack([v_conv_w, v_meta])[None], "adamw_small")
    for i, r in enumerate(res):
        cw, mt = _unpack(r[0], small_shapes)
        out.setdefault("conv_w", [None] * 4)[i] = cw
        out.setdefault("meta", [None] * 4)[i] = mt

    rep_shapes = [()] + [w[name].shape for name in _REPLICATED]
    mine = _pack([loss] + [jnp.stack(grads[name], axis=0) for name in _REPLICATED])
    land = _all_gather(mine, "gather_replicated")
    zero = jnp.zeros((), F32)
    res = _adamw_landed([land], _pack([zero] + [w[n] for n in _REPLICATED])[None],
                        _pack([zero] + [m[n] for n in _REPLICATED])[None],
                        _pack([zero + 1.0] + [v[n] for n in _REPLICATED])[None], "adamw_replicated")
    for i, r in enumerate(res):
        vals = _unpack(r[0], rep_shapes)
        if i == 0:
            loss_total = vals[0]
        for name, val in zip(_REPLICATED, vals[1:]):
            out.setdefault(name, [None] * 4)[i] = val

    return (loss_total, grad_x[None], *[out[n][0] for n in _ORDER], *[out[n][1] for n in _ORDER],
            *[out[n][2] for n in _ORDER], *[out[n][3] for n in _ORDER])
```
